```python
import math
import jax
import jax.numpy as jnp
from jax import lax
import numpy as np

D_MODEL = 1024
BATCH = 4
SEQ = 4096
DEPTH = 4
DEC_BATCH = 128
DEC_SEQ = 1
PAST_LEN = 8192
PAGE_SIZE = 128

ATT_HEADS = 8
ATT_KV_HEADS = 2
HEAD_DIM = 64
WINDOW = 128
ATT_BLOCK = 128
M_INNER = D_MODEL // 2
M_HEADDIM = 64
M_HEADS = M_INNER // M_HEADDIM
M_GROUPS = 2
M_STATE = 64
CONV_W = 4
CONV_DIM = M_INNER + 2 * M_GROUPS * M_STATE
M_CHUNK = 128
HG_WIDTH = D_MODEL // 2
HG_HEADS = 4
HG_DK = HG_WIDTH // HG_HEADS
HG_DV = HG_WIDTH // HG_HEADS
HG_CHUNK = 64
D_FF = 2816
N_BRANCH = 3
ALPHA = (2.0 * DEPTH) ** 0.25
BETA = (8.0 * DEPTH) ** -0.25
LN_EPS = 1e-5
RMS_EPS = 1e-6
COL_SPLITS = (ATT_HEADS * HEAD_DIM, ATT_KV_HEADS * HEAD_DIM, ATT_KV_HEADS * HEAD_DIM,
              M_INNER, CONV_DIM, M_HEADS,
              HG_WIDTH, HG_WIDTH, HG_WIDTH, HG_WIDTH,
              N_BRANCH * D_MODEL)
N_IN = sum(COL_SPLITS)

kernel_name = 'hybrid_swa_ssd_hgrn2_deepnorm_step'


def _layernorm(x, g, b):
    xf = x.astype(jnp.float32)
    mu = jnp.mean(xf, axis=-1, keepdims=True)
    var = jnp.mean(jnp.square(xf - mu), axis=-1, keepdims=True)
    return ((xf - mu) * lax.rsqrt(var + LN_EPS) * g.astype(jnp.float32) + b.astype(jnp.float32)).astype(x.dtype)


def _rmsnorm(x, w):
    xf = x.astype(jnp.float32)
    return (xf * lax.rsqrt(jnp.mean(xf * xf, axis=-1, keepdims=True) + RMS_EPS) * w.astype(jnp.float32)).astype(x.dtype)


def _swiglu(x, wg, wu, wd):
    return (jax.nn.silu(x @ wg) * (x @ wu)) @ wd


def _blk(T, C):
    return C if T % C == 0 else T


def _split_cols(h):
    out, start = [], 0
    for size in COL_SPLITS:
        out.append(h[..., start:start + size])
        start += size
    return out


def _swa_attention(q, k, v, k_prev, v_prev, n_prev_valid, sinks):
    b, T = q.shape[0], q.shape[1]
    blk = _blk(T, ATT_BLOCK)
    nb = T // blk
    grp = ATT_HEADS // ATT_KV_HEADS
    kc = jnp.concatenate([k_prev.astype(k.dtype), k], axis=1)
    vc = jnp.concatenate([v_prev.astype(v.dtype), v], axis=1)
    idx = jnp.arange(nb)[:, None] * blk + jnp.arange(blk + WINDOW)[None, :]
    kb = kc[:, idx]
    vb = vc[:, idx]
    qb = q.reshape(b, nb, blk, ATT_KV_HEADS, grp, HEAD_DIM)
    s = jnp.einsum('bnqkgd,bnskd->bnkgqs', qb, kb).astype(jnp.float32) * (HEAD_DIM ** -0.5)
    dist = WINDOW + jnp.arange(blk)[:, None] - jnp.arange(blk + WINDOW)[None, :]
    valid = (dist >= 0) & (dist <= WINDOW) & (idx[:, None, :] >= WINDOW - n_prev_valid)
    slopes = (2.0 ** (-8.0 * jnp.arange(1, ATT_HEADS + 1, dtype=jnp.float32) / ATT_HEADS)).reshape(ATT_KV_HEADS, grp, 1, 1)
    s = jnp.where(valid[None, :, None, None], s - slopes * dist.astype(jnp.float32), -jnp.inf)
    sink = sinks.astype(jnp.float32).reshape(ATT_KV_HEADS, grp, 1, 1)
    m = jnp.maximum(jnp.max(s, axis=-1, keepdims=True), sink)
    p = jnp.exp(s - m)
    p = p / (jnp.sum(p, axis=-1, keepdims=True) + jnp.exp(sink - m))
    o = jnp.einsum('bnkgqs,bnskd->bnqkgd', p.astype(v.dtype), vb)
    return o.reshape(b, T, ATT_HEADS * HEAD_DIM), kc[:, -WINDOW:], vc[:, -WINDOW:]


def _ssd(xh, dt, A, Bm, Cm, h0):
    b, T = xh.shape[0], xh.shape[1]
    L = _blk(T, M_CHUNK)
    nc = T // L
    hpg = M_HEADS // M_GROUPS
    x = xh.astype(jnp.float32).reshape(b, nc, L, M_GROUPS, hpg, M_HEADDIM)
    dtc = dt.reshape(b, nc, L, M_GROUPS, hpg)
    Bc = Bm.astype(jnp.float32).reshape(b, nc, L, M_GROUPS, M_STATE)
    Cc = Cm.astype(jnp.float32).reshape(b, nc, L, M_GROUPS, M_STATE)
    a = jnp.cumsum(dtc * A.reshape(M_GROUPS, hpg), axis=2)
    causal = jnp.tril(jnp.ones((L, L), bool))[:, :, None, None]
    decay = jnp.exp(jnp.where(causal, a[:, :, :, None] - a[:, :, None], -jnp.inf))
    w = jnp.einsum('bctgn,bcsgn->bctsg', Cc, Bc)[..., None] * decay * dtc[:, :, None]
    y = jnp.einsum('bctsgh,bcsghp->bctghp', w, x)
    cs = jnp.einsum('bcsgh,bcsgn,bcsghp->bcghpn', jnp.exp(a[:, :, -1:] - a) * dtc, Bc, x)

    def step(hc, inp):
        dec, s_c = inp
        return dec[..., None, None] * hc + s_c, hc

    hT, hs = lax.scan(step, h0.astype(jnp.float32).reshape(b, M_GROUPS, hpg, M_HEADDIM, M_STATE),
                      (jnp.moveaxis(jnp.exp(a[:, :, -1]), 1, 0), jnp.moveaxis(cs, 1, 0)))
    y = y + jnp.einsum('bctgn,cbghpn->bctghp', Cc, hs) * jnp.exp(a)[..., None]
    return y.reshape(b, T, M_HEADS, M_HEADDIM), hT.reshape(b, M_HEADS, M_HEADDIM, M_STATE)


def _mamba2(z, xbc, dt_raw, conv_prev, h0, conv_w, conv_b, dt_bias, a_log, d_skip, norm_w):
    b, T = xbc.shape[0], xbc.shape[1]
    xcat = jnp.concatenate([conv_prev.astype(xbc.dtype), xbc], axis=1)
    conv = lax.conv_general_dilated(xcat, conv_w.astype(xbc.dtype)[:, None, :], window_strides=(1,),
                                    padding='VALID', dimension_numbers=('NWC', 'WIO', 'NWC'),
                                    feature_group_count=CONV_DIM)
    act = jax.nn.silu(conv + conv_b)
    xs = act[..., :M_INNER].reshape(b, T, M_HEADS, M_HEADDIM)
    Bm = act[..., M_INNER:M_INNER + M_GROUPS * M_STATE].reshape(b, T, M_GROUPS, M_STATE)
    Cm = act[..., M_INNER + M_GROUPS * M_STATE:].reshape(b, T, M_GROUPS, M_STATE)
    dt = jax.nn.softplus(dt_raw.astype(jnp.float32) + dt_bias.astype(jnp.float32))
    A = -jnp.exp(a_log.astype(jnp.float32))
    y, hT = _ssd(xs, dt, A, Bm, Cm, h0)
    y = y + d_skip.astype(jnp.float32)[:, None] * xs.astype(jnp.float32)
    y = y.reshape(b, T, M_INNER) * jax.nn.silu(z.astype(jnp.float32))
    return _rmsnorm(y, norm_w).astype(z.dtype), xcat[:, -(CONV_W - 1):], hT


def _hgrn2(q, f_logit, i, lb, S0):
    b, T = q.shape[0], q.shape[1]
    qh = jax.nn.silu(q.astype(jnp.float32)).reshape(b, T, HG_HEADS, HG_DK)
    fl = f_logit.astype(jnp.float32).reshape(b, T, HG_HEADS, HG_DK)
    lbh = lb.reshape(HG_HEADS, HG_DK)
    log_f = jnp.logaddexp(jnp.log(lbh), jnp.log1p(-lbh) + jax.nn.log_sigmoid(fl))
    kh = (1.0 - lbh) * jax.nn.sigmoid(-fl)
    ih = i.astype(jnp.float32).reshape(b, T, HG_HEADS, HG_DV)
    L = _blk(T, HG_CHUNK)
    nc = T // L

    def chunks(arr):
        return jnp.moveaxis(arr.reshape(b, nc, L, HG_HEADS, arr.shape[-1]), 1, 0)

    causal = jnp.tril(jnp.ones((L, L), bool))[None, :, :, None, None]

    def step(S, inp):
        qc, kc, ic, gc = inp
        bc = jnp.cumsum(gc, axis=1)
        dec = jnp.exp(jnp.where(causal, bc[:, :, None] - bc[:, None, :], -jnp.inf))
        att = jnp.einsum('bthk,btshk->bhts', qc, dec * kc[:, None])
        o = jnp.einsum('bhts,bshv->bthv', att, ic) + jnp.einsum('bthk,bhkv->bthv', qc * jnp.exp(bc), S)
        S = jnp.exp(bc[:, -1])[..., None] * S + jnp.einsum('bshk,bshv->bhkv', kc * jnp.exp(bc[:, -1:] - bc), ic)
        return S, o

    S_T, o = lax.scan(step, S0.astype(jnp.float32), (chunks(qh), chunks(kh), chunks(ih), chunks(log_f)))
    return jnp.moveaxis(o, 0, 1).reshape(b, T, HG_HEADS, HG_DV), S_T


def _mixer(x, prev, n_prev_valid, lb, p):
    k_prev, v_prev, conv_prev, ssm_prev, hg_prev = prev
    b, T = x.shape[0], x.shape[1]
    h = x @ p['w_in'] + p['b_in']
    q, k, v, mz, mxbc, mdt, hq, hf, hi, hg, gates = _split_cols(h)
    ya, nk, nv = _swa_attention(q, k.reshape(b, T, ATT_KV_HEADS, HEAD_DIM), v.reshape(b, T, ATT_KV_HEADS, HEAD_DIM),
                                k_prev, v_prev, n_prev_valid, p['att_sinks'])
    ym, nconv, nssm = _mamba2(mz, mxbc, mdt, conv_prev, ssm_prev, p['conv_w'], p['conv_b'], p['dt_bias'],
                              p['a_log'], p['d_skip'], p['ssm_norm_w'])
    oh, nhg = _hgrn2(hq, hf, hi, lb, hg_prev)
    yh = (_rmsnorm(oh, p['hg_norm_w']).reshape(b, T, HG_WIDTH) * jax.nn.silu(hg.astype(jnp.float32))).astype(x.dtype)
    g = jax.nn.sigmoid(gates)
    g_att, g_ssm, g_hg = g[..., :D_MODEL], g[..., D_MODEL:2 * D_MODEL], g[..., 2 * D_MODEL:]
    merged = g_att * (ya @ p['w_br_att']) + g_ssm * (ym @ p['w_br_ssm']) + g_hg * (yh @ p['w_br_hg'])
    return merged @ p['w_out'], (nk, nv, nconv, nssm.astype(x.dtype), nhg.astype(x.dtype))


def _layer(x, prev, n_prev_valid, lb, p):
    x = _layernorm(ALPHA * x + 0.5 * _swiglu(x, p['ffn1_wg'], p['ffn1_wu'], p['ffn1_wd']), p['ln1_g'], p['ln1_b'])
    y, new_state = _mixer(x, prev, n_prev_valid, lb, p)
    x = _layernorm(ALPHA * x + y, p['ln2_g'], p['ln2_b'])
    x = _layernorm(ALPHA * x + 0.5 * _swiglu(x, p['ffn2_wg'], p['ffn2_wu'], p['ffn2_wd']), p['ln3_g'], p['ln3_b'])
    return x, new_state


def setup_inputs(seed: int = 0) -> dict:
    key = jax.random.key(seed)
    keys = iter(jax.random.split(key, 48))

    def nrm(shape, scale):
        return scale * jax.random.normal(next(keys), shape, jnp.float32)

    def gain(shape):
        return 1.0 + nrm(shape, 0.1)

    att_w = ATT_HEADS * HEAD_DIM
    inp = {}
    inp['x_prompt'] = nrm((BATCH, SEQ, D_MODEL), 1.0)
    inp['x_sample'] = nrm((DEC_BATCH, DEC_SEQ, D_MODEL), 1.0)
    inp['cache_swa_k'] = nrm((DEPTH, DEC_BATCH, WINDOW, ATT_KV_HEADS, HEAD_DIM), 1.0)
    inp['cache_swa_v'] = nrm((DEPTH, DEC_BATCH, WINDOW, ATT_KV_HEADS, HEAD_DIM), 1.0)
    inp['state_conv'] = nrm((DEPTH, DEC_BATCH, CONV_W - 1, CONV_DIM), 1.0)
    inp['state_ssm'] = nrm((DEPTH, DEC_BATCH, M_HEADS, M_HEADDIM, M_STATE), 0.5)
    inp['state_hgrn'] = nrm((DEPTH, DEC_BATCH, HG_HEADS, HG_DK, HG_DV), 0.5)
    inp['ln1_g'] = gain((DEPTH, D_MODEL))
    inp['ln1_b'] = nrm((DEPTH, D_MODEL), 0.02)
    inp['ffn1_wg'] = nrm((DEPTH, D_MODEL, D_FF), D_MODEL ** -0.5)
    inp['ffn1_wu'] = nrm((DEPTH, D_MODEL, D_FF), D_MODEL ** -0.5)
    inp['ffn1_wd'] = nrm((DEPTH, D_FF, D_MODEL), BETA * D_FF ** -0.5)
    inp['w_in'] = nrm((DEPTH, D_MODEL, N_IN), D_MODEL ** -0.5)
    inp['b_in'] = nrm((DEPTH, N_IN), 0.02)
    inp['att_sinks'] = nrm((DEPTH, ATT_HEADS), 0.5)
    inp['conv_w'] = nrm((DEPTH, CONV_W, CONV_DIM), CONV_W ** -0.5)
    inp['conv_b'] = nrm((DEPTH, CONV_DIM), 0.02)
    dt0 = jnp.exp(jax.random.uniform(next(keys), (DEPTH, M_HEADS), jnp.float32, math.log(1e-3), math.log(1e-1)))
    inp['dt_bias'] = dt0 + jnp.log(-jnp.expm1(-dt0))
    inp['a_log'] = jnp.log(jax.random.uniform(next(keys), (DEPTH, M_HEADS), jnp.float32, 1.0, 16.0))
    inp['d_skip'] = gain((DEPTH, M_HEADS))
    inp['ssm_norm_w'] = gain((DEPTH, M_INNER))
    inp['hg_lb_logits'] = nrm((DEPTH, HG_WIDTH), 1.0)
    inp['hg_norm_w'] = gain((DEPTH, HG_DV))
    inp['w_br_att'] = nrm((DEPTH, att_w, D_MODEL), BETA * att_w ** -0.5)
    inp['w_br_ssm'] = nrm((DEPTH, M_INNER, D_MODEL), BETA * M_INNER ** -0.5)
    inp['w_br_hg'] = nrm((DEPTH, HG_WIDTH, D_MODEL), BETA * HG_WIDTH ** -0.5)
    inp['w_out'] = nrm((DEPTH, D_MODEL, D_MODEL), BETA * D_MODEL ** -0.5)
    inp['ln2_g'] = gain((DEPTH, D_MODEL))
    inp['ln2_b'] = nrm((DEPTH, D_MODEL), 0.02)
    inp['ffn2_wg'] = nrm((DEPTH, D_MODEL, D_FF), D_MODEL ** -0.5)
    inp['ffn2_wu'] = nrm((DEPTH, D_MODEL, D_FF), D_MODEL ** -0.5)
    inp['ffn2_wd'] = nrm((DEPTH, D_FF, D_MODEL), BETA * D_FF ** -0.5)
    inp['ln3_g'] = gain((DEPTH, D_MODEL))
    inp['ln3_b'] = nrm((DEPTH, D_MODEL), 0.02)
    return inp


def reference(x_prompt, x_sample, cache_swa_k, cache_swa_v, state_conv, state_ssm, state_hgrn,
              ln1_g, ln1_b, ffn1_wg, ffn1_wu, ffn1_wd, w_in, b_in, att_sinks, conv_w, conv_b,
              dt_bias, a_log, d_skip, ssm_norm_w, hg_lb_logits, hg_norm_w, w_br_att, w_br_ssm,
              w_br_hg, w_out, ln2_g, ln2_b, ffn2_wg, ffn2_wu, ffn2_wd, ln3_g, ln3_b):
    lb_all = jnp.cumsum(jax.nn.softmax(hg_lb_logits.astype(jnp.float32), axis=0), axis=0)
    lb_all = lb_all - lb_all[0]
    bp, dtype = x_prompt.shape[0], x_prompt.dtype
    zero_prev = (jnp.zeros((bp, WINDOW, ATT_KV_HEADS, HEAD_DIM), dtype),
                 jnp.zeros((bp, WINDOW, ATT_KV_HEADS, HEAD_DIM), dtype),
                 jnp.zeros((bp, CONV_W - 1, CONV_DIM), dtype),
                 jnp.zeros((bp, M_HEADS, M_HEADDIM, M_STATE), dtype),
                 jnp.zeros((bp, HG_HEADS, HG_DK, HG_DV), dtype))
    n_past_valid = min(WINDOW, PAST_LEN)
    y_prompt, y_sample = x_prompt, x_sample
    p_states, s_states = [], []
    for l in range(DEPTH):
        p = {'ln1_g': ln1_g[l], 'ln1_b': ln1_b[l], 'ffn1_wg': ffn1_wg[l], 'ffn1_wu': ffn1_wu[l],
             'ffn1_wd': ffn1_wd[l], 'w_in': w_in[l], 'b_in': b_in[l], 'att_sinks': att_sinks[l],
             'conv_w': conv_w[l], 'conv_b': conv_b[l], 'dt_bias': dt_bias[l], 'a_log': a_log[l],
             'd_skip': d_skip[l], 'ssm_norm_w': ssm_norm_w[l], 'hg_norm_w': hg_norm_w[l],
             'w_br_att': w_br_att[l], 'w_br_ssm': w_br_ssm[l], 'w_br_hg': w_br_hg[l], 'w_out': w_out[l],
             'ln2_g': ln2_g[l], 'ln2_b': ln2_b[l], 'ffn2_wg': ffn2_wg[l], 'ffn2_wu': ffn2_wu[l],
             'ffn2_wd': ffn2_wd[l], 'ln3_g': ln3_g[l], 'ln3_b': ln3_b[l]}
        y_prompt, ps = _layer(y_prompt, zero_prev, 0, lb_all[l], p)
        s_prev = (cache_swa_k[l], cache_swa_v[l], state_conv[l], state_ssm[l], state_hgrn[l])
        y_sample, ss = _layer(y_sample, s_prev, n_past_valid, lb_all[l], p)
        p_states.append(ps)
        s_states.append(ss)
    p_swa_k, p_swa_v, p_conv, p_ssm, p_hgrn = [jnp.stack(t) for t in zip(*p_states)]
    s_swa_k, s_swa_v, s_conv, s_ssm, s_hgrn = [jnp.stack(t) for t in zip(*s_states)]
    return (y_prompt, y_sample, p_swa_k, p_swa_v, p_conv, p_ssm, p_hgrn, s_swa_k, s_swa_v, s_conv, s_ssm, s_hgrn)
```

```python
import functools

import numpy as np
import jax
import jax.numpy as jnp
from jax import lax
from jax.experimental import pallas as pl
from jax.experimental.pallas import tpu as pltpu

F32 = jnp.float32
BF16 = jnp.bfloat16

ATT_HEADS = 8
ATT_KV_HEADS = 2
HEAD_DIM = 64
WINDOW = 128
PAST_LEN = 8192
M_HEADS = 8
M_HEADDIM = 64
M_GROUPS = 2
M_STATE = 64
CONV_W = 4
HG_HEADS = 4
DEPTH = 4
ALPHA = (2.0 * DEPTH) ** 0.25
LN_EPS = 1e-5
RMS_EPS = 1e-6
CHUNK = 128

V7X_VMEM_BYTES = 64 * 1024 * 1024
LANES = 128
SUBLANES = 8
MXU_WIDTH = 256

NT_DIMS = (((1,), (1,)), ((), ()))
TN_DIMS = (((0,), (0,)), ((), ()))


def _vmem_limit(nbytes):
    return int(min(V7X_VMEM_BYTES - 8 * 1024 * 1024, nbytes + 16 * 1024 * 1024))


def _params(semantics, vmem_bytes):
    return pltpu.CompilerParams(dimension_semantics=semantics, vmem_limit_bytes=_vmem_limit(vmem_bytes))


def _resident(block_shape, index_map):
    return pl.BlockSpec(block_shape, index_map, pipeline_mode=pl.Buffered(1))


def _silu(x):
    return x * jax.nn.sigmoid(x)


def _softplus(x):
    return jnp.maximum(x, 0.0) + jnp.log1p(jnp.exp(-jnp.abs(x)))


def _log_sigmoid(x):
    return jnp.minimum(x, 0.0) - jnp.log1p(jnp.exp(-jnp.abs(x)))


def _logaddexp(a, b):
    return jnp.maximum(a, b) + jnp.log1p(jnp.exp(-jnp.abs(a - b)))


def _layernorm(y, g, b):
    mu = jnp.mean(y, axis=-1, keepdims=True)
    d = y - mu
    var = jnp.mean(d * d, axis=-1, keepdims=True)
    return d * lax.rsqrt(var + LN_EPS) * g + b


def _rms(y):
    return y * lax.rsqrt(jnp.mean(y * y, axis=-1, keepdims=True) + RMS_EPS)


def _split_bf16(x):
    hi = x.astype(BF16)
    lo = (x - hi.astype(F32)).astype(BF16)
    return hi, lo


def _col_chunks(width, step):
    return [(c, min(step, width - c)) for c in range(0, width, step)]


def _ffn_ln_body(x_ref, wg_ref, wu_ref, wd_ref, g_ref, b_ref, o_ref, *, ff_chunk):
    x = x_ref[...]
    xb = x.astype(BF16)
    acc = None
    for c0, cw in _col_chunks(wg_ref.shape[1], ff_chunk):
        gate = jnp.dot(xb, wg_ref[:, c0:c0 + cw], preferred_element_type=F32)
        up = jnp.dot(xb, wu_ref[:, c0:c0 + cw], preferred_element_type=F32)
        hid = (_silu(gate) * up).astype(BF16)
        part = jnp.dot(hid, wd_ref[c0:c0 + cw, :], preferred_element_type=F32)
        acc = part if acc is None else acc + part
    o_ref[...] = _layernorm(ALPHA * x + 0.5 * acc, g_ref[...], b_ref[...])


def _ffn_ln(x, wg, wu, wd, g, b, layer, tm):
    rows, d = x.shape
    f = wg.shape[2]
    wspec = lambda shape: _resident((None,) + shape, lambda i: (layer, 0, 0))
    vmem = 3 * d * f * 2 + 4 * tm * d * 4 + 4 * tm * MXU_WIDTH * 4
    return pl.pallas_call(
        functools.partial(_ffn_ln_body, ff_chunk=MXU_WIDTH),
        out_shape=jax.ShapeDtypeStruct((rows, d), F32),
        grid=(rows // tm,),
        in_specs=[pl.BlockSpec((tm, d), lambda i: (i, 0)),
                  wspec((d, f)), wspec((d, f)), wspec((f, d)),
                  wspec((1, d)), wspec((1, d))],
        out_specs=pl.BlockSpec((tm, d), lambda i: (i, 0)),
        compiler_params=_params(("parallel",), vmem),
        name="ffn_ln",
    )(x, wg, wu, wd, g, b)


def _inproj_body(x_ref, w_ref, b_ref, *o_refs):
    xb = x_ref[...].astype(BF16)
    base = 0
    for o_ref in o_refs:
        for c0, cw in _col_chunks(o_ref.shape[1], 2 * MXU_WIDTH):
            w = w_ref[:, base + c0:base + c0 + cw]
            o_ref[:, c0:c0 + cw] = (jnp.dot(xb, w, preferred_element_type=F32)
                                    + b_ref[:, base + c0:base + c0 + cw])
        base += o_ref.shape[1]


def _inproj(x, w, b, layer, tm, widths):
    rows, d = x.shape
    n = w.shape[2]
    assert n == sum(widths)
    vmem = d * n * 2 + 2 * tm * d * 4 + 2 * tm * n * 4
    return pl.pallas_call(
        _inproj_body,
        out_shape=[jax.ShapeDtypeStruct((rows, wd), F32) for wd in widths],
        grid=(rows // tm,),
        in_specs=[pl.BlockSpec((tm, d), lambda i: (i, 0)),
                  _resident((None, d, n), lambda i: (layer, 0, 0)),
                  _resident((None, 1, n), lambda i: (layer, 0, 0))],
        out_specs=[pl.BlockSpec((tm, wd), lambda i: (i, 0)) for wd in widths],
        compiler_params=_params(("parallel",), vmem),
        name="inproj",
    )(x, w, b)


def _merge_body(x_ref, ya_ref, ym_ref, yh_ref, wgate_ref, bgate_ref, wa_ref, ws_ref, wh_ref,
                wo_ref, g_ref, b_ref, o_ref):
    x = x_ref[...]
    xb = x.astype(BF16)
    d = x.shape[1]
    branches = ((ya_ref[...].astype(BF16), wa_ref), (ym_ref[...].astype(BF16), ws_ref),
                (yh_ref[...].astype(BF16), wh_ref))
    merged = []
    for c0, cw in _col_chunks(d, MXU_WIDTH):
        m = None
        for k, (yb, w_ref) in enumerate(branches):
            col = k * d + c0
            gate = jax.nn.sigmoid(jnp.dot(xb, wgate_ref[:, col:col + cw], preferred_element_type=F32)
                                  + bgate_ref[:, col:col + cw])
            term = gate * jnp.dot(yb, w_ref[:, c0:c0 + cw], preferred_element_type=F32)
            m = term if m is None else m + term
        merged.append(m.astype(BF16))
    y = jnp.dot(jnp.concatenate(merged, axis=1), wo_ref[...], preferred_element_type=F32)
    o_ref[...] = _layernorm(ALPHA * x + y, g_ref[...], b_ref[...])


def _merge(x, ya, ym, yh, wgate, bgate, wa, ws, wh, wo, g, b, layer, tm):
    rows, d = x.shape
    wb = ya.shape[1]
    wspec = lambda shape: _resident((None,) + shape, lambda i: (layer, 0, 0))
    row = lambda width: pl.BlockSpec((tm, width), lambda i: (i, 0))
    vmem = (3 * d * d + 3 * wb * d + d * d) * 2 + 4 * tm * d * 4 + 6 * tm * wb * 4
    return pl.pallas_call(
        _merge_body,
        out_shape=jax.ShapeDtypeStruct((rows, d), F32),
        grid=(rows // tm,),
        in_specs=[row(d), row(wb), row(wb), row(wb),
                  wspec((d, 3 * d)), wspec((1, 3 * d)),
                  wspec((wb, d)), wspec((wb, d)), wspec((wb, d)), wspec((d, d)),
                  wspec((1, d)), wspec((1, d))],
        out_specs=row(d),
        compiler_params=_params(("parallel",), vmem),
        name="merge_out_ln",
    )(x, ya, ym, yh, wgate, bgate, wa, ws, wh, wo, g, b)


def _alibi_slope_col(head_rows):
    slope = jnp.zeros(head_rows.shape, F32)
    for h in range(ATT_HEADS):
        slope = jnp.where(head_rows == h, 2.0 ** (-8.0 * (h + 1) / ATT_HEADS), slope)
    return slope


def _attn_prompt_body(sink_ref, q_ref, kvc_ref, kvp_ref, o_ref):
    n = pl.program_id(1)
    blk = q_ref.shape[0]
    grp = ATT_HEADS // ATT_KV_HEADS
    q = q_ref[...]
    kcat = jnp.concatenate([kvp_ref[:, :LANES], kvc_ref[:, :LANES]], axis=0).astype(BF16)
    vcat = jnp.concatenate([kvp_ref[:, LANES:], kvc_ref[:, LANES:]], axis=0).astype(BF16)
    lane = lax.broadcasted_iota(jnp.int32, (blk, LANES), 1)
    low = lane < HEAD_DIM

    rows = lax.broadcasted_iota(jnp.int32, (grp * blk, 2 * blk), 0)
    cols = lax.broadcasted_iota(jnp.int32, (grp * blk, 2 * blk), 1)
    dist = WINDOW + (rows & (blk - 1)) - cols
    valid = (dist >= 0) & (dist <= WINDOW) & ((cols >= blk) | (n > 0))
    distf = dist.astype(F32)
    grow = lax.broadcasted_iota(jnp.int32, (grp * blk, 1), 0) >> (blk.bit_length() - 1)

    for kv in range(ATT_KV_HEADS):
        keep = low if kv == 0 else jnp.logical_not(low)
        parts = []
        for g in range(grp):
            h = kv * grp + g
            qcol = q[:, (h // 2) * LANES:(h // 2 + 1) * LANES]
            if (h % 2) != kv:
                qcol = pltpu.roll(qcol, HEAD_DIM, 1)
            parts.append(jnp.where(keep, qcol, 0.0))
        qs = jnp.concatenate(parts, axis=0).astype(BF16)
        s = lax.dot_general(qs, kcat, NT_DIMS, preferred_element_type=F32) * (HEAD_DIM ** -0.5)
        head = kv * grp + grow
        s = jnp.where(valid, s - _alibi_slope_col(head) * distf, -jnp.inf)
        sink = jnp.zeros((grp * blk, 1), F32)
        for g in range(grp):
            sink = jnp.where(grow == g, sink_ref[kv * grp + g], sink)
        m = jnp.maximum(jnp.max(s, axis=1, keepdims=True), sink)
        p = jnp.exp(s - m)
        den = jnp.sum(p, axis=1, keepdims=True) + jnp.exp(sink - m)
        p = (p / den).astype(BF16)
        o = jnp.dot(p, vcat, preferred_element_type=F32)
        for j in range(grp // 2):
            even = o[(2 * j) * blk:(2 * j + 1) * blk]
            odd = o[(2 * j + 1) * blk:(2 * j + 2) * blk]
            if kv == 0:
                col = jnp.where(low, even, pltpu.roll(odd, HEAD_DIM, 1))
            else:
                col = jnp.where(low, pltpu.roll(even, HEAD_DIM, 1), odd)
            c0 = (kv * grp // 2 + j) * LANES
            o_ref[:, c0:c0 + LANES] = col.astype(o_ref.dtype)


def _attn_prompt(qkv, sinks, batch, seq):
    nb = seq // CHUNK
    qw = ATT_HEADS * HEAD_DIM
    kvw = 2 * ATT_KV_HEADS * HEAD_DIM
    kv_col = qw // kvw
    return pl.pallas_call(
        _attn_prompt_body,
        out_shape=jax.ShapeDtypeStruct((batch * seq, qw), BF16),
        grid=(batch, nb),
        in_specs=[pl.BlockSpec(memory_space=pltpu.SMEM),
                  pl.BlockSpec((CHUNK, qw), lambda b, n: (b * nb + n, 0)),
                  pl.BlockSpec((CHUNK, kvw), lambda b, n: (b * nb + n, kv_col)),
                  pl.BlockSpec((CHUNK, kvw), lambda b, n: (b * nb + jnp.maximum(n - 1, 0), kv_col))],
        out_specs=pl.BlockSpec((CHUNK, qw), lambda b, n: (b * nb + n, 0)),
        compiler_params=_params(("parallel", "arbitrary"), 8 * 1024 * 1024),
        name="attn_prompt",
    )(sinks, qkv, qkv, qkv)


def _expand_heads(col_of, lane_low):
    cols = []
    for j in range(M_HEADS // 2):
        cols.append(jnp.where(lane_low, col_of(2 * j), col_of(2 * j + 1)))
    return jnp.concatenate(cols, axis=1)


def _ssd_prompt_body(zxd_ref, cw_ref, cb_ref, dtb_ref, a_ref, dskip_ref, nw_ref, tril_ref,
                     y_ref, st_ref, xpad_ref):
    c = pl.program_id(1)
    L = zxd_ref.shape[0]
    inner = M_HEADS * M_HEADDIM
    conv_dim = inner + 2 * M_GROUPS * M_STATE

    @pl.when(c == 0)
    def _():
        st_ref[...] = jnp.zeros_like(st_ref)
        xpad_ref[0:SUBLANES, :] = jnp.zeros((SUBLANES, conv_dim), F32)

    z = zxd_ref[:, 0:inner]
    xbc = zxd_ref[:, inner:inner + conv_dim]
    dtraw = zxd_ref[:, inner + conv_dim:]
    xpad_ref[SUBLANES:SUBLANES + L, :] = xbc
    conv = xbc * cw_ref[CONV_W - 1:CONV_W, :] + cb_ref[...]
    for w in range(CONV_W - 1):
        off = SUBLANES - (CONV_W - 1) + w
        conv = conv + xpad_ref[off:off + L, :] * cw_ref[w:w + 1, :]
    xpad_ref[0:SUBLANES, :] = xbc[L - SUBLANES:L, :]
    act = _silu(conv)
    xs = act[:, :inner]
    bm = act[:, inner:inner + LANES]
    cm = act[:, inner + LANES:inner + 2 * LANES]

    dt = _softplus(dtraw + dtb_ref[...])
    dta_hi, dta_lo = _split_bf16(dt * a_ref[...])
    cum = jnp.dot(tril_ref[...], jnp.concatenate([dta_hi, dta_lo], axis=1), preferred_element_type=F32)
    a = cum[:, :LANES] + cum[:, LANES:]
    a_t = a.T
    dt_t = dt.T
    ea = jnp.exp(a)
    a_last = a[L - 1:L, :]
    coef = jnp.exp(a_last - a) * dt
    ea_last = jnp.exp(a_last)

    lane = lax.broadcasted_iota(jnp.int32, (L, LANES), 1)
    low = lane < M_STATE
    rows = lax.broadcasted_iota(jnp.int32, (L, L), 0)
    cols = lax.broadcasted_iota(jnp.int32, (L, L), 1)
    causal = rows >= cols

    bmb = bm.astype(BF16)
    cmb = cm.astype(BF16)
    cb_g = [lax.dot_general(jnp.where(low, cm, 0.0).astype(BF16), bmb, NT_DIMS, preferred_element_type=F32),
            lax.dot_general(jnp.where(low, 0.0, cm).astype(BF16), bmb, NT_DIMS, preferred_element_type=F32)]

    def weights(h):
        diff = jnp.broadcast_to(a[:, h:h + 1], (L, L)) - jnp.broadcast_to(a_t[h:h + 1, :], (L, L))
        decay = jnp.exp(jnp.where(causal, diff, -jnp.inf))
        hpg = M_HEADS // M_GROUPS
        return (cb_g[h // hpg] * decay * jnp.broadcast_to(dt_t[h:h + 1, :], (L, L))).astype(BF16)

    ycols = []
    for j in range(M_HEADS // 2):
        xcol = xs[:, j * LANES:(j + 1) * LANES]
        ycols.append(jnp.dot(weights(2 * j), jnp.where(low, xcol, 0.0).astype(BF16), preferred_element_type=F32)
                     + jnp.dot(weights(2 * j + 1), jnp.where(low, 0.0, xcol).astype(BF16),
                               preferred_element_type=F32))
    y = jnp.concatenate(ycols, axis=1)

    st = st_ref[...]
    ea_x = _expand_heads(lambda h: jnp.broadcast_to(ea[:, h:h + 1], (L, LANES)), low)
    y = y + jnp.dot(cmb, st.astype(BF16), preferred_element_type=F32) * ea_x

    coef_x = _expand_heads(lambda h: jnp.broadcast_to(coef[:, h:h + 1], (L, LANES)), low)
    cs = jnp.dot(bm.T.astype(BF16), (xs * coef_x).astype(BF16), preferred_element_type=F32)
    srow = lax.broadcasted_iota(jnp.int32, cs.shape, 0) // M_STATE
    scol = lax.broadcasted_iota(jnp.int32, cs.shape, 1) // (inner // M_GROUPS)
    low1 = lane[0:1, :] < M_STATE
    dec_x = _expand_heads(lambda h: jnp.broadcast_to(ea_last[:, h:h + 1], (1, LANES)), low1)
    st_ref[...] = st * dec_x + jnp.where(srow == scol, cs, 0.0)

    y = (y + dskip_ref[...] * xs) * _silu(z)
    y_ref[...] = (_rms(y) * nw_ref[...]).astype(y_ref.dtype)


def _ssd_prompt(zxd, conv_w, conv_b, dtb, a_neg, dskip, norm_w, tril, batch, seq):
    nc = seq // CHUNK
    width = zxd.shape[1]
    inner = M_HEADS * M_HEADDIM
    conv_dim = conv_w.shape[1]
    full = lambda a: pl.BlockSpec(a.shape, lambda b, c: (0,) * a.ndim)
    consts = (conv_w, conv_b, dtb, a_neg, dskip, norm_w, tril)
    return pl.pallas_call(
        _ssd_prompt_body,
        out_shape=[jax.ShapeDtypeStruct((batch * seq, inner), BF16),
                   jax.ShapeDtypeStruct((batch, M_GROUPS * M_STATE, inner), F32)],
        grid=(batch, nc),
        in_specs=[pl.BlockSpec((CHUNK, width), lambda b, c: (b * nc + c, 0))] + [full(a) for a in consts],
        out_specs=[pl.BlockSpec((CHUNK, inner), lambda b, c: (b * nc + c, 0)),
                   pl.BlockSpec((None, M_GROUPS * M_STATE, inner), lambda b, c: (b, 0, 0))],
        scratch_shapes=[pltpu.VMEM((SUBLANES + CHUNK, conv_dim), F32)],
        compiler_params=_params(("arbitrary", "arbitrary"), 8 * 1024 * 1024),
        name="ssd_prompt",
    )(zxd, *consts)


HG_LEVELS = tuple(2 ** i for i in range(7))


def _hgrn_sum_matrix(L):
    r = np.arange(L)
    tt, rr = np.meshgrid(r, r, indexing="ij")
    mats = []
    for b in HG_LEVELS:
        mid = (r // (2 * b)) * 2 * b + b - 1
        upper = (r // b) % 2 == 1
        up = (mid[:, None] < rr) & (rr <= tt)
        lo = (tt < rr) & (rr <= mid[:, None])
        mats.append(np.where(upper[:, None], up, lo))
    mats.append(rr <= tt)
    mats.append(rr > tt)
    return np.concatenate(mats, axis=0).astype(np.float32)


def _hgrn_gates(hq, hf, log_lb, log1m_lb, one_m_lb):
    q = _silu(hq)
    logf = _logaddexp(log_lb, log1m_lb + _log_sigmoid(hf))
    kk = one_m_lb * jax.nn.sigmoid(-hf)
    return q, logf, kk


def _hgrn_prompt_body(x_ref, sums_ref, lb_ref, nw_ref, y_ref, so_ref, st_ref):
    c = pl.program_id(1)
    L = x_ref.shape[0]
    width = HG_HEADS * LANES

    @pl.when(c == 0)
    def _():
        st_ref[...] = jnp.zeros_like(st_ref)

    rows = lax.broadcasted_iota(jnp.int32, (L, L), 0)
    cols = lax.broadcasted_iota(jnp.int32, (L, L), 1)
    nlev = len(HG_LEVELS)
    for h in range(HG_HEADS):
        sl = slice(h * LANES, (h + 1) * LANES)
        q, logf, kk = _hgrn_gates(x_ref[:, sl], x_ref[:, width + h * LANES:width + (h + 1) * LANES],
                                  lb_ref[0:1, sl], lb_ref[1:2, sl], lb_ref[2:3, sl])
        iv = x_ref[:, 2 * width + h * LANES:2 * width + (h + 1) * LANES]
        og = x_ref[:, 3 * width + h * LANES:3 * width + (h + 1) * LANES]
        g_hi, g_lo = _split_bf16(logf)
        e2 = jnp.dot(sums_ref[...], jnp.concatenate([g_hi, g_lo], axis=1), preferred_element_type=F32)
        e = e2[:, :LANES] + e2[:, LANES:]
        att = jnp.where(rows == cols, jnp.sum(q * kk, axis=1, keepdims=True), 0.0)
        for lvl, b in enumerate(HG_LEVELS):
            sh = b.bit_length() - 1
            upper = ((rows >> sh) & 1) == 1
            xk = jnp.exp(e[lvl * L:(lvl + 1) * L]) * jnp.where(upper, q, kk)
            qb = jnp.where(upper, xk, 0.0).astype(BF16)
            kb = jnp.where(upper, 0.0, xk).astype(BF16)
            a_lvl = lax.dot_general(qb, kb, NT_DIMS, preferred_element_type=F32)
            att = att + jnp.where((rows >> (sh + 1)) == (cols >> (sh + 1)), a_lvl, 0.0)
        e_q = e[nlev * L:(nlev + 1) * L]
        e_k = e[(nlev + 1) * L:(nlev + 2) * L]
        qd = (q * jnp.exp(e_q)).astype(BF16)
        kd = (kk * jnp.exp(e_k)).astype(BF16)
        ivb = iv.astype(BF16)
        st = st_ref[sl, :]
        o = (jnp.dot(att.astype(BF16), ivb, preferred_element_type=F32)
             + lax.dot_general(qd, st.astype(BF16), NT_DIMS, preferred_element_type=F32))
        st_new = st * jnp.exp(e_q[L - 1:L, :]) + jnp.dot(iv.T.astype(BF16), kd, preferred_element_type=F32)
        st_ref[sl, :] = st_new
        so_ref[sl, :] = st_new.T
        y_ref[:, sl] = (_rms(o) * nw_ref[:, sl] * _silu(og)).astype(y_ref.dtype)


def _hgrn_prompt(xh, sums, lb_rows, nw, batch, seq):
    nc = seq // CHUNK
    width = HG_HEADS * LANES
    full = lambda a: pl.BlockSpec(a.shape, lambda b, c: (0,) * a.ndim)
    return pl.pallas_call(
        _hgrn_prompt_body,
        out_shape=[jax.ShapeDtypeStruct((batch * seq, width), BF16),
                   jax.ShapeDtypeStruct((batch, width, LANES), F32)],
        grid=(batch, nc),
        in_specs=[pl.BlockSpec((CHUNK, 4 * width), lambda b, c: (b * nc + c, 0)),
                  full(sums), full(lb_rows), full(nw)],
        out_specs=[pl.BlockSpec((CHUNK, width), lambda b, c: (b * nc + c, 0)),
                   pl.BlockSpec((None, width, LANES), lambda b, c: (b, 0, 0))],
        scratch_shapes=[pltpu.VMEM((width, LANES), F32)],
        compiler_params=_params(("arbitrary", "arbitrary"), 16 * 1024 * 1024),
        name="hgrn_prompt",
    )(xh, sums, lb_rows, nw)


def _hi_lo_rows(x_row, row_idx):
    xb = jnp.broadcast_to(x_row, row_idx.shape)
    hi = xb.astype(BF16).astype(F32)
    return jnp.where(row_idx == 0, hi, jnp.where(row_idx == 1, xb - hi, 0.0)).astype(BF16)


def _sample_mixer_body(q8_ref, kn_ref, vn_ref, kc_ref, vc_ref, sink_ref,
                       zx_ref, cst_ref, h0_ref, cw_ref, cb_ref, dtb_ref, a_ref, dskip_ref, snw_ref,
                       xh_ref, s0_ref, lb_ref, hnw_ref,
                       oa_ref, ym_ref, hn_ref, yh_ref, sn_ref, *, n_prev_valid):
    nb = q8_ref.shape[0]
    inner = M_HEADS * M_HEADDIM
    conv_dim = inner + 2 * M_GROUPS * M_STATE
    width = HG_HEADS * LANES
    grp = ATT_HEADS // ATT_KV_HEADS
    row8 = lax.broadcasted_iota(jnp.int32, (SUBLANES, LANES), 0)
    lane8 = lax.broadcasted_iota(jnp.int32, (SUBLANES, LANES), 1)
    hrow = lax.broadcasted_iota(jnp.int32, (ATT_HEADS, 1), 0)
    slope = _alibi_slope_col(hrow)
    jj = lax.broadcasted_iota(jnp.int32, (ATT_HEADS, WINDOW), 1)
    cache_dist = (WINDOW - jj).astype(F32)
    cache_ok = jj >= WINDOW - n_prev_valid
    kv_half = (lane8 // HEAD_DIM) == (row8 // grp)
    lane_in = lax.broadcasted_iota(jnp.int32, (1, inner), 1)
    in_g0 = lane_in < inner // M_GROUPS
    rowx = lax.broadcasted_iota(jnp.int32, (SUBLANES, inner), 0)
    lanex = lax.broadcasted_iota(jnp.int32, (SUBLANES, inner), 1)
    ones_rows01 = jnp.where(lax.broadcasted_iota(jnp.int32, (SUBLANES, LANES), 0) < 2, 1.0, 0.0).astype(BF16)

    for i in range(nb):
        q8 = q8_ref[i]
        kn = kn_ref[i]
        vn = vn_ref[i]
        s_c = lax.dot_general(q8.astype(BF16), kc_ref[i].astype(BF16), NT_DIMS,
                              preferred_element_type=F32) * (HEAD_DIM ** -0.5)
        s_c = jnp.where(cache_ok, s_c - slope * cache_dist, -jnp.inf)
        s_n = jnp.sum(q8 * kn, axis=1, keepdims=True) * (HEAD_DIM ** -0.5)
        sink = sink_ref[:, 0:1]
        m = jnp.maximum(jnp.maximum(jnp.max(s_c, axis=1, keepdims=True), s_n), sink)
        p_c = jnp.exp(s_c - m)
        p_n = jnp.exp(s_n - m)
        den = jnp.sum(p_c, axis=1, keepdims=True) + p_n + jnp.exp(sink - m)
        o = (jnp.dot((p_c / den).astype(BF16), vc_ref[i].astype(BF16), preferred_element_type=F32)
             + (p_n / den) * vn)
        oa_ref[i] = jnp.where(kv_half, o, 0.0)

        zx = zx_ref[i]
        z = zx[:, :inner]
        xbc = zx[:, inner:inner + conv_dim]
        dtraw = zx[:, inner + conv_dim:]
        cst = cst_ref[i]
        conv = xbc * cw_ref[CONV_W - 1:CONV_W, :] + cb_ref[...]
        for w in range(CONV_W - 1):
            conv = conv + cst[w:w + 1, :] * cw_ref[w:w + 1, :]
        act = _silu(conv)
        xs = act[:, :inner]
        bm = act[:, inner:inner + LANES]
        cm = act[:, inner + LANES:]
        dt = _softplus(dtraw + dtb_ref[...])
        da = jnp.exp(dt * a_ref[...])
        h0 = h0_ref[i]
        cm_b = jnp.broadcast_to(cm, (SUBLANES, LANES))
        bm_b = jnp.broadcast_to(bm, (SUBLANES, LANES))
        c8 = jnp.where(row8 == 0, cm_b, jnp.where(row8 == 1, pltpu.roll(cm_b, M_STATE, 1), 0.0))[:, :M_STATE]
        b8 = jnp.where(row8 == 0, bm_b, jnp.where(row8 == 1, pltpu.roll(bm_b, M_STATE, 1), 0.0))[:, :M_STATE]
        ch = lax.dot_general(c8.astype(BF16), h0.astype(BF16), NT_DIMS, preferred_element_type=F32)
        c_dot_h = jnp.where(in_g0, ch[0:1, :], ch[1:2, :])
        cbp = cm * bm
        lane1 = lane8[0:1, :]
        cb0 = jnp.sum(jnp.where(lane1 < M_STATE, cbp, 0.0), axis=1, keepdims=True)
        cb1 = jnp.sum(jnp.where(lane1 < M_STATE, 0.0, cbp), axis=1, keepdims=True)
        y = da * c_dot_h + dt * jnp.where(in_g0, cb0, cb1) * xs + dskip_ref[...] * xs
        y = y * _silu(z)
        ym_ref[i] = _rms(y) * snw_ref[...]
        dtx = jnp.broadcast_to(dt * xs, (SUBLANES, inner))
        lhs_u = jnp.where((rowx == 0) & (lanex < inner // M_GROUPS), dtx,
                          jnp.where((rowx == 1) & (lanex >= inner // M_GROUPS), dtx, 0.0)).astype(BF16)
        upd = lax.dot_general(lhs_u, b8.astype(BF16), TN_DIMS, preferred_element_type=F32)
        da_col = lax.dot_general(_hi_lo_rows(da, rowx), ones_rows01[:, :M_STATE], TN_DIMS,
                                 preferred_element_type=F32)
        hn_ref[i] = h0 * da_col + upd

        xh = xh_ref[i]
        q, logf, kk = _hgrn_gates(xh[:, :width], xh[:, width:2 * width],
                                  lb_ref[0:1, :], lb_ref[1:2, :], lb_ref[2:3, :])
        iv = xh[:, 2 * width:3 * width]
        og = xh[:, 3 * width:]
        f = jnp.exp(logf)
        s0 = s0_ref[i]
        head_of_lane = lanex // LANES
        qf_b = jnp.broadcast_to(q * f, (SUBLANES, width))
        kk_b = jnp.broadcast_to(kk, (SUBLANES, width))
        lhs_q = jnp.where(rowx == head_of_lane, qf_b, 0.0).astype(BF16)
        qs = jnp.dot(lhs_q, s0.astype(BF16), preferred_element_type=F32)
        qkk = q * kk
        o_parts = []
        i_rows = []
        for h in range(HG_HEADS):
            sl = slice(h * LANES, (h + 1) * LANES)
            att = jnp.sum(qkk[:, sl], axis=1, keepdims=True)
            o_h = att * iv[:, sl] + qs[h:h + 1, :]
            o_parts.append(_rms(o_h))
            i_rows.append(iv[:, sl])
        yh_ref[i] = jnp.concatenate(o_parts, axis=1) * hnw_ref[...] * _silu(og)
        i8 = jnp.concatenate(i_rows + [jnp.zeros((SUBLANES - HG_HEADS, LANES), F32)], axis=0)
        lhs_k = jnp.where(rowx == head_of_lane, kk_b, 0.0).astype(BF16)
        upd_s = lax.dot_general(lhs_k, i8.astype(BF16), TN_DIMS, preferred_element_type=F32)
        f_col = lax.dot_general(_hi_lo_rows(f, rowx), ones_rows01, TN_DIMS, preferred_element_type=F32)
        sn_ref[i] = s0 * f_col + upd_s


def _sample_mixer(q8, kn, vn, kc, vc, sink8, zx, cst, h0, conv_w, conv_b, dtb_x, a_x, dskip, snw,
                  xh, s0, lb_rows, hnw, layer, nb):
    n = q8.shape[0]
    per = lambda a: pl.BlockSpec((nb,) + a.shape[1:], lambda i: (i,) + (0,) * (a.ndim - 1))
    lay = lambda a: pl.BlockSpec((None, nb) + a.shape[2:], lambda i: (layer, i) + (0,) * (a.ndim - 2))
    full = lambda a: pl.BlockSpec(a.shape, lambda i: (0,) * a.ndim)
    inner = M_HEADS * M_HEADDIM
    width = HG_HEADS * LANES
    out_shape = [jax.ShapeDtypeStruct((n, ATT_HEADS, LANES), F32),
                 jax.ShapeDtypeStruct((n, 1, inner), F32),
                 jax.ShapeDtypeStruct((n, inner, M_STATE), F32),
                 jax.ShapeDtypeStruct((n, 1, width), F32),
                 jax.ShapeDtypeStruct((n, width, LANES), F32)]
    return pl.pallas_call(
        functools.partial(_sample_mixer_body, n_prev_valid=min(WINDOW, PAST_LEN)),
        out_shape=out_shape,
        grid=(n // nb,),
        in_specs=[per(q8), per(kn), per(vn), lay(kc), lay(vc), full(sink8),
                  per(zx), lay(cst), lay(h0), full(conv_w), full(conv_b), full(dtb_x), full(a_x),
                  full(dskip), full(snw),
                  per(xh), lay(s0), full(lb_rows), full(hnw)],
        out_specs=[pl.BlockSpec((nb,) + s.shape[1:], lambda i: (i, 0, 0)) for s in out_shape],
        compiler_params=_params(("parallel",), 16 * 1024 * 1024),
        name="sample_mixer",
    )(q8, kn, vn, kc, vc, sink8, zx, cst, h0, conv_w, conv_b, dtb_x, a_x, dskip, snw, xh, s0, lb_rows, hnw)


def _row_tile(rows):
    for tm in (512, 256, 128):
        if rows % tm == 0:
            return tm
    return rows


def kernel(x_prompt, x_sample, cache_swa_k, cache_swa_v, state_conv, state_ssm, state_hgrn, ln1_g, ln1_b, ffn1_wg, ffn1_wu, ffn1_wd, w_in, b_in, att_sinks, conv_w, conv_b, dt_bias, a_log, d_skip, ssm_norm_w, hg_lb_logits, hg_norm_w, w_br_att, w_br_ssm, w_br_hg, w_out, ln2_g, ln2_b, ffn2_wg, ffn2_wu, ffn2_wd, ln3_g, ln3_b):
    bp, seq, d = x_prompt.shape
    ns = x_sample.shape[0]
    depth = w_in.shape[0]
    assert seq % CHUNK == 0 and x_sample.shape[1] == 1
    qw = ATT_HEADS * HEAD_DIM
    kvw = ATT_KV_HEADS * HEAD_DIM
    inner = M_HEADS * M_HEADDIM
    conv_dim = inner + 2 * M_GROUPS * M_STATE
    width = HG_HEADS * LANES
    hpg = M_HEADS // M_GROUPS

    bf = lambda a: a.astype(BF16)
    row3 = lambda a: a.reshape(depth, 1, a.shape[-1])
    o_z = qw + 2 * kvw
    o_dt = o_z + inner + conv_dim
    o_h = o_dt + M_HEADS
    o_g = o_h + 4 * width
    dt_pad = jnp.pad(w_in[:, :, o_dt:o_h], ((0, 0), (0, 0), (0, LANES - M_HEADS)))
    dt_rep = jnp.repeat(w_in[:, :, o_dt:o_h], M_HEADDIM, axis=2)
    w_prompt = bf(jnp.concatenate([w_in[:, :, :o_dt], dt_pad, w_in[:, :, o_h:o_g]], axis=2))
    w_sample = bf(jnp.concatenate([w_in[:, :, :o_dt], dt_rep, w_in[:, :, o_h:o_g]], axis=2))
    b_prompt = row3(jnp.concatenate([b_in[:, :o_dt], jnp.pad(b_in[:, o_dt:o_h], ((0, 0), (0, LANES - M_HEADS))),
                                     b_in[:, o_h:o_g]], axis=1))
    b_sample = row3(jnp.concatenate([b_in[:, :o_dt], jnp.repeat(b_in[:, o_dt:o_h], M_HEADDIM, axis=1),
                                     b_in[:, o_h:o_g]], axis=1))
    w_gate, b_gate = bf(w_in[:, :, o_g:]), row3(b_in[:, o_g:])
    widths_p = (qw + 2 * kvw, inner + conv_dim + LANES, 4 * width)
    widths_s = (qw + 2 * kvw, inner + conv_dim + inner, 4 * width)
    ffn1 = (bf(ffn1_wg), bf(ffn1_wu), bf(ffn1_wd), row3(ln1_g), row3(ln1_b))
    ffn2 = (bf(ffn2_wg), bf(ffn2_wu), bf(ffn2_wd), row3(ln3_g), row3(ln3_b))
    merge_w = (w_gate, b_gate, bf(w_br_att), bf(w_br_ssm), bf(w_br_hg), bf(w_out), row3(ln2_g), row3(ln2_b))

    a_neg = -jnp.exp(a_log.astype(F32))
    pad_h = lambda a: jnp.pad(a, ((0, 0), (0, LANES - M_HEADS)))
    rep_h = lambda a: jnp.repeat(a, M_HEADDIM, axis=1)
    lb_all = jnp.cumsum(jax.nn.softmax(hg_lb_logits.astype(F32), axis=0), axis=0)
    lb_all = lb_all - lb_all[0]
    lb_rows = jnp.stack([jnp.log(lb_all), jnp.log1p(-lb_all), 1.0 - lb_all], axis=1)
    lb_rows = jnp.pad(lb_rows, ((0, 0), (0, SUBLANES - 3), (0, 0)))
    hnw = jnp.tile(hg_norm_w, (1, HG_HEADS))
    sums = jnp.asarray(_hgrn_sum_matrix(CHUNK), BF16)
    tril = jnp.asarray(np.tril(np.ones((CHUNK, CHUNK), np.float32)), BF16)
    sink8 = jnp.broadcast_to(att_sinks[:, :, None], (depth, ATT_HEADS, LANES))

    tm_p = _row_tile(bp * seq)
    tm_s = _row_tile(ns)
    nb_s = SUBLANES if ns % SUBLANES == 0 else 1

    xp = x_prompt.reshape(bp * seq, d)
    xs = x_sample.reshape(ns, d)
    p_states, s_states = [], []
    for l in range(depth):
        xp = _ffn_ln(xp, *ffn1, l, tm_p)
        qkv, zxd, xh = _inproj(xp, w_prompt, b_prompt, l, tm_p, widths_p)
        ya = _attn_prompt(qkv, att_sinks[l], bp, seq)
        ym, st_ssm = _ssd_prompt(zxd, conv_w[l], conv_b[l][None], pad_h(dt_bias)[l][None], pad_h(a_neg)[l][None],
                                 rep_h(d_skip)[l][None], ssm_norm_w[l][None], tril, bp, seq)
        yh, st_hg = _hgrn_prompt(xh, sums, lb_rows[l], hnw[l][None], bp, seq)
        xp = _merge(xp, ya, ym, yh, *merge_w, l, tm_p)
        xp = _ffn_ln(xp, *ffn2, l, tm_p)
        qkv3 = qkv.reshape(bp, seq, qw + 2 * kvw)
        p_k = qkv3[:, seq - WINDOW:, qw:qw + kvw].reshape(bp, WINDOW, ATT_KV_HEADS, HEAD_DIM)
        p_v = qkv3[:, seq - WINDOW:, qw + kvw:].reshape(bp, WINDOW, ATT_KV_HEADS, HEAD_DIM)
        p_conv = zxd.reshape(bp, seq, -1)[:, seq - (CONV_W - 1):, inner:inner + conv_dim]
        st6 = st_ssm.reshape(bp, M_GROUPS, M_STATE, M_GROUPS, hpg, M_HEADDIM)
        p_ssm = jnp.stack([st6[:, g, :, g] for g in range(M_GROUPS)], axis=1)
        p_ssm = p_ssm.transpose(0, 1, 3, 4, 2).reshape(bp, M_HEADS, M_HEADDIM, M_STATE)
        p_hg = st_hg.reshape(bp, HG_HEADS, LANES, LANES)
        p_states.append((p_k, p_v, p_conv, p_ssm, p_hg))

        xs = _ffn_ln(xs, *ffn1, l, tm_s)
        qkv_s, zx_s, xh_s = _inproj(xs, w_sample, b_sample, l, tm_s, widths_s)
        q4 = qkv_s[:, :qw].reshape(ns, ATT_KV_HEADS, ATT_HEADS // ATT_KV_HEADS, HEAD_DIM)
        zq = jnp.zeros_like(q4[:, 0])
        q8 = jnp.concatenate([jnp.concatenate([q4[:, 0], zq], axis=-1),
                              jnp.concatenate([zq, q4[:, 1]], axis=-1)], axis=1)
        k_new = qkv_s[:, qw:qw + kvw]
        v_new = qkv_s[:, qw + kvw:]
        oa, ym_s, h_new, yh_s, s_new = _sample_mixer(
            q8, k_new[:, None, :], v_new[:, None, :],
            cache_swa_k.reshape(depth, ns, WINDOW, kvw), cache_swa_v.reshape(depth, ns, WINDOW, kvw), sink8[l],
            zx_s[:, None, :], state_conv, state_ssm.reshape(depth, ns, inner, M_STATE),
            conv_w[l], conv_b[l][None], rep_h(dt_bias)[l][None], rep_h(a_neg)[l][None], rep_h(d_skip)[l][None],
            ssm_norm_w[l][None],
            xh_s[:, None, :], state_hgrn.reshape(depth, ns, width, LANES), lb_rows[l], hnw[l][None], l, nb_s)
        grp = ATT_HEADS // ATT_KV_HEADS
        ya_s = jnp.concatenate([oa[:, :grp, :HEAD_DIM].reshape(ns, grp * HEAD_DIM),
                                oa[:, grp:, HEAD_DIM:].reshape(ns, grp * HEAD_DIM)], axis=1)
        xs = _merge(xs, ya_s, ym_s.reshape(ns, inner), yh_s.reshape(ns, width), *merge_w, l, tm_s)
        xs = _ffn_ln(xs, *ffn2, l, tm_s)
        s_k = jnp.concatenate([cache_swa_k[l][:, 1:], k_new.reshape(ns, 1, ATT_KV_HEADS, HEAD_DIM)], axis=1)
        s_v = jnp.concatenate([cache_swa_v[l][:, 1:], v_new.reshape(ns, 1, ATT_KV_HEADS, HEAD_DIM)], axis=1)
        s_conv = jnp.concatenate([state_conv[l][:, 1:], zx_s[:, None, inner:inner + conv_dim]], axis=1)
        s_states.append((s_k, s_v, s_conv, h_new.reshape(ns, M_HEADS, M_HEADDIM, M_STATE),
                         s_new.reshape(ns, HG_HEADS, LANES, LANES)))

    outs_p = [jnp.stack(t) for t in zip(*p_states)]
    outs_s = [jnp.stack(t) for t in zip(*s_states)]
    return (xp.reshape(bp, seq, d), xs.reshape(ns, 1, d), *outs_p, *outs_s)
```

```python
import functools

import numpy as np
import jax
import jax.numpy as jnp
from jax import lax
from jax.experimental import pallas as pl
from jax.experimental.pallas import tpu as pltpu

F32 = jnp.float32
BF16 = jnp.bfloat16

ATT_HEADS = 8
ATT_KV_HEADS = 2
HEAD_DIM = 64
WINDOW = 128
PAST_LEN = 8192
M_HEADS = 8
M_HEADDIM = 64
M_GROUPS = 2
M_STATE = 64
CONV_W = 4
HG_HEADS = 4
DEPTH = 4
ALPHA = (2.0 * DEPTH) ** 0.25
LN_EPS = 1e-5
RMS_EPS = 1e-6
LOG2_E = 1.4426950408889634
CHUNK = 128

V7X_VMEM_BYTES = 64 * 1024 * 1024
LANES = 128
SUBLANES = 8
MXU_WIDTH = 256

NT_DIMS = (((1,), (1,)), ((), ()))
TN_DIMS = (((0,), (0,)), ((), ()))


def _vmem_limit(nbytes):
    return int(min(V7X_VMEM_BYTES - 8 * 1024 * 1024, nbytes + 16 * 1024 * 1024))


def _params(semantics, vmem_bytes):
    return pltpu.CompilerParams(dimension_semantics=semantics, vmem_limit_bytes=_vmem_limit(vmem_bytes))


def _resident(block_shape, index_map):
    return pl.BlockSpec(block_shape, index_map, pipeline_mode=pl.Buffered(1))


def _silu(x):
    return x * jax.nn.sigmoid(x)


def _softplus(x):
    return jnp.maximum(x, 0.0) + jnp.log1p(jnp.exp(-jnp.abs(x)))


def _log_sigmoid(x):
    return jnp.minimum(x, 0.0) - jnp.log1p(jnp.exp(-jnp.abs(x)))


def _logaddexp(a, b):
    return jnp.maximum(a, b) + jnp.log1p(jnp.exp(-jnp.abs(a - b)))


def _layernorm(y, g, b):
    mu = jnp.mean(y, axis=-1, keepdims=True)
    d = y - mu
    var = jnp.mean(d * d, axis=-1, keepdims=True)
    return d * lax.rsqrt(var + LN_EPS) * g + b


def _rms(y):
    return y * lax.rsqrt(jnp.mean(y * y, axis=-1, keepdims=True) + RMS_EPS)


def _split_bf16(x):
    hi = x.astype(BF16)
    lo = (x - hi.astype(F32)).astype(BF16)
    return hi, lo


def _col_chunks(width, step):
    return [(c, min(step, width - c)) for c in range(0, width, step)]


def _ffn_ln_body(x_ref, wg_ref, wu_ref, wd_ref, g_ref, b_ref, o_ref, *, ff_chunk):
    x = x_ref[...]
    xb = x.astype(BF16)
    acc = None
    for c0, cw in _col_chunks(wg_ref.shape[1], ff_chunk):
        gate = jnp.dot(xb, wg_ref[:, c0:c0 + cw], preferred_element_type=F32)
        up = jnp.dot(xb, wu_ref[:, c0:c0 + cw], preferred_element_type=F32)
        hid = (_silu(gate) * up).astype(BF16)
        part = jnp.dot(hid, wd_ref[c0:c0 + cw, :], preferred_element_type=F32)
        acc = part if acc is None else acc + part
    o_ref[...] = _layernorm(ALPHA * x + 0.5 * acc, g_ref[...], b_ref[...])


def _ffn_ln(x, wg, wu, wd, g, b, layer, tm):
    rows, d = x.shape
    f = wg.shape[2]
    wspec = lambda shape: _resident((None,) + shape, lambda i: (layer, 0, 0))
    vmem = 3 * d * f * 2 + 4 * tm * d * 4 + 4 * tm * MXU_WIDTH * 4
    return pl.pallas_call(
        functools.partial(_ffn_ln_body, ff_chunk=MXU_WIDTH),
        out_shape=jax.ShapeDtypeStruct((rows, d), F32),
        grid=(rows // tm,),
        in_specs=[pl.BlockSpec((tm, d), lambda i: (i, 0)),
                  wspec((d, f)), wspec((d, f)), wspec((f, d)),
                  wspec((1, d)), wspec((1, d))],
        out_specs=pl.BlockSpec((tm, d), lambda i: (i, 0)),
        compiler_params=_params(("parallel",), vmem),
        name="ffn_ln",
    )(x, wg, wu, wd, g, b)


def _inproj_body(x_ref, w_ref, b_ref, *o_refs):
    xb = x_ref[...].astype(BF16)
    base = 0
    for o_ref in o_refs:
        for c0, cw in _col_chunks(o_ref.shape[1], 2 * MXU_WIDTH):
            w = w_ref[:, base + c0:base + c0 + cw]
            o_ref[:, c0:c0 + cw] = (jnp.dot(xb, w, preferred_element_type=F32)
                                    + b_ref[:, base + c0:base + c0 + cw])
        base += o_ref.shape[1]


def _inproj(x, w, b, layer, tm, widths):
    rows, d = x.shape
    n = w.shape[2]
    assert n == sum(widths)
    vmem = d * n * 2 + 2 * tm * d * 4 + 2 * tm * n * 4
    return pl.pallas_call(
        _inproj_body,
        out_shape=[jax.ShapeDtypeStruct((rows, wd), F32) for wd in widths],
        grid=(rows // tm,),
        in_specs=[pl.BlockSpec((tm, d), lambda i: (i, 0)),
                  _resident((None, d, n), lambda i: (layer, 0, 0)),
                  _resident((None, 1, n), lambda i: (layer, 0, 0))],
        out_specs=[pl.BlockSpec((tm, wd), lambda i: (i, 0)) for wd in widths],
        compiler_params=_params(("parallel",), vmem),
        name="inproj",
    )(x, w, b)


def _merge_body(x_ref, ya_ref, ym_ref, yh_ref, wgate_ref, bgate_ref, wa_ref, ws_ref, wh_ref,
                wo_ref, g_ref, b_ref, o_ref):
    x = x_ref[...]
    xb = x.astype(BF16)
    d = x.shape[1]
    branches = ((ya_ref[...].astype(BF16), wa_ref), (ym_ref[...].astype(BF16), ws_ref),
                (yh_ref[...].astype(BF16), wh_ref))
    merged = []
    for c0, cw in _col_chunks(d, MXU_WIDTH):
        m = None
        for k, (yb, w_ref) in enumerate(branches):
            col = k * d + c0
            gate = jax.nn.sigmoid(jnp.dot(xb, wgate_ref[:, col:col + cw], preferred_element_type=F32)
                                  + bgate_ref[:, col:col + cw])
            term = gate * jnp.dot(yb, w_ref[:, c0:c0 + cw], preferred_element_type=F32)
            m = term if m is None else m + term
        merged.append(m.astype(BF16))
    y = jnp.dot(jnp.concatenate(merged, axis=1), wo_ref[...], preferred_element_type=F32)
    o_ref[...] = _layernorm(ALPHA * x + y, g_ref[...], b_ref[...])


def _merge(x, ya, ym, yh, wgate, bgate, wa, ws, wh, wo, g, b, layer, tm):
    rows, d = x.shape
    wb = ya.shape[1]
    wspec = lambda shape: _resident((None,) + shape, lambda i: (layer, 0, 0))
    row = lambda width: pl.BlockSpec((tm, width), lambda i: (i, 0))
    vmem = (3 * d * d + 3 * wb * d + d * d) * 2 + 4 * tm * d * 4 + 6 * tm * wb * 4
    return pl.pallas_call(
        _merge_body,
        out_shape=jax.ShapeDtypeStruct((rows, d), F32),
        grid=(rows // tm,),
        in_specs=[row(d), row(wb), row(wb), row(wb),
                  wspec((d, 3 * d)), wspec((1, 3 * d)),
                  wspec((wb, d)), wspec((wb, d)), wspec((wb, d)), wspec((d, d)),
                  wspec((1, d)), wspec((1, d))],
        out_specs=row(d),
        compiler_params=_params(("parallel",), vmem),
        name="merge_out_ln",
    )(x, ya, ym, yh, wgate, bgate, wa, ws, wh, wo, g, b)


def _alibi_slope_col(head_rows):
    slope = jnp.zeros(head_rows.shape, F32)
    for h in range(ATT_HEADS):
        slope = jnp.where(head_rows == h, 2.0 ** (-8.0 * (h + 1) / ATT_HEADS), slope)
    return slope


def _attn_bias(blk):
    tq = np.arange(blk)[:, None]
    j = np.arange(2 * blk)[None, :]
    dist = WINDOW + tq - j
    ok = (dist >= 0) & (dist <= WINDOW)
    slopes = 2.0 ** (-8.0 * np.arange(1, ATT_HEADS + 1) / ATT_HEADS)
    out = np.empty((2, ATT_HEADS, blk, 2 * blk), np.float32)
    for has_prev in range(2):
        vis = ok & ((j >= blk) | (has_prev == 1))
        out[has_prev] = np.where(vis[None], -slopes[:, None, None] * dist[None], -np.inf)
    return out


def _attn_prompt_body(sink_ref, bias_ref, q_ref, kvc_ref, kvp_ref, o_ref):
    blk = kvp_ref.shape[0]
    nsub = q_ref.shape[0] // blk
    grp = ATT_HEADS // ATT_KV_HEADS
    has_prev = jnp.minimum(pl.program_id(1), 1)
    lane = lax.broadcasted_iota(jnp.int32, (blk, LANES), 1)
    low = lane < HEAD_DIM

    def kv_block(i):
        return kvp_ref[...] if i < 0 else kvc_ref[i * blk:(i + 1) * blk, :]

    for i in range(nsub):
        prev, cur = kv_block(i - 1), kv_block(i)
        kcat = jnp.concatenate([prev[:, :LANES], cur[:, :LANES]], axis=0).astype(BF16)
        vcat = jnp.concatenate([prev[:, LANES:], cur[:, LANES:]], axis=0).astype(BF16)

        def head_out(h):
            kv = h // grp
            qcol = q_ref[i * blk:(i + 1) * blk, (h // 2) * LANES:(h // 2 + 1) * LANES] * (HEAD_DIM ** -0.5)
            if (h % 2) != kv:
                qcol = pltpu.roll(qcol, HEAD_DIM, 1)
            qh = jnp.where(low if kv == 0 else jnp.logical_not(low), qcol, 0.0).astype(BF16)
            bias = bias_ref[has_prev, h] if i == 0 else bias_ref[1, h]
            s = lax.dot_general(qh, kcat, NT_DIMS, preferred_element_type=F32) + bias
            sink = sink_ref[h]
            m = jnp.maximum(jnp.max(s, axis=1, keepdims=True), sink)
            p = jnp.exp(s - m)
            den = jnp.sum(p, axis=1, keepdims=True) + jnp.exp(sink - m)
            p = (p * (1.0 / den)).astype(BF16)
            return jnp.dot(p, vcat, preferred_element_type=F32)

        for c in range(ATT_HEADS // 2):
            even, odd = head_out(2 * c), head_out(2 * c + 1)
            if (2 * c) // grp == 0:
                col = jnp.where(low, even, pltpu.roll(odd, HEAD_DIM, 1))
            else:
                col = jnp.where(low, pltpu.roll(even, HEAD_DIM, 1), odd)
            o_ref[i * blk:(i + 1) * blk, c * LANES:(c + 1) * LANES] = col.astype(o_ref.dtype)


def _attn_prompt(qkv, sinks, bias, batch, seq, nsub):
    rows = nsub * CHUNK
    ns = seq // rows
    qw = ATT_HEADS * HEAD_DIM
    kvw = 2 * ATT_KV_HEADS * HEAD_DIM
    kv_col = qw // kvw
    return pl.pallas_call(
        _attn_prompt_body,
        out_shape=jax.ShapeDtypeStruct((batch * seq, qw), BF16),
        grid=(batch, ns),
        in_specs=[pl.BlockSpec(memory_space=pltpu.SMEM),
                  pl.BlockSpec(bias.shape, lambda b, n: (0, 0, 0, 0)),
                  pl.BlockSpec((rows, qw), lambda b, n: (b * ns + n, 0)),
                  pl.BlockSpec((rows, kvw), lambda b, n: (b * ns + n, kv_col)),
                  pl.BlockSpec((CHUNK, kvw), lambda b, n: ((b * ns + n) * nsub - jnp.minimum(n, 1), kv_col))],
        out_specs=pl.BlockSpec((rows, qw), lambda b, n: (b * ns + n, 0)),
        compiler_params=_params(("parallel", "arbitrary"), 16 * 1024 * 1024),
        name="attn_prompt",
    )(sinks, bias, qkv, qkv, qkv)


def _expand_heads(col_of, lane_low):
    cols = []
    for j in range(M_HEADS // 2):
        cols.append(jnp.where(lane_low, col_of(2 * j), col_of(2 * j + 1)))
    return jnp.concatenate(cols, axis=1)


def _ssd_prompt_body(zxd_ref, cw_ref, cb_ref, dtb_ref, a_ref, dskip_ref, nw_ref, tril_ref,
                     y_ref, st_ref, xpad_ref):
    c = pl.program_id(1)
    L = zxd_ref.shape[0]
    inner = M_HEADS * M_HEADDIM
    conv_dim = inner + 2 * M_GROUPS * M_STATE

    @pl.when(c == 0)
    def _():
        st_ref[...] = jnp.zeros_like(st_ref)
        xpad_ref[0:SUBLANES, :] = jnp.zeros((SUBLANES, conv_dim), F32)

    z = zxd_ref[:, 0:inner]
    xbc = zxd_ref[:, inner:inner + conv_dim]
    dtraw = zxd_ref[:, inner + conv_dim:]
    xpad_ref[SUBLANES:SUBLANES + L, :] = xbc
    conv = xbc * cw_ref[CONV_W - 1:CONV_W, :] + cb_ref[...]
    for w in range(CONV_W - 1):
        off = SUBLANES - (CONV_W - 1) + w
        conv = conv + xpad_ref[off:off + L, :] * cw_ref[w:w + 1, :]
    xpad_ref[0:SUBLANES, :] = xbc[L - SUBLANES:L, :]
    act = _silu(conv)
    xs = act[:, :inner]
    bm = act[:, inner:inner + LANES]
    cm = act[:, inner + LANES:inner + 2 * LANES]

    dt = _softplus(dtraw + dtb_ref[...])
    dta_hi, dta_lo = _split_bf16(dt * a_ref[...])
    cum = jnp.dot(tril_ref[...], jnp.concatenate([dta_hi, dta_lo], axis=1), preferred_element_type=F32)
    a = cum[:, :LANES] + cum[:, LANES:]
    a_t = a.T
    dt_t = dt.T
    ea = jnp.exp(a)
    a_last = a[L - 1:L, :]
    coef = jnp.exp(a_last - a) * dt
    ea_last = jnp.exp(a_last)

    lane = lax.broadcasted_iota(jnp.int32, (L, LANES), 1)
    low = lane < M_STATE
    rows = lax.broadcasted_iota(jnp.int32, (L, L), 0)
    cols = lax.broadcasted_iota(jnp.int32, (L, L), 1)
    causal = rows >= cols

    bmb = bm.astype(BF16)
    cmb = cm.astype(BF16)
    cb_g = [lax.dot_general(jnp.where(low, cm, 0.0).astype(BF16), bmb, NT_DIMS, preferred_element_type=F32),
            lax.dot_general(jnp.where(low, 0.0, cm).astype(BF16), bmb, NT_DIMS, preferred_element_type=F32)]

    def weights(h):
        diff = jnp.broadcast_to(a[:, h:h + 1], (L, L)) - jnp.broadcast_to(a_t[h:h + 1, :], (L, L))
        decay = jnp.exp(jnp.where(causal, diff, -jnp.inf))
        hpg = M_HEADS // M_GROUPS
        return (cb_g[h // hpg] * decay * jnp.broadcast_to(dt_t[h:h + 1, :], (L, L))).astype(BF16)

    ycols = []
    for j in range(M_HEADS // 2):
        xcol = xs[:, j * LANES:(j + 1) * LANES]
        ycols.append(jnp.dot(weights(2 * j), jnp.where(low, xcol, 0.0).astype(BF16), preferred_element_type=F32)
                     + jnp.dot(weights(2 * j + 1), jnp.where(low, 0.0, xcol).astype(BF16),
                               preferred_element_type=F32))
    y = jnp.concatenate(ycols, axis=1)

    st = st_ref[...]
    ea_x = _expand_heads(lambda h: jnp.broadcast_to(ea[:, h:h + 1], (L, LANES)), low)
    y = y + jnp.dot(cmb, st.astype(BF16), preferred_element_type=F32) * ea_x

    coef_x = _expand_heads(lambda h: jnp.broadcast_to(coef[:, h:h + 1], (L, LANES)), low)
    cs = jnp.dot(bm.T.astype(BF16), (xs * coef_x).astype(BF16), preferred_element_type=F32)
    srow = lax.broadcasted_iota(jnp.int32, cs.shape, 0) // M_STATE
    scol = lax.broadcasted_iota(jnp.int32, cs.shape, 1) // (inner // M_GROUPS)
    low1 = lane[0:1, :] < M_STATE
    dec_x = _expand_heads(lambda h: jnp.broadcast_to(ea_last[:, h:h + 1], (1, LANES)), low1)
    st_ref[...] = st * dec_x + jnp.where(srow == scol, cs, 0.0)

    y = (y + dskip_ref[...] * xs) * _silu(z)
    y_ref[...] = (_rms(y) * nw_ref[...]).astype(y_ref.dtype)


def _ssd_prompt(zxd, conv_w, conv_b, dtb, a_neg, dskip, norm_w, tril, batch, seq):
    nc = seq // CHUNK
    width = zxd.shape[1]
    inner = M_HEADS * M_HEADDIM
    conv_dim = conv_w.shape[1]
    full = lambda a: pl.BlockSpec(a.shape, lambda b, c: (0,) * a.ndim)
    consts = (conv_w, conv_b, dtb, a_neg, dskip, norm_w, tril)
    return pl.pallas_call(
        _ssd_prompt_body,
        out_shape=[jax.ShapeDtypeStruct((batch * seq, inner), BF16),
                   jax.ShapeDtypeStruct((batch, M_GROUPS * M_STATE, inner), F32)],
        grid=(batch, nc),
        in_specs=[pl.BlockSpec((CHUNK, width), lambda b, c: (b * nc + c, 0))] + [full(a) for a in consts],
        out_specs=[pl.BlockSpec((CHUNK, inner), lambda b, c: (b * nc + c, 0)),
                   pl.BlockSpec((None, M_GROUPS * M_STATE, inner), lambda b, c: (b, 0, 0))],
        scratch_shapes=[pltpu.VMEM((SUBLANES + CHUNK, conv_dim), F32)],
        compiler_params=_params(("arbitrary", "arbitrary"), 8 * 1024 * 1024),
        name="ssd_prompt",
    )(zxd, *consts)


HG_LEVELS = tuple(2 ** i for i in range(7))
HG_FINE = tuple(b for b in HG_LEVELS if b < SUBLANES)


def _hgrn_sum_matrix(L):
    r = np.arange(L)
    tt, rr = np.meshgrid(r, r, indexing="ij")
    mats = []
    for b in HG_FINE:
        mid = (r // (2 * b)) * 2 * b + b - 1
        upper = (r // b) % 2 == 1
        up = (mid[:, None] < rr) & (rr <= tt)
        lo = (tt < rr) & (rr <= mid[:, None])
        mats.append(np.where(upper[:, None], up, lo))
    mats.append(rr <= tt)
    m = np.concatenate(mats, axis=0).astype(np.float32)
    return np.concatenate([m, m], axis=1)


def _hgrn_level_tables(L):
    r = np.arange(L)
    later = np.concatenate([np.repeat((((r // b) % 2) == 1)[:, None], LANES, axis=1) for b in HG_LEVELS], axis=0)
    level = np.full((L, L), -1, np.int32)
    for lvl, b in enumerate(HG_LEVELS):
        t_later = ((r // b) % 2 == 1)[:, None]
        s_earlier = ((r // b) % 2 == 0)[None, :]
        same_parent = (r // (2 * b))[:, None] == (r // (2 * b))[None, :]
        level[t_later & s_earlier & same_parent] = lvl
    level[r, r] = len(HG_LEVELS)
    return np.where(later, 1.0, -1.0).astype(np.float32), level


def _hgrn_gates(hq, hf, log_lb, log1m_lb, one_m_lb):
    q = _silu(hq)
    e_neg = jnp.exp(-jnp.abs(hf))
    one_p = 1.0 + e_neg
    log_sig = jnp.minimum(hf, 0.0) - jnp.log(one_p)
    inv = 1.0 / one_p
    kk = one_m_lb * jnp.where(hf >= 0.0, e_neg * inv, inv)
    b = log1m_lb + log_sig
    logf = jnp.maximum(log_lb, b) + jnp.log(1.0 + jnp.exp(-jnp.abs(log_lb - b)))
    return q, logf, kk


def _hgrn_prompt_body(x_ref, sums_ref, sign_ref, level_ref, lb_ref, nw_ref, y_ref, st_ref):
    c = pl.program_id(1)
    L = CHUNK
    width = HG_HEADS * LANES
    nlev, nfine = len(HG_LEVELS), len(HG_FINE)

    @pl.when(c == 0)
    def _():
        st_ref[...] = jnp.zeros_like(st_ref)

    level = level_ref[...]
    for r0 in range(0, x_ref.shape[0], L):
        rows = slice(r0, r0 + L)
        gates = []
        for h in range(HG_HEADS):
            sl = slice(h * LANES, (h + 1) * LANES)
            q, logf, kk = _hgrn_gates(x_ref[rows, sl], x_ref[rows, width + h * LANES:width + (h + 1) * LANES],
                                      lb_ref[0:1, sl], lb_ref[1:2, sl], lb_ref[2:3, sl])
            gates.append((q, logf * LOG2_E, kk))

        for pair in range(HG_HEADS // 2):
            parts = [_split_bf16(gates[2 * pair + k][1]) for k in range(2)]
            w = jnp.concatenate([jnp.concatenate([parts[0][0], parts[1][0]], axis=1),
                                 jnp.concatenate([parts[0][1], parts[1][1]], axis=1)], axis=0)
            e2 = jnp.dot(sums_ref[...], w, preferred_element_type=F32)
            for k in range(2):
                h = 2 * pair + k
                sl = slice(h * LANES, (h + 1) * LANES)
                q, _, kk = gates[h]
                iv = x_ref[rows, 2 * width + h * LANES:2 * width + (h + 1) * LANES]
                og = x_ref[rows, 3 * width + h * LANES:3 * width + (h + 1) * LANES]
                e = e2[:, k * LANES:(k + 1) * LANES]
                bc = e[nfine * L:(nfine + 1) * L]
                att = jnp.where(level == nlev, jnp.sum(q * kk, axis=1, keepdims=True), 0.0)
                for lvl, b in enumerate(HG_LEVELS):
                    if lvl < nfine:
                        later = sign_ref[lvl * L:(lvl + 1) * L, :] > 0.0
                        xk = jnp.exp2(e[lvl * L:(lvl + 1) * L]) * jnp.where(later, q, kk)
                    else:
                        blocks = []
                        for p in range(0, L, 2 * b):
                            mid = bc[p + b - 1:p + b, :]
                            blocks.append(kk[p:p + b] * jnp.exp2(mid - bc[p:p + b]))
                            blocks.append(q[p + b:p + 2 * b] * jnp.exp2(bc[p + b:p + 2 * b] - mid))
                        xk = jnp.concatenate(blocks, axis=0)
                    xk = xk.astype(BF16)
                    att = jnp.where(level == lvl, lax.dot_general(xk, xk, NT_DIMS, preferred_element_type=F32), att)
                bc_last = bc[L - 1:L, :]
                qd = (q * jnp.exp2(bc)).astype(BF16)
                kd_t = (kk * jnp.exp2(bc_last - bc)).T.astype(BF16)
                ivb = iv.astype(BF16)
                st = st_ref[sl, :]
                o = jnp.dot(jnp.concatenate([att.astype(BF16), qd], axis=1),
                            jnp.concatenate([ivb, st.astype(BF16)], axis=0), preferred_element_type=F32)
                decay_col = jnp.broadcast_to(jnp.exp2(bc_last), (L, LANES)).T
                st_ref[sl, :] = st * decay_col + jnp.dot(kd_t, ivb, preferred_element_type=F32)
                y_ref[rows, sl] = (_rms(o) * nw_ref[:, sl] * _silu(og)).astype(y_ref.dtype)


def _hgrn_prompt(xh, sums, sign, level, lb_rows, nw, batch, seq, nsub):
    rows = nsub * CHUNK
    nc = seq // rows
    width = HG_HEADS * LANES
    full = lambda a: pl.BlockSpec(a.shape, lambda b, c: (0,) * a.ndim)
    return pl.pallas_call(
        _hgrn_prompt_body,
        out_shape=[jax.ShapeDtypeStruct((batch * seq, width), BF16),
                   jax.ShapeDtypeStruct((batch, width, LANES), F32)],
        grid=(batch, nc),
        in_specs=[pl.BlockSpec((rows, 4 * width), lambda b, c: (b * nc + c, 0)),
                  full(sums), full(sign), full(level), full(lb_rows), full(nw)],
        out_specs=[pl.BlockSpec((rows, width), lambda b, c: (b * nc + c, 0)),
                   pl.BlockSpec((None, width, LANES), lambda b, c: (b, 0, 0))],
        compiler_params=_params(("arbitrary", "arbitrary"), 16 * 1024 * 1024),
        name="hgrn_prompt",
    )(xh, sums, sign, level, lb_rows, nw)


def _hi_lo_rows(x_row, row_idx):
    xb = jnp.broadcast_to(x_row, row_idx.shape)
    hi = xb.astype(BF16).astype(F32)
    return jnp.where(row_idx == 0, hi, jnp.where(row_idx == 1, xb - hi, 0.0)).astype(BF16)


def _sample_mixer_body(q8_ref, kn_ref, vn_ref, kc_ref, vc_ref, sink_ref,
                       zx_ref, cst_ref, h0_ref, cw_ref, cb_ref, dtb_ref, a_ref, dskip_ref, snw_ref,
                       xh_ref, s0_ref, lb_ref, hnw_ref,
                       oa_ref, ym_ref, hn_ref, yh_ref, sn_ref, *, n_prev_valid):
    nb = q8_ref.shape[0]
    inner = M_HEADS * M_HEADDIM
    conv_dim = inner + 2 * M_GROUPS * M_STATE
    width = HG_HEADS * LANES
    grp = ATT_HEADS // ATT_KV_HEADS
    row8 = lax.broadcasted_iota(jnp.int32, (SUBLANES, LANES), 0)
    lane8 = lax.broadcasted_iota(jnp.int32, (SUBLANES, LANES), 1)
    hrow = lax.broadcasted_iota(jnp.int32, (ATT_HEADS, 1), 0)
    slope = _alibi_slope_col(hrow)
    jj = lax.broadcasted_iota(jnp.int32, (ATT_HEADS, WINDOW), 1)
    cache_dist = (WINDOW - jj).astype(F32)
    cache_ok = jj >= WINDOW - n_prev_valid
    kv_half = (lane8 // HEAD_DIM) == (row8 // grp)
    lane_in = lax.broadcasted_iota(jnp.int32, (1, inner), 1)
    in_g0 = lane_in < inner // M_GROUPS
    rowx = lax.broadcasted_iota(jnp.int32, (SUBLANES, inner), 0)
    lanex = lax.broadcasted_iota(jnp.int32, (SUBLANES, inner), 1)
    ones_rows01 = jnp.where(lax.broadcasted_iota(jnp.int32, (SUBLANES, LANES), 0) < 2, 1.0, 0.0).astype(BF16)

    for i in range(nb):
        q8 = q8_ref[i]
        kn = kn_ref[i]
        vn = vn_ref[i]
        s_c = lax.dot_general(q8.astype(BF16), kc_ref[i].astype(BF16), NT_DIMS,
                              preferred_element_type=F32) * (HEAD_DIM ** -0.5)
        s_c = jnp.where(cache_ok, s_c - slope * cache_dist, -jnp.inf)
        s_n = jnp.sum(q8 * kn, axis=1, keepdims=True) * (HEAD_DIM ** -0.5)
        sink = sink_ref[:, 0:1]
        m = jnp.maximum(jnp.maximum(jnp.max(s_c, axis=1, keepdims=True), s_n), sink)
        p_c = jnp.exp(s_c - m)
        p_n = jnp.exp(s_n - m)
        den = jnp.sum(p_c, axis=1, keepdims=True) + p_n + jnp.exp(sink - m)
        o = (jnp.dot((p_c / den).astype(BF16), vc_ref[i].astype(BF16), preferred_element_type=F32)
             + (p_n / den) * vn)
        oa_ref[i] = jnp.where(kv_half, o, 0.0)

        zx = zx_ref[i]
        z = zx[:, :inner]
        xbc = zx[:, inner:inner + conv_dim]
        dtraw = zx[:, inner + conv_dim:]
        cst = cst_ref[i]
        conv = xbc * cw_ref[CONV_W - 1:CONV_W, :] + cb_ref[...]
        for w in range(CONV_W - 1):
            conv = conv + cst[w:w + 1, :] * cw_ref[w:w + 1, :]
        act = _silu(conv)
        xs = act[:, :inner]
        bm = act[:, inner:inner + LANES]
        cm = act[:, inner + LANES:]
        dt = _softplus(dtraw + dtb_ref[...])
        da = jnp.exp(dt * a_ref[...])
        h0 = h0_ref[i]
        cm_b = jnp.broadcast_to(cm, (SUBLANES, LANES))
        bm_b = jnp.broadcast_to(bm, (SUBLANES, LANES))
        c8 = jnp.where(row8 == 0, cm_b, jnp.where(row8 == 1, pltpu.roll(cm_b, M_STATE, 1), 0.0))[:, :M_STATE]
        b8 = jnp.where(row8 == 0, bm_b, jnp.where(row8 == 1, pltpu.roll(bm_b, M_STATE, 1), 0.0))[:, :M_STATE]
        ch = lax.dot_general(c8.astype(BF16), h0.astype(BF16), NT_DIMS, preferred_element_type=F32)
        c_dot_h = jnp.where(in_g0, ch[0:1, :], ch[1:2, :])
        cbp = cm * bm
        lane1 = lane8[0:1, :]
        cb0 = jnp.sum(jnp.where(lane1 < M_STATE, cbp, 0.0), axis=1, keepdims=True)
        cb1 = jnp.sum(jnp.where(lane1 < M_STATE, 0.0, cbp), axis=1, keepdims=True)
        y = da * c_dot_h + dt * jnp.where(in_g0, cb0, cb1) * xs + dskip_ref[...] * xs
        y = y * _silu(z)
        ym_ref[i] = _rms(y) * snw_ref[...]
        dtx = jnp.broadcast_to(dt * xs, (SUBLANES, inner))
        lhs_u = jnp.where((rowx == 0) & (lanex < inner // M_GROUPS), dtx,
                          jnp.where((rowx == 1) & (lanex >= inner // M_GROUPS), dtx, 0.0)).astype(BF16)
        upd = lax.dot_general(lhs_u, b8.astype(BF16), TN_DIMS, preferred_element_type=F32)
        da_col = lax.dot_general(_hi_lo_rows(da, rowx), ones_rows01[:, :M_STATE], TN_DIMS,
                                 preferred_element_type=F32)
        hn_ref[i] = h0 * da_col + upd

        xh = xh_ref[i]
        q, logf, kk = _hgrn_gates(xh[:, :width], xh[:, width:2 * width],
                                  lb_ref[0:1, :], lb_ref[1:2, :], lb_ref[2:3, :])
        iv = xh[:, 2 * width:3 * width]
        og = xh[:, 3 * width:]
        f = jnp.exp(logf)
        s0 = s0_ref[i]
        head_of_lane = lanex // LANES
        qf_b = jnp.broadcast_to(q * f, (SUBLANES, width))
        kk_b = jnp.broadcast_to(kk, (SUBLANES, width))
        lhs_q = jnp.where(rowx == head_of_lane, qf_b, 0.0).astype(BF16)
        qs = jnp.dot(lhs_q, s0.astype(BF16), preferred_element_type=F32)
        qkk = q * kk
        o_parts = []
        i_rows = []
        for h in range(HG_HEADS):
            sl = slice(h * LANES, (h + 1) * LANES)
            att = jnp.sum(qkk[:, sl], axis=1, keepdims=True)
            o_h = att * iv[:, sl] + qs[h:h + 1, :]
            o_parts.append(_rms(o_h))
            i_rows.append(iv[:, sl])
        yh_ref[i] = jnp.concatenate(o_parts, axis=1) * hnw_ref[...] * _silu(og)
        i8 = jnp.concatenate(i_rows + [jnp.zeros((SUBLANES - HG_HEADS, LANES), F32)], axis=0)
        lhs_k = jnp.where(rowx == head_of_lane, kk_b, 0.0).astype(BF16)
        upd_s = lax.dot_general(lhs_k, i8.astype(BF16), TN_DIMS, preferred_element_type=F32)
        f_col = lax.dot_general(_hi_lo_rows(f, rowx), ones_rows01, TN_DIMS, preferred_element_type=F32)
        sn_ref[i] = s0 * f_col + upd_s


def _sample_mixer(q8, kn, vn, kc, vc, sink8, zx, cst, h0, conv_w, conv_b, dtb_x, a_x, dskip, snw,
                  xh, s0, lb_rows, hnw, layer, nb):
    n = q8.shape[0]
    per = lambda a: pl.BlockSpec((nb,) + a.shape[1:], lambda i: (i,) + (0,) * (a.ndim - 1))
    lay = lambda a: pl.BlockSpec((None, nb) + a.shape[2:], lambda i: (layer, i) + (0,) * (a.ndim - 2))
    full = lambda a: pl.BlockSpec(a.shape, lambda i: (0,) * a.ndim)
    inner = M_HEADS * M_HEADDIM
    width = HG_HEADS * LANES
    out_shape = [jax.ShapeDtypeStruct((n, ATT_HEADS, LANES), F32),
                 jax.ShapeDtypeStruct((n, 1, inner), F32),
                 jax.ShapeDtypeStruct((n, inner, M_STATE), F32),
                 jax.ShapeDtypeStruct((n, 1, width), F32),
                 jax.ShapeDtypeStruct((n, width, LANES), F32)]
    return pl.pallas_call(
        functools.partial(_sample_mixer_body, n_prev_valid=min(WINDOW, PAST_LEN)),
        out_shape=out_shape,
        grid=(n // nb,),
        in_specs=[per(q8), per(kn), per(vn), lay(kc), lay(vc), full(sink8),
                  per(zx), lay(cst), lay(h0), full(conv_w), full(conv_b), full(dtb_x), full(a_x),
                  full(dskip), full(snw),
                  per(xh), lay(s0), full(lb_rows), full(hnw)],
        out_specs=[pl.BlockSpec((nb,) + s.shape[1:], lambda i: (i, 0, 0)) for s in out_shape],
        compiler_params=_params(("parallel",), 16 * 1024 * 1024),
        name="sample_mixer",
    )(q8, kn, vn, kc, vc, sink8, zx, cst, h0, conv_w, conv_b, dtb_x, a_x, dskip, snw, xh, s0, lb_rows, hnw)


def _row_tile(rows):
    for tm in (512, 256, 128):
        if rows % tm == 0:
            return tm
    return rows


def kernel(x_prompt, x_sample, cache_swa_k, cache_swa_v, state_conv, state_ssm, state_hgrn, ln1_g, ln1_b, ffn1_wg, ffn1_wu, ffn1_wd, w_in, b_in, att_sinks, conv_w, conv_b, dt_bias, a_log, d_skip, ssm_norm_w, hg_lb_logits, hg_norm_w, w_br_att, w_br_ssm, w_br_hg, w_out, ln2_g, ln2_b, ffn2_wg, ffn2_wu, ffn2_wd, ln3_g, ln3_b):
    bp, seq, d = x_prompt.shape
    ns = x_sample.shape[0]
    depth = w_in.shape[0]
    assert seq % CHUNK == 0 and x_sample.shape[1] == 1
    qw = ATT_HEADS * HEAD_DIM
    kvw = ATT_KV_HEADS * HEAD_DIM
    inner = M_HEADS * M_HEADDIM
    conv_dim = inner + 2 * M_GROUPS * M_STATE
    width = HG_HEADS * LANES
    hpg = M_HEADS // M_GROUPS

    bf = lambda a: a.astype(BF16)
    row3 = lambda a: a.reshape(depth, 1, a.shape[-1])
    o_z = qw + 2 * kvw
    o_dt = o_z + inner + conv_dim
    o_h = o_dt + M_HEADS
    o_g = o_h + 4 * width
    dt_pad = jnp.pad(w_in[:, :, o_dt:o_h], ((0, 0), (0, 0), (0, LANES - M_HEADS)))
    dt_rep = jnp.repeat(w_in[:, :, o_dt:o_h], M_HEADDIM, axis=2)
    w_prompt = bf(jnp.concatenate([w_in[:, :, :o_dt], dt_pad, w_in[:, :, o_h:o_g]], axis=2))
    w_sample = bf(jnp.concatenate([w_in[:, :, :o_dt], dt_rep, w_in[:, :, o_h:o_g]], axis=2))
    b_prompt = row3(jnp.concatenate([b_in[:, :o_dt], jnp.pad(b_in[:, o_dt:o_h], ((0, 0), (0, LANES - M_HEADS))),
                                     b_in[:, o_h:o_g]], axis=1))
    b_sample = row3(jnp.concatenate([b_in[:, :o_dt], jnp.repeat(b_in[:, o_dt:o_h], M_HEADDIM, axis=1),
                                     b_in[:, o_h:o_g]], axis=1))
    w_gate, b_gate = bf(w_in[:, :, o_g:]), row3(b_in[:, o_g:])
    widths_p = (qw + 2 * kvw, inner + conv_dim + LANES, 4 * width)
    widths_s = (qw + 2 * kvw, inner + conv_dim + inner, 4 * width)
    ffn1 = (bf(ffn1_wg), bf(ffn1_wu), bf(ffn1_wd), row3(ln1_g), row3(ln1_b))
    ffn2 = (bf(ffn2_wg), bf(ffn2_wu), bf(ffn2_wd), row3(ln3_g), row3(ln3_b))
    merge_w = (w_gate, b_gate, bf(w_br_att), bf(w_br_ssm), bf(w_br_hg), bf(w_out), row3(ln2_g), row3(ln2_b))

    a_neg = -jnp.exp(a_log.astype(F32))
    pad_h = lambda a: jnp.pad(a, ((0, 0), (0, LANES - M_HEADS)))
    rep_h = lambda a: jnp.repeat(a, M_HEADDIM, axis=1)
    lb_all = jnp.cumsum(jax.nn.softmax(hg_lb_logits.astype(F32), axis=0), axis=0)
    lb_all = lb_all - lb_all[0]
    lb_rows = jnp.stack([jnp.log(lb_all), jnp.log1p(-lb_all), 1.0 - lb_all], axis=1)
    lb_rows = jnp.pad(lb_rows, ((0, 0), (0, SUBLANES - 3), (0, 0)))
    hnw = jnp.tile(hg_norm_w, (1, HG_HEADS))
    sums = jnp.asarray(_hgrn_sum_matrix(CHUNK), BF16)
    sign_np, level_np = _hgrn_level_tables(CHUNK)
    hg_sign, hg_level = jnp.asarray(sign_np), jnp.asarray(level_np)
    att_bias = jnp.asarray(_attn_bias(CHUNK))
    tril = jnp.asarray(np.tril(np.ones((CHUNK, CHUNK), np.float32)), BF16)
    sink8 = jnp.broadcast_to(att_sinks[:, :, None], (depth, ATT_HEADS, LANES))

    tm_p = _row_tile(bp * seq)
    tm_s = _row_tile(ns)
    nb_s = SUBLANES if ns % SUBLANES == 0 else 1
    att_sub = max(k for k in (4, 2, 1) if (seq // CHUNK) % k == 0)
    hg_sub = max(k for k in (2, 1) if (seq // CHUNK) % k == 0)

    xp = x_prompt.reshape(bp * seq, d)
    xs = x_sample.reshape(ns, d)
    p_states, s_states = [], []
    for l in range(depth):
        xp = _ffn_ln(xp, *ffn1, l, tm_p)
        qkv, zxd, xh = _inproj(xp, w_prompt, b_prompt, l, tm_p, widths_p)
        ya = _attn_prompt(qkv, att_sinks[l], att_bias, bp, seq, att_sub)
        ym, st_ssm = _ssd_prompt(zxd, conv_w[l], conv_b[l][None], pad_h(dt_bias)[l][None], pad_h(a_neg)[l][None],
                                 rep_h(d_skip)[l][None], ssm_norm_w[l][None], tril, bp, seq)
        yh, st_hg = _hgrn_prompt(xh, sums, hg_sign, hg_level, lb_rows[l], hnw[l][None], bp, seq, hg_sub)
        xp = _merge(xp, ya, ym, yh, *merge_w, l, tm_p)
        xp = _ffn_ln(xp, *ffn2, l, tm_p)
        qkv3 = qkv.reshape(bp, seq, qw + 2 * kvw)
        p_k = qkv3[:, seq - WINDOW:, qw:qw + kvw].reshape(bp, WINDOW, ATT_KV_HEADS, HEAD_DIM)
        p_v = qkv3[:, seq - WINDOW:, qw + kvw:].reshape(bp, WINDOW, ATT_KV_HEADS, HEAD_DIM)
        p_conv = zxd.reshape(bp, seq, -1)[:, seq - (CONV_W - 1):, inner:inner + conv_dim]
        st6 = st_ssm.reshape(bp, M_GROUPS, M_STATE, M_GROUPS, hpg, M_HEADDIM)
        p_ssm = jnp.stack([st6[:, g, :, g] for g in range(M_GROUPS)], axis=1)
        p_ssm = p_ssm.transpose(0, 1, 3, 4, 2).reshape(bp, M_HEADS, M_HEADDIM, M_STATE)
        p_hg = st_hg.reshape(bp, HG_HEADS, LANES, LANES)
        p_states.append((p_k, p_v, p_conv, p_ssm, p_hg))

        xs = _ffn_ln(xs, *ffn1, l, tm_s)
        qkv_s, zx_s, xh_s = _inproj(xs, w_sample, b_sample, l, tm_s, widths_s)
        q4 = qkv_s[:, :qw].reshape(ns, ATT_KV_HEADS, ATT_HEADS // ATT_KV_HEADS, HEAD_DIM)
        zq = jnp.zeros_like(q4[:, 0])
        q8 = jnp.concatenate([jnp.concatenate([q4[:, 0], zq], axis=-1),
                              jnp.concatenate([zq, q4[:, 1]], axis=-1)], axis=1)
        k_new = qkv_s[:, qw:qw + kvw]
        v_new = qkv_s[:, qw + kvw:]
        oa, ym_s, h_new, yh_s, s_new = _sample_mixer(
            q8, k_new[:, None, :], v_new[:, None, :],
            cache_swa_k.reshape(depth, ns, WINDOW, kvw), cache_swa_v.reshape(depth, ns, WINDOW, kvw), sink8[l],
            zx_s[:, None, :], state_conv, state_ssm.reshape(depth, ns, inner, M_STATE),
            conv_w[l], conv_b[l][None], rep_h(dt_bias)[l][None], rep_h(a_neg)[l][None], rep_h(d_skip)[l][None],
            ssm_norm_w[l][None],
            xh_s[:, None, :], state_hgrn.reshape(depth, ns, width, LANES), lb_rows[l], hnw[l][None], l, nb_s)
        grp = ATT_HEADS // ATT_KV_HEADS
        ya_s = jnp.concatenate([oa[:, :grp, :HEAD_DIM].reshape(ns, grp * HEAD_DIM),
                                oa[:, grp:, HEAD_DIM:].reshape(ns, grp * HEAD_DIM)], axis=1)
        xs = _merge(xs, ya_s, ym_s.reshape(ns, inner), yh_s.reshape(ns, width), *merge_w, l, tm_s)
        xs = _ffn_ln(xs, *ffn2, l, tm_s)
        s_k = jnp.concatenate([cache_swa_k[l][:, 1:], k_new.reshape(ns, 1, ATT_KV_HEADS, HEAD_DIM)], axis=1)
        s_v = jnp.concatenate([cache_swa_v[l][:, 1:], v_new.reshape(ns, 1, ATT_KV_HEADS, HEAD_DIM)], axis=1)
        s_conv = jnp.concatenate([state_conv[l][:, 1:], zx_s[:, None, inner:inner + conv_dim]], axis=1)
        s_states.append((s_k, s_v, s_conv, h_new.reshape(ns, M_HEADS, M_HEADDIM, M_STATE),
                         s_new.reshape(ns, HG_HEADS, LANES, LANES)))

    outs_p = [jnp.stack(t) for t in zip(*p_states)]
    outs_s = [jnp.stack(t) for t in zip(*s_states)]
    return (xp.reshape(bp, seq, d), xs.reshape(ns, 1, d), *outs_p, *outs_s)
```

```python
import functools

import numpy as np
import jax
import jax.numpy as jnp
from jax import lax
from jax.experimental import pallas as pl
from jax.experimental.pallas import tpu as pltpu

F32 = jnp.float32
BF16 = jnp.bfloat16

ATT_HEADS = 8
ATT_KV_HEADS = 2
HEAD_DIM = 64
WINDOW = 128
PAST_LEN = 8192
M_HEADS = 8
M_HEADDIM = 64
M_GROUPS = 2
M_STATE = 64
CONV_W = 4
HG_HEADS = 4
DEPTH = 4
ALPHA = (2.0 * DEPTH) ** 0.25
LN_EPS = 1e-5
RMS_EPS = 1e-6
LOG2_E = 1.4426950408889634
CHUNK = 128

V7X_VMEM_BYTES = 64 * 1024 * 1024
LANES = 128
SUBLANES = 8
MXU_WIDTH = 256

NT_DIMS = (((1,), (1,)), ((), ()))
TN_DIMS = (((0,), (0,)), ((), ()))


def _vmem_limit(nbytes):
    return int(min(V7X_VMEM_BYTES - 8 * 1024 * 1024, nbytes + 16 * 1024 * 1024))


def _params(semantics, vmem_bytes):
    return pltpu.CompilerParams(dimension_semantics=semantics, vmem_limit_bytes=_vmem_limit(vmem_bytes))


def _resident(block_shape, index_map):
    return pl.BlockSpec(block_shape, index_map, pipeline_mode=pl.Buffered(1))


def _silu(x):
    return x * jax.nn.sigmoid(x)


def _softplus(x):
    return jnp.maximum(x, 0.0) + jnp.log1p(jnp.exp(-jnp.abs(x)))


def _layernorm(y, g, b):
    mu = jnp.mean(y, axis=-1, keepdims=True)
    d = y - mu
    var = jnp.mean(d * d, axis=-1, keepdims=True)
    return d * lax.rsqrt(var + LN_EPS) * g + b


def _rms(y):
    return y * lax.rsqrt(jnp.mean(y * y, axis=-1, keepdims=True) + RMS_EPS)


def _split_bf16(x):
    hi = x.astype(BF16)
    lo = (x - hi.astype(F32)).astype(BF16)
    return hi, lo


def _col_chunks(width, step):
    return [(c, min(step, width - c)) for c in range(0, width, step)]


def _ffn_ln_body(x_ref, wg_ref, wu_ref, wd_ref, g_ref, b_ref, o_ref, *, ff_chunk):
    x = x_ref[...]
    xb = x.astype(BF16)
    acc = None
    for c0, cw in _col_chunks(wg_ref.shape[1], ff_chunk):
        gate = jnp.dot(xb, wg_ref[:, c0:c0 + cw], preferred_element_type=F32)
        up = jnp.dot(xb, wu_ref[:, c0:c0 + cw], preferred_element_type=F32)
        hid = (_silu(gate) * up).astype(BF16)
        part = jnp.dot(hid, wd_ref[c0:c0 + cw, :], preferred_element_type=F32)
        acc = part if acc is None else acc + part
    o_ref[...] = _layernorm(ALPHA * x + 0.5 * acc, g_ref[...], b_ref[...])


def _ffn_ln(x, wg, wu, wd, g, b, layer, tm):
    rows, d = x.shape
    f = wg.shape[2]
    wspec = lambda shape: _resident((None,) + shape, lambda i: (layer, 0, 0))
    vmem = 3 * d * f * 2 + 4 * tm * d * 4 + 4 * tm * MXU_WIDTH * 4
    return pl.pallas_call(
        functools.partial(_ffn_ln_body, ff_chunk=MXU_WIDTH),
        out_shape=jax.ShapeDtypeStruct((rows, d), F32),
        grid=(rows // tm,),
        in_specs=[pl.BlockSpec((tm, d), lambda i: (i, 0)),
                  wspec((d, f)), wspec((d, f)), wspec((f, d)),
                  wspec((1, d)), wspec((1, d))],
        out_specs=pl.BlockSpec((tm, d), lambda i: (i, 0)),
        compiler_params=_params(("parallel",), vmem),
        name="ffn_ln",
    )(x, wg, wu, wd, g, b)


def _hgrn_decay(hf, log_lb, log1m_lb, one_m_lb):
    e_neg = jnp.exp(-jnp.abs(hf))
    one_p = 1.0 + e_neg
    log_sig = jnp.minimum(hf, 0.0) - jnp.log(one_p)
    inv = 1.0 / one_p
    kk = one_m_lb * jnp.where(hf >= 0.0, e_neg * inv, inv)
    b = log1m_lb + log_sig
    logf = jnp.maximum(log_lb, b) + jnp.log(1.0 + jnp.exp(-jnp.abs(log_lb - b)))
    return logf, kk


def _project_rows(xb, w_ref, b_ref, o_refs):
    base = 0
    for o_ref in o_refs:
        for c0, cw in _col_chunks(o_ref.shape[1], 2 * MXU_WIDTH):
            w = w_ref[:, base + c0:base + c0 + cw]
            o_ref[:, c0:c0 + cw] = (jnp.dot(xb, w, preferred_element_type=F32)
                                    + b_ref[:, base + c0:base + c0 + cw])
        base += o_ref.shape[1]


def _inproj_body(x_ref, w_ref, b_ref, *o_refs):
    _project_rows(x_ref[...].astype(BF16), w_ref, b_ref, o_refs)


def _inproj(x, w, b, layer, tm, widths):
    rows, d = x.shape
    n = w.shape[2]
    assert n == sum(widths)
    vmem = d * n * 2 + 2 * tm * d * 4 + 2 * tm * n * 4
    return pl.pallas_call(
        _inproj_body,
        out_shape=[jax.ShapeDtypeStruct((rows, wd), F32) for wd in widths],
        grid=(rows // tm,),
        in_specs=[pl.BlockSpec((tm, d), lambda i: (i, 0)),
                  _resident((None, d, n), lambda i: (layer, 0, 0)),
                  _resident((None, 1, n), lambda i: (layer, 0, 0))],
        out_specs=[pl.BlockSpec((tm, wd), lambda i: (i, 0)) for wd in widths],
        compiler_params=_params(("parallel",), vmem),
        name="inproj",
    )(x, w, b)


def _inproj_sample_body(x_ref, w_ref, b_ref, wt_ref, bt_ref, *o_refs):
    xb = x_ref[...].astype(BF16)
    _project_rows(xb, w_ref, b_ref, o_refs[:-1])
    o_refs[-1][...] = lax.dot_general(wt_ref[...], xb, NT_DIMS, preferred_element_type=F32) + bt_ref[...]


def _inproj_sample(x, w, b, wt, bt, layer, in_widths):
    rows, d = x.shape
    n, nt = w.shape[2], wt.shape[1]
    vmem = d * (n + nt) * 2 + 2 * rows * d * 4 + 2 * rows * (n + nt) * 4
    return pl.pallas_call(
        _inproj_sample_body,
        out_shape=[jax.ShapeDtypeStruct((rows, wd), F32) for wd in in_widths]
        + [jax.ShapeDtypeStruct((nt, rows), F32)],
        grid=(1,),
        in_specs=[pl.BlockSpec((rows, d), lambda i: (0, 0)),
                  pl.BlockSpec((None, d, n), lambda i: (layer, 0, 0)),
                  pl.BlockSpec((None, 1, n), lambda i: (layer, 0, 0)),
                  pl.BlockSpec((None, nt, d), lambda i: (layer, 0, 0)),
                  pl.BlockSpec((None, nt, 1), lambda i: (layer, 0, 0))],
        out_specs=[pl.BlockSpec((rows, wd), lambda i: (0, 0)) for wd in in_widths]
        + [pl.BlockSpec((nt, rows), lambda i: (0, 0))],
        compiler_params=_params(("arbitrary",), vmem),
        name="inproj_sample",
    )(x, w, b, wt, bt)


def _merge_body(x_ref, ya_ref, ym_ref, yh_ref, wgate_ref, bgate_ref, wa_ref, ws_ref, wh_ref,
                wo_ref, g_ref, b_ref, o_ref):
    x = x_ref[...]
    xb = x.astype(BF16)
    d = x.shape[1]
    branches = ((ya_ref[...].astype(BF16), wa_ref), (ym_ref[...].astype(BF16), ws_ref),
                (yh_ref[...].astype(BF16), wh_ref))
    merged = []
    for c0, cw in _col_chunks(d, MXU_WIDTH):
        m = None
        for k, (yb, w_ref) in enumerate(branches):
            col = k * d + c0
            gate = jax.nn.sigmoid(jnp.dot(xb, wgate_ref[:, col:col + cw], preferred_element_type=F32)
                                  + bgate_ref[:, col:col + cw])
            term = gate * jnp.dot(yb, w_ref[:, c0:c0 + cw], preferred_element_type=F32)
            m = term if m is None else m + term
        merged.append(m.astype(BF16))
    y = jnp.dot(jnp.concatenate(merged, axis=1), wo_ref[...], preferred_element_type=F32)
    o_ref[...] = _layernorm(ALPHA * x + y, g_ref[...], b_ref[...])


def _merge(x, ya, ym, yh, wgate, bgate, wa, ws, wh, wo, g, b, layer, tm):
    rows, d = x.shape
    wb = ya.shape[1]
    wspec = lambda shape: _resident((None,) + shape, lambda i: (layer, 0, 0))
    row = lambda width: pl.BlockSpec((tm, width), lambda i: (i, 0))
    vmem = (3 * d * d + 3 * wb * d + d * d) * 2 + 4 * tm * d * 4 + 6 * tm * wb * 4
    return pl.pallas_call(
        _merge_body,
        out_shape=jax.ShapeDtypeStruct((rows, d), F32),
        grid=(rows // tm,),
        in_specs=[row(d), row(wb), row(wb), row(wb),
                  wspec((d, 3 * d)), wspec((1, 3 * d)),
                  wspec((wb, d)), wspec((wb, d)), wspec((wb, d)), wspec((d, d)),
                  wspec((1, d)), wspec((1, d))],
        out_specs=row(d),
        compiler_params=_params(("parallel",), vmem),
        name="merge_out_ln",
    )(x, ya, ym, yh, wgate, bgate, wa, ws, wh, wo, g, b)


def _alibi_slope_col(head_rows):
    slope = jnp.zeros(head_rows.shape, F32)
    for h in range(ATT_HEADS):
        slope = jnp.where(head_rows == h, 2.0 ** (-8.0 * (h + 1) / ATT_HEADS), slope)
    return slope


def _attn_bias(blk):
    tq = np.arange(blk)[:, None]
    j = np.arange(2 * blk)[None, :]
    dist = WINDOW + tq - j
    ok = (dist >= 0) & (dist <= WINDOW)
    slopes = 2.0 ** (-8.0 * np.arange(1, ATT_HEADS + 1) / ATT_HEADS)
    out = np.empty((2, ATT_HEADS, blk, 2 * blk), np.float32)
    for has_prev in range(2):
        vis = ok & ((j >= blk) | (has_prev == 1))
        out[has_prev] = np.where(vis[None], -slopes[:, None, None] * dist[None], -np.inf)
    return out


def _attn_prompt_body(sink_ref, bias_ref, q_ref, kvc_ref, kvp_ref, o_ref):
    blk = kvp_ref.shape[0]
    nsub = q_ref.shape[0] // blk
    grp = ATT_HEADS // ATT_KV_HEADS
    has_prev = jnp.minimum(pl.program_id(1), 1)
    lane = lax.broadcasted_iota(jnp.int32, (blk, LANES), 1)
    low = lane < HEAD_DIM

    def kv_block(i):
        return kvp_ref[...] if i < 0 else kvc_ref[i * blk:(i + 1) * blk, :]

    for i in range(nsub):
        prev, cur = kv_block(i - 1), kv_block(i)
        kcat = jnp.concatenate([prev[:, :LANES], cur[:, :LANES]], axis=0).astype(BF16)
        vcat = jnp.concatenate([prev[:, LANES:], cur[:, LANES:]], axis=0).astype(BF16)

        def head_out(h):
            kv = h // grp
            qcol = q_ref[i * blk:(i + 1) * blk, (h // 2) * LANES:(h // 2 + 1) * LANES] * (HEAD_DIM ** -0.5)
            if (h % 2) != kv:
                qcol = pltpu.roll(qcol, HEAD_DIM, 1)
            qh = jnp.where(low if kv == 0 else jnp.logical_not(low), qcol, 0.0).astype(BF16)
            bias = bias_ref[has_prev, h] if i == 0 else bias_ref[1, h]
            s = lax.dot_general(qh, kcat, NT_DIMS, preferred_element_type=F32) + bias
            sink = sink_ref[h]
            m = jnp.maximum(jnp.max(s, axis=1, keepdims=True), sink)
            p = jnp.exp(s - m)
            den = jnp.sum(p, axis=1, keepdims=True) + jnp.exp(sink - m)
            p = (p * (1.0 / den)).astype(BF16)
            return jnp.dot(p, vcat, preferred_element_type=F32)

        for c in range(ATT_HEADS // 2):
            even, odd = head_out(2 * c), head_out(2 * c + 1)
            if (2 * c) // grp == 0:
                col = jnp.where(low, even, pltpu.roll(odd, HEAD_DIM, 1))
            else:
                col = jnp.where(low, pltpu.roll(even, HEAD_DIM, 1), odd)
            o_ref[i * blk:(i + 1) * blk, c * LANES:(c + 1) * LANES] = col.astype(o_ref.dtype)


def _attn_prompt(qkv, sinks, bias, batch, seq, nsub):
    rows = nsub * CHUNK
    ns = seq // rows
    qw = ATT_HEADS * HEAD_DIM
    kvw = 2 * ATT_KV_HEADS * HEAD_DIM
    kv_col = qw // kvw
    return pl.pallas_call(
        _attn_prompt_body,
        out_shape=jax.ShapeDtypeStruct((batch * seq, qw), BF16),
        grid=(batch, ns),
        in_specs=[pl.BlockSpec(memory_space=pltpu.SMEM),
                  pl.BlockSpec(bias.shape, lambda b, n: (0, 0, 0, 0)),
                  pl.BlockSpec((rows, qw), lambda b, n: (b * ns + n, 0)),
                  pl.BlockSpec((rows, kvw), lambda b, n: (b * ns + n, kv_col)),
                  pl.BlockSpec((CHUNK, kvw), lambda b, n: ((b * ns + n) * nsub - jnp.minimum(n, 1), kv_col))],
        out_specs=pl.BlockSpec((rows, qw), lambda b, n: (b * ns + n, 0)),
        compiler_params=_params(("parallel", "arbitrary"), 16 * 1024 * 1024),
        name="attn_prompt",
    )(sinks, bias, qkv, qkv, qkv)


def _expand_heads(col_of, lane_low):
    cols = []
    for j in range(M_HEADS // 2):
        cols.append(jnp.where(lane_low, col_of(2 * j), col_of(2 * j + 1)))
    return jnp.concatenate(cols, axis=1)


def _ssd_prompt_body(zxd_ref, cw_ref, cb_ref, dtb_ref, a_ref, dskip_ref, nw_ref, tril_ref,
                     y_ref, st_ref, xpad_ref):
    c = pl.program_id(1)
    L = CHUNK
    inner = M_HEADS * M_HEADDIM
    conv_dim = inner + 2 * M_GROUPS * M_STATE

    @pl.when(c == 0)
    def _():
        st_ref[...] = jnp.zeros_like(st_ref)
        xpad_ref[0:SUBLANES, :] = jnp.zeros((SUBLANES, conv_dim), F32)

    lane = lax.broadcasted_iota(jnp.int32, (L, LANES), 1)
    low = lane < M_STATE
    low1 = lane[0:1, :] < M_STATE
    causal = lax.broadcasted_iota(jnp.int32, (L, L), 0) >= lax.broadcasted_iota(jnp.int32, (L, L), 1)
    srow = lax.broadcasted_iota(jnp.int32, (M_GROUPS * M_STATE, inner), 0) // M_STATE
    scol = lax.broadcasted_iota(jnp.int32, (M_GROUPS * M_STATE, inner), 1) // (inner // M_GROUPS)
    hpg = M_HEADS // M_GROUPS

    for r0 in range(0, zxd_ref.shape[0], L):
        rs = slice(r0, r0 + L)
        z = zxd_ref[rs, 0:inner]
        xbc = zxd_ref[rs, inner:inner + conv_dim]
        dtraw = zxd_ref[rs, inner + conv_dim:]
        xpad_ref[SUBLANES:SUBLANES + L, :] = xbc
        conv = xbc * cw_ref[CONV_W - 1:CONV_W, :] + cb_ref[...]
        for w in range(CONV_W - 1):
            off = SUBLANES - (CONV_W - 1) + w
            conv = conv + xpad_ref[off:off + L, :] * cw_ref[w:w + 1, :]
        xpad_ref[0:SUBLANES, :] = xbc[L - SUBLANES:L, :]
        act = _silu(conv)
        xs = act[:, :inner]
        bm = act[:, inner:inner + LANES]
        cm = act[:, inner + LANES:inner + 2 * LANES]

        dt = _softplus(dtraw + dtb_ref[...])
        dta_hi, dta_lo = _split_bf16(dt * a_ref[...])
        cum = jnp.dot(tril_ref[...], jnp.concatenate([dta_hi, dta_lo], axis=1), preferred_element_type=F32)
        a = cum[:, :LANES] + cum[:, LANES:]
        a_t = a.T
        dt_t = dt.T
        ea = jnp.exp(a)
        a_last = a[L - 1:L, :]
        coef = jnp.exp(a_last - a) * dt
        ea_last = jnp.exp(a_last)

        bmb = bm.astype(BF16)
        cmb = cm.astype(BF16)
        cb_g = [lax.dot_general(jnp.where(low, cm, 0.0).astype(BF16), bmb, NT_DIMS, preferred_element_type=F32),
                lax.dot_general(jnp.where(low, 0.0, cm).astype(BF16), bmb, NT_DIMS, preferred_element_type=F32)]

        def weights(h):
            diff = jnp.broadcast_to(a[:, h:h + 1], (L, L)) - jnp.broadcast_to(a_t[h:h + 1, :], (L, L))
            decay = jnp.exp(jnp.where(causal, diff, -jnp.inf))
            return (cb_g[h // hpg] * decay * jnp.broadcast_to(dt_t[h:h + 1, :], (L, L))).astype(BF16)

        ycols = []
        for j in range(M_HEADS // 2):
            xcol = xs[:, j * LANES:(j + 1) * LANES]
            ycols.append(jnp.dot(weights(2 * j), jnp.where(low, xcol, 0.0).astype(BF16), preferred_element_type=F32)
                         + jnp.dot(weights(2 * j + 1), jnp.where(low, 0.0, xcol).astype(BF16),
                                   preferred_element_type=F32))
        y = jnp.concatenate(ycols, axis=1)

        st = st_ref[...]
        ea_x = _expand_heads(lambda h: jnp.broadcast_to(ea[:, h:h + 1], (L, LANES)), low)
        y = y + jnp.dot(cmb, st.astype(BF16), preferred_element_type=F32) * ea_x

        coef_x = _expand_heads(lambda h: jnp.broadcast_to(coef[:, h:h + 1], (L, LANES)), low)
        cs = jnp.dot(bm.T.astype(BF16), (xs * coef_x).astype(BF16), preferred_element_type=F32)
        dec_x = _expand_heads(lambda h: jnp.broadcast_to(ea_last[:, h:h + 1], (1, LANES)), low1)
        st_ref[...] = st * dec_x + jnp.where(srow == scol, cs, 0.0)

        y = (y + dskip_ref[...] * xs) * _silu(z)
        y_ref[rs, :] = (_rms(y) * nw_ref[...]).astype(y_ref.dtype)


def _ssd_prompt(zxd, conv_w, conv_b, dtb, a_neg, dskip, norm_w, tril, batch, seq, nsub):
    nc = seq // (nsub * CHUNK)
    width = zxd.shape[1]
    inner = M_HEADS * M_HEADDIM
    conv_dim = conv_w.shape[1]
    full = lambda a: pl.BlockSpec(a.shape, lambda b, c: (0,) * a.ndim)
    consts = (conv_w, conv_b, dtb, a_neg, dskip, norm_w, tril)
    return pl.pallas_call(
        _ssd_prompt_body,
        out_shape=[jax.ShapeDtypeStruct((batch * seq, inner), BF16),
                   jax.ShapeDtypeStruct((batch, M_GROUPS * M_STATE, inner), F32)],
        grid=(batch, nc),
        in_specs=[pl.BlockSpec((nsub * CHUNK, width), lambda b, c: (b * nc + c, 0))] + [full(a) for a in consts],
        out_specs=[pl.BlockSpec((nsub * CHUNK, inner), lambda b, c: (b * nc + c, 0)),
                   pl.BlockSpec((None, M_GROUPS * M_STATE, inner), lambda b, c: (b, 0, 0))],
        scratch_shapes=[pltpu.VMEM((SUBLANES + CHUNK, conv_dim), F32)],
        compiler_params=_params(("arbitrary", "arbitrary"), 16 * 1024 * 1024),
        name="ssd_prompt",
    )(zxd, *consts)


HG_LEVELS = tuple(2 ** i for i in range(7))
HG_FINE = tuple(b for b in HG_LEVELS if b < SUBLANES)


def _hgrn_sum_matrix(L):
    r = np.arange(L)
    tt, rr = np.meshgrid(r, r, indexing="ij")
    mats = []
    for b in HG_FINE:
        mid = (r // (2 * b)) * 2 * b + b - 1
        upper = (r // b) % 2 == 1
        up = (mid[:, None] < rr) & (rr <= tt)
        lo = (tt < rr) & (rr <= mid[:, None])
        mats.append(np.where(upper[:, None], up, lo))
    mats.append(rr <= tt)
    m = np.concatenate(mats, axis=0).astype(np.float32)
    return np.concatenate([m, m], axis=1)


def _hgrn_level_tables(L):
    r = np.arange(L)
    later = np.concatenate([np.repeat((((r // b) % 2) == 1)[:, None], LANES, axis=1) for b in HG_LEVELS], axis=0)
    level = np.full((L, L), -1, np.int32)
    for lvl, b in enumerate(HG_LEVELS):
        t_later = ((r // b) % 2 == 1)[:, None]
        s_earlier = ((r // b) % 2 == 0)[None, :]
        same_parent = (r // (2 * b))[:, None] == (r // (2 * b))[None, :]
        level[t_later & s_earlier & same_parent] = lvl
    level[r, r] = len(HG_LEVELS)
    return np.where(later, 1.0, -1.0).astype(np.float32), level


def _hgrn_prompt_body(x_ref, sums_ref, sign_ref, level_ref, lb_ref, nw_ref, y_ref, st_ref):
    c = pl.program_id(1)
    L = CHUNK
    width = HG_HEADS * LANES
    nlev, nfine = len(HG_LEVELS), len(HG_FINE)

    @pl.when(c == 0)
    def _():
        st_ref[...] = jnp.zeros_like(st_ref)

    level = level_ref[...]
    for r0 in range(0, x_ref.shape[0], L):
        rows = slice(r0, r0 + L)
        part = lambda j, h: x_ref[rows, j * width + h * LANES:j * width + (h + 1) * LANES]
        decay = []
        for h in range(HG_HEADS):
            sl = slice(h * LANES, (h + 1) * LANES)
            logf, kk = _hgrn_decay(part(1, h), lb_ref[0:1, sl], lb_ref[1:2, sl], lb_ref[2:3, sl])
            decay.append((logf * LOG2_E, kk))

        for pair in range(HG_HEADS // 2):
            parts = [_split_bf16(decay[2 * pair + k][0]) for k in range(2)]
            w = jnp.concatenate([jnp.concatenate([parts[0][0], parts[1][0]], axis=1),
                                 jnp.concatenate([parts[0][1], parts[1][1]], axis=1)], axis=0)
            e2 = jnp.dot(sums_ref[...], w, preferred_element_type=F32)
            for k in range(2):
                h = 2 * pair + k
                sl = slice(h * LANES, (h + 1) * LANES)
                q, kk, iv, sog = _silu(part(0, h)), decay[h][1], part(2, h), _silu(part(3, h))
                e = e2[:, k * LANES:(k + 1) * LANES]
                bc = e[nfine * L:(nfine + 1) * L]
                att = jnp.where(level == nlev, jnp.sum(q * kk, axis=1, keepdims=True), 0.0)
                for lvl, b in enumerate(HG_LEVELS):
                    if lvl < nfine:
                        later = sign_ref[lvl * L:(lvl + 1) * L, :] > 0.0
                        xk = jnp.exp2(e[lvl * L:(lvl + 1) * L]) * jnp.where(later, q, kk)
                    else:
                        blocks = []
                        for p in range(0, L, 2 * b):
                            mid = bc[p + b - 1:p + b, :]
                            blocks.append(kk[p:p + b] * jnp.exp2(mid - bc[p:p + b]))
                            blocks.append(q[p + b:p + 2 * b] * jnp.exp2(bc[p + b:p + 2 * b] - mid))
                        xk = jnp.concatenate(blocks, axis=0)
                    xk = xk.astype(BF16)
                    att = jnp.where(level == lvl, lax.dot_general(xk, xk, NT_DIMS, preferred_element_type=F32), att)
                bc_last = bc[L - 1:L, :]
                qd = (q * jnp.exp2(bc)).astype(BF16)
                kd_t = (kk * jnp.exp2(bc_last - bc)).T.astype(BF16)
                ivb = iv.astype(BF16)
                st = st_ref[sl, :]
                o = jnp.dot(jnp.concatenate([att.astype(BF16), qd], axis=1),
                            jnp.concatenate([ivb, st.astype(BF16)], axis=0), preferred_element_type=F32)
                decay_col = jnp.broadcast_to(jnp.exp2(bc_last), (L, LANES)).T
                st_ref[sl, :] = st * decay_col + jnp.dot(kd_t, ivb, preferred_element_type=F32)
                y_ref[rows, sl] = (_rms(o) * nw_ref[:, sl] * sog).astype(y_ref.dtype)


def _hgrn_prompt(xh, sums, sign, level, lb_rows, nw, batch, seq, nsub):
    rows = nsub * CHUNK
    nc = seq // rows
    width = HG_HEADS * LANES
    full = lambda a: pl.BlockSpec(a.shape, lambda b, c: (0,) * a.ndim)
    return pl.pallas_call(
        _hgrn_prompt_body,
        out_shape=[jax.ShapeDtypeStruct((batch * seq, width), BF16),
                   jax.ShapeDtypeStruct((batch, width, LANES), F32)],
        grid=(batch, nc),
        in_specs=[pl.BlockSpec((rows, 4 * width), lambda b, c: (b * nc + c, 0)),
                  full(sums), full(sign), full(level), full(lb_rows), full(nw)],
        out_specs=[pl.BlockSpec((rows, width), lambda b, c: (b * nc + c, 0)),
                   pl.BlockSpec((None, width, LANES), lambda b, c: (b, 0, 0))],
        compiler_params=_params(("arbitrary", "arbitrary"), 16 * 1024 * 1024),
        name="hgrn_prompt",
    )(xh, sums, sign, level, lb_rows, nw)


def _hi_lo_rows(x_row, row_idx):
    xb = jnp.broadcast_to(x_row, row_idx.shape)
    hi = xb.astype(BF16).astype(F32)
    return jnp.where(row_idx == 0, hi, jnp.where(row_idx == 1, xb - hi, 0.0)).astype(BF16)


def _sample_attn_hgrn_body(q8_ref, kr_ref, vr_ref, kvc_ref, kt_ref, vt_ref, sink_ref,
                           xh_ref, s0_ref, lb_ref, hnw_ref, kt_stack_ref, vt_stack_ref, s_stack_ref,
                           oa_ref, ktn_ref, vtn_ref, yh_ref, sn_ref, *, n_prev_valid):
    del kt_stack_ref, vt_stack_ref, s_stack_ref
    nb = q8_ref.shape[0]
    width = HG_HEADS * LANES
    grp = ATT_HEADS // ATT_KV_HEADS
    row8 = lax.broadcasted_iota(jnp.int32, (SUBLANES, LANES), 0)
    lane8 = lax.broadcasted_iota(jnp.int32, (SUBLANES, LANES), 1)
    hrow = lax.broadcasted_iota(jnp.int32, (ATT_HEADS, 1), 0)
    slope = _alibi_slope_col(hrow)
    jj = lax.broadcasted_iota(jnp.int32, (ATT_HEADS, WINDOW), 1)
    cache_dist = (WINDOW - jj).astype(F32)
    cache_ok = jj >= WINDOW - n_prev_valid
    kv_half = (lane8 // HEAD_DIM) == (row8 // grp)
    last_lane = lax.broadcasted_iota(jnp.int32, (LANES, WINDOW), 1) == WINDOW - 1
    rowx = lax.broadcasted_iota(jnp.int32, (SUBLANES, width), 0)
    lanex = lax.broadcasted_iota(jnp.int32, (SUBLANES, width), 1)
    ones_rows01 = jnp.where(row8 < 2, 1.0, 0.0).astype(BF16)
    kvw = ATT_KV_HEADS * HEAD_DIM

    for i in range(nb):
        q8 = q8_ref[i]
        kt, vt = kt_ref[i], vt_ref[i]
        s_c = jnp.dot(q8.astype(BF16), kt.astype(BF16), preferred_element_type=F32) * (HEAD_DIM ** -0.5)
        s_c = jnp.where(cache_ok, s_c - slope * cache_dist, -jnp.inf)
        s_n = jnp.sum(q8 * kr_ref[i], axis=1, keepdims=True) * (HEAD_DIM ** -0.5)
        sink = sink_ref[:, 0:1]
        m = jnp.maximum(jnp.maximum(jnp.max(s_c, axis=1, keepdims=True), s_n), sink)
        p_c = jnp.exp(s_c - m)
        p_n = jnp.exp(s_n - m)
        den = jnp.sum(p_c, axis=1, keepdims=True) + p_n + jnp.exp(sink - m)
        o = (lax.dot_general((p_c / den).astype(BF16), vt.astype(BF16), NT_DIMS, preferred_element_type=F32)
             + (p_n / den) * vr_ref[i])
        oa_ref[i] = jnp.where(kv_half, o, 0.0)
        ktn_ref[i] = jnp.where(last_lane, kvc_ref[0:kvw, i:i + 1], pltpu.roll(kt, WINDOW - 1, 1))
        vtn_ref[i] = jnp.where(last_lane, kvc_ref[kvw:2 * kvw, i:i + 1], pltpu.roll(vt, WINDOW - 1, 1))

        xh = xh_ref[i]
        q = _silu(xh[:, :width])
        logf, kk = _hgrn_decay(xh[:, width:2 * width], lb_ref[0:1, :], lb_ref[1:2, :], lb_ref[2:3, :])
        iv = xh[:, 2 * width:3 * width]
        og = xh[:, 3 * width:]
        f = jnp.exp(logf)
        s0 = s0_ref[i]
        head_of_lane = lanex // LANES
        qf_b = jnp.broadcast_to(q * f, (SUBLANES, width))
        kk_b = jnp.broadcast_to(kk, (SUBLANES, width))
        lhs_q = jnp.where(rowx == head_of_lane, qf_b, 0.0).astype(BF16)
        qs = jnp.dot(lhs_q, s0.astype(BF16), preferred_element_type=F32)
        qkk = q * kk
        o_parts = []
        i_rows = []
        for h in range(HG_HEADS):
            sl = slice(h * LANES, (h + 1) * LANES)
            att = jnp.sum(qkk[:, sl], axis=1, keepdims=True)
            o_h = att * iv[:, sl] + qs[h:h + 1, :]
            o_parts.append(_rms(o_h))
            i_rows.append(iv[:, sl])
        yh_ref[i] = jnp.concatenate(o_parts, axis=1) * hnw_ref[...] * _silu(og)
        i8 = jnp.concatenate(i_rows + [jnp.zeros((SUBLANES - HG_HEADS, LANES), F32)], axis=0)
        lhs_k = jnp.where(rowx == head_of_lane, kk_b, 0.0).astype(BF16)
        upd_s = lax.dot_general(lhs_k, i8.astype(BF16), TN_DIMS, preferred_element_type=F32)
        f_col = lax.dot_general(_hi_lo_rows(f, rowx), ones_rows01, TN_DIMS, preferred_element_type=F32)
        sn_ref[i] = s0 * f_col + upd_s


def _sample_attn_hgrn(q8, kr, vr, kvc, kt, vt, sink8, xh, s0, lb_rows, hnw, kt_stack, vt_stack, s_stack, layer, nb):
    n = q8.shape[0]
    per = lambda a: pl.BlockSpec((nb,) + a.shape[1:], lambda i: (i,) + (0,) * (a.ndim - 1))
    lay = lambda a: pl.BlockSpec((None, nb) + a.shape[2:], lambda i: (layer, i) + (0,) * (a.ndim - 2))
    full = lambda a: pl.BlockSpec(a.shape, lambda i: (0,) * a.ndim)
    hbm = pl.BlockSpec(memory_space=pl.ANY)
    width = HG_HEADS * LANES
    out_shape = [jax.ShapeDtypeStruct((n, ATT_HEADS, LANES), F32),
                 jax.ShapeDtypeStruct(kt_stack.shape, F32),
                 jax.ShapeDtypeStruct(vt_stack.shape, F32),
                 jax.ShapeDtypeStruct((n, 1, width), F32),
                 jax.ShapeDtypeStruct(s_stack.shape, F32)]
    return pl.pallas_call(
        functools.partial(_sample_attn_hgrn_body, n_prev_valid=min(WINDOW, PAST_LEN)),
        out_shape=out_shape,
        grid=(n // nb,),
        in_specs=[per(q8), per(kr), per(vr), pl.BlockSpec((None,) + kvc.shape[1:], lambda i: (i, 0, 0)),
                  lay(kt), lay(vt), full(sink8), per(xh), lay(s0), full(lb_rows), full(hnw), hbm, hbm, hbm],
        out_specs=[per(q8), lay(kt_stack), lay(vt_stack),
                   pl.BlockSpec((nb, 1, width), lambda i: (i, 0, 0)), lay(s_stack)],
        input_output_aliases={11: 1, 12: 2, 13: 4},
        compiler_params=_params(("parallel",), 16 * 1024 * 1024),
        name="sample_attn_hgrn",
    )(q8, kr, vr, kvc, kt, vt, sink8, xh, s0, lb_rows, hnw, kt_stack, vt_stack, s_stack)


def _ssd_sample_body(col_ref, cst_ref, h0_ref, cw_ref, cb_ref, dtb_ref, a_ref, dskip_ref, nw_ref, stack_ref,
                     ym_ref, hn_ref, xs_s, b_s, c_s, dt_s, da_s, yrow_s, y_s):
    del stack_ref
    h = pl.program_id(0)
    inner = M_HEADS * M_HEADDIM
    conv_dim = inner + 2 * M_GROUPS * M_STATE
    kvw2 = 2 * ATT_KV_HEADS * HEAD_DIM
    z0, x0, d0 = kvw2, kvw2 + inner, kvw2 + inner + conv_dim
    ns = col_ref.shape[1]

    @pl.when(h == 0)
    def _():
        conv = col_ref[x0:d0, :] * cw_ref[CONV_W - 1] + cb_ref[...]
        for w in range(CONV_W - 1):
            conv = conv + cst_ref[w] * cw_ref[w]
        act = _silu(conv)
        xs_s[...] = act[:inner]
        b_s[...] = act[inner:inner + M_GROUPS * M_STATE]
        c_s[...] = act[inner + M_GROUPS * M_STATE:]
        dt = _softplus(col_ref[d0:d0 + M_HEADS, :] + dtb_ref[...])
        da = jnp.exp(dt * a_ref[...])
        for hh in range(M_HEADS):
            dt_s[hh] = jnp.broadcast_to(dt[hh:hh + 1, :], (SUBLANES, ns))
            da_s[hh] = jnp.broadcast_to(da[hh:hh + 1, :], (SUBLANES, ns))

    hpg = M_HEADS // M_GROUPS
    x_h = xs_s[pl.ds(pl.multiple_of(h * M_HEADDIM, M_HEADDIM), M_HEADDIM), :]
    g0 = pl.multiple_of((h // hpg) * M_STATE, M_STATE)
    b_g = b_s[pl.ds(g0, M_STATE), :]
    c_g = c_s[pl.ds(g0, M_STATE), :]
    dt = dt_s[h][0:1, :]
    da = da_s[h][0:1, :]
    dtx = x_h * dt
    for p in range(M_HEADDIM):
        s_p = h0_ref[p * M_STATE:(p + 1) * M_STATE, :]
        yrow_s[p:p + 1, :] = jnp.sum(c_g * s_p, axis=0, keepdims=True)
        hn_ref[p * M_STATE:(p + 1) * M_STATE, :] = da * s_p + b_g * dtx[p:p + 1, :]
    cb = jnp.sum(c_g * b_g, axis=0, keepdims=True)
    y_s[pl.ds(pl.multiple_of(h * M_HEADDIM, M_HEADDIM), M_HEADDIM), :] = (
        da * yrow_s[...] + (dt * cb + dskip_ref[h][0:1, :]) * x_h)

    @pl.when(h == pl.num_programs(0) - 1)
    def _():
        y = y_s[...] * _silu(col_ref[z0:x0, :])
        y = y * lax.rsqrt(jnp.mean(y * y, axis=0, keepdims=True) + RMS_EPS) * nw_ref[...]
        for j in range(inner // LANES):
            ym_ref[:, j * LANES:(j + 1) * LANES] = y[j * LANES:(j + 1) * LANES, :].T


def _ssd_sample(col, cst, h0, cw, cb, dtb, a_neg, dskip, nw, stack, layer):
    ns = col.shape[1]
    inner = M_HEADS * M_HEADDIM
    full = lambda a: pl.BlockSpec(a.shape, lambda h: (0,) * a.ndim)
    state_block = pl.BlockSpec((None, None) + h0.shape[2:], lambda h: (layer, h, 0, 0))
    return pl.pallas_call(
        _ssd_sample_body,
        out_shape=[jax.ShapeDtypeStruct((ns, inner), F32), jax.ShapeDtypeStruct(stack.shape, F32)],
        grid=(M_HEADS,),
        in_specs=[full(col), pl.BlockSpec((None,) + cst.shape[1:], lambda h: (layer, 0, 0, 0)), state_block,
                  full(cw), full(cb), full(dtb), full(a_neg), full(dskip), full(nw),
                  pl.BlockSpec(memory_space=pl.ANY)],
        out_specs=[pl.BlockSpec((ns, inner), lambda h: (0, 0)), state_block],
        scratch_shapes=[pltpu.VMEM((inner, ns), F32), pltpu.VMEM((M_GROUPS * M_STATE, ns), F32),
                        pltpu.VMEM((M_GROUPS * M_STATE, ns), F32), pltpu.VMEM((M_HEADS, SUBLANES, ns), F32),
                        pltpu.VMEM((M_HEADS, SUBLANES, ns), F32), pltpu.VMEM((M_HEADDIM, ns), F32),
                        pltpu.VMEM((inner, ns), F32)],
        input_output_aliases={9: 1},
        compiler_params=_params(("arbitrary",), 24 * 1024 * 1024),
        name="ssd_sample",
    )(col, cst, h0, cw, cb, dtb, a_neg, dskip, nw, stack)


def _row_tile(rows):
    for tm in (512, 256, 128):
        if rows % tm == 0:
            return tm
    return rows


def kernel(x_prompt, x_sample, cache_swa_k, cache_swa_v, state_conv, state_ssm, state_hgrn, ln1_g, ln1_b, ffn1_wg, ffn1_wu, ffn1_wd, w_in, b_in, att_sinks, conv_w, conv_b, dt_bias, a_log, d_skip, ssm_norm_w, hg_lb_logits, hg_norm_w, w_br_att, w_br_ssm, w_br_hg, w_out, ln2_g, ln2_b, ffn2_wg, ffn2_wu, ffn2_wd, ln3_g, ln3_b):
    bp, seq, d = x_prompt.shape
    ns = x_sample.shape[0]
    depth = w_in.shape[0]
    assert seq % CHUNK == 0 and x_sample.shape[1] == 1
    qw = ATT_HEADS * HEAD_DIM
    kvw = ATT_KV_HEADS * HEAD_DIM
    inner = M_HEADS * M_HEADDIM
    conv_dim = inner + 2 * M_GROUPS * M_STATE
    width = HG_HEADS * LANES
    hpg = M_HEADS // M_GROUPS

    bf = lambda a: a.astype(BF16)
    row3 = lambda a: a.reshape(depth, 1, a.shape[-1])
    o_z = qw + 2 * kvw
    o_dt = o_z + inner + conv_dim
    o_h = o_dt + M_HEADS
    o_g = o_h + 4 * width
    dt_pad = jnp.pad(w_in[:, :, o_dt:o_h], ((0, 0), (0, 0), (0, LANES - M_HEADS)))
    w_rows = bf(jnp.concatenate([w_in[:, :, :o_dt], dt_pad, w_in[:, :, o_h:o_g]], axis=2))
    b_rows = row3(jnp.concatenate([b_in[:, :o_dt], jnp.pad(b_in[:, o_dt:o_h], ((0, 0), (0, LANES - M_HEADS))),
                                   b_in[:, o_h:o_g]], axis=1))
    w_cols = bf(jnp.swapaxes(w_in[:, :, qw:o_h], 1, 2))
    b_cols = b_in[:, qw:o_h, None]
    w_gate, b_gate = bf(w_in[:, :, o_g:]), row3(b_in[:, o_g:])
    in_widths = (qw + 2 * kvw, inner + conv_dim + LANES, 4 * width)
    ffn1 = (bf(ffn1_wg), bf(ffn1_wu), bf(ffn1_wd), row3(ln1_g), row3(ln1_b))
    ffn2 = (bf(ffn2_wg), bf(ffn2_wu), bf(ffn2_wd), row3(ln3_g), row3(ln3_b))
    merge_w = (w_gate, b_gate, bf(w_br_att), bf(w_br_ssm), bf(w_br_hg), bf(w_out), row3(ln2_g), row3(ln2_b))

    a_neg = -jnp.exp(a_log.astype(F32))
    pad_h = lambda a: jnp.pad(a, ((0, 0), (0, LANES - M_HEADS)))
    rep_h = lambda a: jnp.repeat(a, M_HEADDIM, axis=1)
    lb_all = jnp.cumsum(jax.nn.softmax(hg_lb_logits.astype(F32), axis=0), axis=0)
    lb_all = lb_all - lb_all[0]
    lb_rows = jnp.stack([jnp.log(lb_all), jnp.log1p(-lb_all), 1.0 - lb_all], axis=1)
    lb_rows = jnp.pad(lb_rows, ((0, 0), (0, SUBLANES - 3), (0, 0)))
    hnw = jnp.tile(hg_norm_w, (1, HG_HEADS))
    sums = jnp.asarray(_hgrn_sum_matrix(CHUNK), BF16)
    sign_np, level_np = _hgrn_level_tables(CHUNK)
    hg_sign, hg_level = jnp.asarray(sign_np), jnp.asarray(level_np)
    att_bias = jnp.asarray(_attn_bias(CHUNK))
    tril = jnp.asarray(np.tril(np.ones((CHUNK, CHUNK), np.float32)), BF16)
    sink8 = jnp.broadcast_to(att_sinks[:, :, None], (depth, ATT_HEADS, LANES))

    tm_p = _row_tile(bp * seq)
    tm_s = _row_tile(ns)
    nb_s = SUBLANES if ns % SUBLANES == 0 else 1
    lanes_b = lambda a: jnp.broadcast_to(a[..., None], a.shape + (ns,))
    kt_in = cache_swa_k.transpose(0, 1, 3, 4, 2).reshape(depth, ns, kvw, WINDOW)
    vt_in = cache_swa_v.transpose(0, 1, 3, 4, 2).reshape(depth, ns, kvw, WINDOW)
    ssm_in = state_ssm.transpose(0, 2, 3, 4, 1).reshape(depth, M_HEADS, M_HEADDIM * M_STATE, ns)
    conv_in = state_conv.transpose(0, 2, 3, 1)
    hg_in = state_hgrn.reshape(depth, ns, width, LANES)
    kt_out, vt_out = jnp.zeros(kt_in.shape, F32), jnp.zeros(vt_in.shape, F32)
    ssm_out, hg_out = jnp.zeros(ssm_in.shape, F32), jnp.zeros(hg_in.shape, F32)
    cw_b, cb_b = lanes_b(conv_w), lanes_b(conv_b)
    dtb_b, a_b, nw_b = lanes_b(dt_bias), lanes_b(a_neg), lanes_b(ssm_norm_w)
    dskip_b = jnp.broadcast_to(d_skip[:, :, None, None], (depth, M_HEADS, SUBLANES, ns))
    att_sub = max(k for k in (4, 2, 1) if (seq // CHUNK) % k == 0)
    scan_sub = max(k for k in (2, 1) if (seq // CHUNK) % k == 0)

    xp = x_prompt.reshape(bp * seq, d)
    xs = x_sample.reshape(ns, d)
    p_states, s_conv = [], []
    for l in range(depth):
        xp = _ffn_ln(xp, *ffn1, l, tm_p)
        qkv, zxd, xh = _inproj(xp, w_rows, b_rows, l, tm_p, in_widths)
        ya = _attn_prompt(qkv, att_sinks[l], att_bias, bp, seq, att_sub)
        ym, st_ssm = _ssd_prompt(zxd, conv_w[l], conv_b[l][None], pad_h(dt_bias)[l][None], pad_h(a_neg)[l][None],
                                 rep_h(d_skip)[l][None], ssm_norm_w[l][None], tril, bp, seq, scan_sub)
        yh, st_hg = _hgrn_prompt(xh, sums, hg_sign, hg_level, lb_rows[l], hnw[l][None], bp, seq, scan_sub)
        xp = _merge(xp, ya, ym, yh, *merge_w, l, tm_p)
        xp = _ffn_ln(xp, *ffn2, l, tm_p)
        qkv3 = qkv.reshape(bp, seq, qw + 2 * kvw)
        p_k = qkv3[:, seq - WINDOW:, qw:qw + kvw].reshape(bp, WINDOW, ATT_KV_HEADS, HEAD_DIM)
        p_v = qkv3[:, seq - WINDOW:, qw + kvw:].reshape(bp, WINDOW, ATT_KV_HEADS, HEAD_DIM)
        p_conv = zxd.reshape(bp, seq, -1)[:, seq - (CONV_W - 1):, inner:inner + conv_dim]
        st6 = st_ssm.reshape(bp, M_GROUPS, M_STATE, M_GROUPS, hpg, M_HEADDIM)
        p_ssm = jnp.stack([st6[:, g, :, g] for g in range(M_GROUPS)], axis=1)
        p_ssm = p_ssm.transpose(0, 1, 3, 4, 2).reshape(bp, M_HEADS, M_HEADDIM, M_STATE)
        p_hg = st_hg.reshape(bp, HG_HEADS, LANES, LANES)
        p_states.append((p_k, p_v, p_conv, p_ssm, p_hg))

        xs = _ffn_ln(xs, *ffn1, l, tm_s)
        qkv_s, zx_s, xh_s, col_s = _inproj_sample(xs, w_rows, b_rows, w_cols, b_cols, l, in_widths)
        q4 = qkv_s[:, :qw].reshape(ns, ATT_KV_HEADS, ATT_HEADS // ATT_KV_HEADS, HEAD_DIM)
        zq = jnp.zeros_like(q4[:, 0])
        q8 = jnp.concatenate([jnp.concatenate([q4[:, 0], zq], axis=-1),
                              jnp.concatenate([zq, q4[:, 1]], axis=-1)], axis=1)
        kv_cols = col_s[:2 * kvw].reshape(2 * kvw, ns // nb_s, nb_s).transpose(1, 0, 2)
        oa, kt_out, vt_out, yh_s, hg_out = _sample_attn_hgrn(
            q8, qkv_s[:, None, qw:qw + kvw], qkv_s[:, None, qw + kvw:], kv_cols, kt_in, vt_in, sink8[l],
            xh_s[:, None, :], hg_in, lb_rows[l], hnw[l][None], kt_out, vt_out, hg_out, l, nb_s)
        ym_s, ssm_out = _ssd_sample(col_s, conv_in, ssm_in, cw_b[l], cb_b[l], dtb_b[l], a_b[l], dskip_b[l],
                                    nw_b[l], ssm_out, l)
        grp = ATT_HEADS // ATT_KV_HEADS
        ya_s = jnp.concatenate([oa[:, :grp, :HEAD_DIM].reshape(ns, grp * HEAD_DIM),
                                oa[:, grp:, HEAD_DIM:].reshape(ns, grp * HEAD_DIM)], axis=1)
        xs = _merge(xs, ya_s, ym_s, yh_s.reshape(ns, width), *merge_w, l, tm_s)
        xs = _ffn_ln(xs, *ffn2, l, tm_s)
        s_conv.append(jnp.concatenate([state_conv[l][:, 1:], zx_s[:, None, inner:inner + conv_dim]], axis=1))

    outs_p = [jnp.stack(t) for t in zip(*p_states)]
    unwind = lambda t: t.reshape(depth, ns, ATT_KV_HEADS, HEAD_DIM, WINDOW).transpose(0, 1, 4, 2, 3)
    s_ssm = ssm_out.reshape(depth, M_HEADS, M_HEADDIM, M_STATE, ns).transpose(0, 4, 1, 2, 3)
    outs_s = [unwind(kt_out), unwind(vt_out), jnp.stack(s_conv), s_ssm,
              hg_out.reshape(depth, ns, HG_HEADS, LANES, LANES)]
    return (xp.reshape(bp, seq, d), xs.reshape(ns, 1, d), *outs_p, *outs_s)
```

```python
import functools

import numpy as np
import jax
import jax.numpy as jnp
from jax import lax
from jax.experimental import pallas as pl
from jax.experimental.pallas import tpu as pltpu

F32 = jnp.float32
BF16 = jnp.bfloat16

ATT_HEADS = 8
ATT_KV_HEADS = 2
HEAD_DIM = 64
WINDOW = 128
PAST_LEN = 8192
M_HEADS = 8
M_HEADDIM = 64
M_GROUPS = 2
M_STATE = 64
CONV_W = 4
HG_HEADS = 4
DEPTH = 4
ALPHA = (2.0 * DEPTH) ** 0.25
LN_EPS = 1e-5
RMS_EPS = 1e-6
LOG2_E = 1.4426950408889634
CHUNK = 128

V7X_VMEM_BYTES = 64 * 1024 * 1024
LANES = 128
SUBLANES = 8
MXU_WIDTH = 256

NT_DIMS = (((1,), (1,)), ((), ()))
TN_DIMS = (((0,), (0,)), ((), ()))


def _vmem_limit(nbytes):
    return int(min(V7X_VMEM_BYTES - 8 * 1024 * 1024, nbytes + 16 * 1024 * 1024))


def _params(semantics, vmem_bytes):
    return pltpu.CompilerParams(dimension_semantics=semantics, vmem_limit_bytes=_vmem_limit(vmem_bytes))


def _layer_block(a, layer):
    zeros = (0,) * (a.ndim - 1)
    return pl.BlockSpec((None,) + a.shape[1:], lambda *_: (layer,) + zeros)


def _resident(block_shape, index_map):
    return pl.BlockSpec(block_shape, index_map, pipeline_mode=pl.Buffered(1))


def _silu(x):
    return x * jax.nn.sigmoid(x)


def _softplus(x):
    return jnp.maximum(x, 0.0) + jnp.log(1.0 + jnp.exp(-jnp.abs(x)))


def _layernorm(y, g, b):
    mu = jnp.mean(y, axis=-1, keepdims=True)
    d = y - mu
    var = jnp.mean(d * d, axis=-1, keepdims=True)
    return d * lax.rsqrt(var + LN_EPS) * g + b


def _rms(y):
    return y * lax.rsqrt(jnp.mean(y * y, axis=-1, keepdims=True) + RMS_EPS)


def _split_bf16(x):
    hi = x.astype(BF16)
    lo = (x - hi.astype(F32)).astype(BF16)
    return hi, lo


def _col_chunks(width, step):
    return [(c, min(step, width - c)) for c in range(0, width, step)]


def _ffn_ln_body(x_ref, wg_ref, wu_ref, wd_ref, g_ref, b_ref, o_ref, *, ff_chunk):
    x = x_ref[...]
    xb = x.astype(BF16)
    acc = None
    for c0, cw in _col_chunks(wg_ref.shape[1], ff_chunk):
        gate = jnp.dot(xb, wg_ref[:, c0:c0 + cw], preferred_element_type=F32)
        up = jnp.dot(xb, wu_ref[:, c0:c0 + cw], preferred_element_type=F32)
        hid = (_silu(gate) * up).astype(BF16)
        part = jnp.dot(hid, wd_ref[c0:c0 + cw, :], preferred_element_type=F32)
        acc = part if acc is None else acc + part
    o_ref[...] = _layernorm(ALPHA * x + 0.5 * acc, g_ref[...], b_ref[...])


def _ffn_ln(x, wg, wu, wd, g, b, layer, tm):
    rows, d = x.shape
    f = wg.shape[2]
    wspec = lambda shape: _resident((None,) + shape, lambda i: (layer, 0, 0))
    vmem = 3 * d * f * 2 + 4 * tm * d * 4 + 4 * tm * MXU_WIDTH * 4
    return pl.pallas_call(
        functools.partial(_ffn_ln_body, ff_chunk=MXU_WIDTH),
        out_shape=jax.ShapeDtypeStruct((rows, d), F32),
        grid=(rows // tm,),
        in_specs=[pl.BlockSpec((tm, d), lambda i: (i, 0)),
                  wspec((d, f)), wspec((d, f)), wspec((f, d)),
                  wspec((1, d)), wspec((1, d))],
        out_specs=pl.BlockSpec((tm, d), lambda i: (i, 0)),
        compiler_params=_params(("parallel",), vmem),
        name="ffn_ln",
    )(x, wg, wu, wd, g, b)


def _hgrn_decay(hf, log_lb, log1m_lb, one_m_lb):
    e_neg = jnp.exp(-jnp.abs(hf))
    one_p = 1.0 + e_neg
    log_sig = jnp.minimum(hf, 0.0) - jnp.log(one_p)
    inv = 1.0 / one_p
    kk = one_m_lb * jnp.where(hf >= 0.0, e_neg * inv, inv)
    b = log1m_lb + log_sig
    logf = jnp.maximum(log_lb, b) + jnp.log(1.0 + jnp.exp(-jnp.abs(log_lb - b)))
    return logf, kk


def _project_rows(xb, wt_ref, b_ref, o_refs):
    base = 0
    for o_ref in o_refs:
        for c0, cw in _col_chunks(o_ref.shape[1], 2 * MXU_WIDTH):
            w = wt_ref[base + c0:base + c0 + cw, :]
            o_ref[:, c0:c0 + cw] = (lax.dot_general(xb, w, NT_DIMS, preferred_element_type=F32)
                                    + b_ref[:, base + c0:base + c0 + cw])
        base += o_ref.shape[1]


def _inproj_body(x_ref, wt_ref, b_ref, *o_refs):
    _project_rows(x_ref[...].astype(BF16), wt_ref, b_ref, o_refs)


def _inproj(x, wt, b, layer, tm, widths):
    rows, d = x.shape
    n = wt.shape[1]
    assert n == sum(widths)
    vmem = d * n * 2 + 2 * tm * d * 4 + 2 * tm * n * 4
    return pl.pallas_call(
        _inproj_body,
        out_shape=[jax.ShapeDtypeStruct((rows, wd), F32) for wd in widths],
        grid=(rows // tm,),
        in_specs=[pl.BlockSpec((tm, d), lambda i: (i, 0)),
                  _resident((None, n, d), lambda i: (layer, 0, 0)),
                  _resident((None, 1, n), lambda i: (layer, 0, 0))],
        out_specs=[pl.BlockSpec((tm, wd), lambda i: (i, 0)) for wd in widths],
        compiler_params=_params(("parallel",), vmem),
        name="inproj",
    )(x, wt, b)


def _inproj_sample_body(x_ref, wt_ref, b_ref, bt_ref, *o_refs, col_rows):
    xb = x_ref[...].astype(BF16)
    _project_rows(xb, wt_ref, b_ref, o_refs[:-1])
    r0, r1 = col_rows
    o_refs[-1][...] = lax.dot_general(wt_ref[r0:r1, :], xb, NT_DIMS, preferred_element_type=F32) + bt_ref[...]


def _inproj_sample(x, wt, b, bt, layer, widths, col_rows):
    rows, d = x.shape
    n, nt = wt.shape[1], col_rows[1] - col_rows[0]
    vmem = d * n * 2 + 2 * rows * d * 4 + 2 * rows * (n + nt) * 4
    return pl.pallas_call(
        functools.partial(_inproj_sample_body, col_rows=col_rows),
        out_shape=[jax.ShapeDtypeStruct((rows, wd), F32) for wd in widths]
        + [jax.ShapeDtypeStruct((nt, rows), F32)],
        grid=(1,),
        in_specs=[pl.BlockSpec((rows, d), lambda i: (0, 0)),
                  pl.BlockSpec((None, n, d), lambda i: (layer, 0, 0)),
                  pl.BlockSpec((None, 1, n), lambda i: (layer, 0, 0)),
                  pl.BlockSpec((None, nt, 1), lambda i: (layer, 0, 0))],
        out_specs=[pl.BlockSpec((rows, wd), lambda i: (0, 0)) for wd in widths]
        + [pl.BlockSpec((nt, rows), lambda i: (0, 0))],
        compiler_params=_params(("arbitrary",), vmem),
        name="inproj_sample",
    )(x, wt, b, bt)


def _merge_body(x_ref, ya_ref, ym_ref, yh_ref, wgate_ref, bgate_ref, wa_ref, ws_ref, wh_ref,
                wo_ref, g_ref, b_ref, o_ref):
    x = x_ref[...]
    xb = x.astype(BF16)
    d = x.shape[1]
    branches = ((ya_ref[...].astype(BF16), wa_ref), (ym_ref[...].astype(BF16), ws_ref),
                (yh_ref[...].astype(BF16), wh_ref))
    merged = []
    for c0, cw in _col_chunks(d, MXU_WIDTH):
        m = None
        for k, (yb, w_ref) in enumerate(branches):
            col = k * d + c0
            gate = jax.nn.sigmoid(lax.dot_general(xb, wgate_ref[col:col + cw, :], NT_DIMS,
                                                  preferred_element_type=F32) + bgate_ref[:, col:col + cw])
            term = gate * jnp.dot(yb, w_ref[:, c0:c0 + cw], preferred_element_type=F32)
            m = term if m is None else m + term
        merged.append(m.astype(BF16))
    y = jnp.dot(jnp.concatenate(merged, axis=1), wo_ref[...], preferred_element_type=F32)
    o_ref[...] = _layernorm(ALPHA * x + y, g_ref[...], b_ref[...])


def _merge(x, ya, ym, yh, wgate, bgate, wa, ws, wh, wo, g, b, layer, tm):
    rows, d = x.shape
    wb = ya.shape[1]
    wspec = lambda shape: _resident((None,) + shape, lambda i: (layer, 0, 0))
    row = lambda width: pl.BlockSpec((tm, width), lambda i: (i, 0))
    vmem = (3 * d * d + 3 * wb * d + d * d) * 2 + 4 * tm * d * 4 + 6 * tm * wb * 4
    return pl.pallas_call(
        _merge_body,
        out_shape=jax.ShapeDtypeStruct((rows, d), F32),
        grid=(rows // tm,),
        in_specs=[row(d), row(wb), row(wb), row(wb),
                  wspec((3 * d, d)), wspec((1, 3 * d)),
                  wspec((wb, d)), wspec((wb, d)), wspec((wb, d)), wspec((d, d)),
                  wspec((1, d)), wspec((1, d))],
        out_specs=row(d),
        compiler_params=_params(("parallel",), vmem),
        name="merge_out_ln",
    )(x, ya, ym, yh, wgate, bgate, wa, ws, wh, wo, g, b)


def _alibi_slope_col(head_rows):
    slope = jnp.zeros(head_rows.shape, F32)
    for h in range(ATT_HEADS):
        slope = jnp.where(head_rows == h, 2.0 ** (-8.0 * (h + 1) / ATT_HEADS), slope)
    return slope


def _attn_bias(blk):
    tq = np.arange(blk)[:, None]
    j = np.arange(2 * blk)[None, :]
    dist = WINDOW + tq - j
    ok = (dist >= 0) & (dist <= WINDOW)
    slopes = 2.0 ** (-8.0 * np.arange(1, ATT_HEADS + 1) / ATT_HEADS)
    out = np.empty((2, ATT_HEADS, blk, 2 * blk), np.float32)
    for has_prev in range(2):
        vis = ok & ((j >= blk) | (has_prev == 1))
        out[has_prev] = np.where(vis[None], -slopes[:, None, None] * dist[None], -np.inf)
    return out


def _attn_prompt_body(sink_ref, bias_ref, q_ref, kvc_ref, kvp_ref, o_ref, *, layer):
    blk = kvp_ref.shape[0]
    nsub = q_ref.shape[0] // blk
    grp = ATT_HEADS // ATT_KV_HEADS
    has_prev = jnp.minimum(pl.program_id(1), 1)
    lane = lax.broadcasted_iota(jnp.int32, (blk, LANES), 1)
    low = lane < HEAD_DIM

    def kv_block(i):
        return kvp_ref[...] if i < 0 else kvc_ref[i * blk:(i + 1) * blk, :]

    for i in range(nsub):
        prev, cur = kv_block(i - 1), kv_block(i)
        kcat = jnp.concatenate([prev[:, :LANES], cur[:, :LANES]], axis=0).astype(BF16)
        vcat = jnp.concatenate([prev[:, LANES:], cur[:, LANES:]], axis=0).astype(BF16)

        def head_out(h):
            kv = h // grp
            qcol = q_ref[i * blk:(i + 1) * blk, (h // 2) * LANES:(h // 2 + 1) * LANES] * (HEAD_DIM ** -0.5)
            if (h % 2) != kv:
                qcol = pltpu.roll(qcol, HEAD_DIM, 1)
            qh = jnp.where(low if kv == 0 else jnp.logical_not(low), qcol, 0.0).astype(BF16)
            bias = bias_ref[has_prev, h] if i == 0 else bias_ref[1, h]
            s = lax.dot_general(qh, kcat, NT_DIMS, preferred_element_type=F32) + bias
            sink = sink_ref[layer, h]
            m = jnp.maximum(jnp.max(s, axis=1, keepdims=True), sink)
            p = jnp.exp(s - m)
            den = jnp.sum(p, axis=1, keepdims=True) + jnp.exp(sink - m)
            p = (p * (1.0 / den)).astype(BF16)
            return jnp.dot(p, vcat, preferred_element_type=F32)

        for c in range(ATT_HEADS // 2):
            even, odd = head_out(2 * c), head_out(2 * c + 1)
            if (2 * c) // grp == 0:
                col = jnp.where(low, even, pltpu.roll(odd, HEAD_DIM, 1))
            else:
                col = jnp.where(low, pltpu.roll(even, HEAD_DIM, 1), odd)
            o_ref[i * blk:(i + 1) * blk, c * LANES:(c + 1) * LANES] = col.astype(o_ref.dtype)


def _attn_prompt(qkv, sinks, bias, layer, batch, seq, nsub):
    rows = nsub * CHUNK
    ns = seq // rows
    qw = ATT_HEADS * HEAD_DIM
    kvw = 2 * ATT_KV_HEADS * HEAD_DIM
    kv_col = qw // kvw
    return pl.pallas_call(
        functools.partial(_attn_prompt_body, layer=layer),
        out_shape=jax.ShapeDtypeStruct((batch * seq, qw), BF16),
        grid=(batch, ns),
        in_specs=[pl.BlockSpec(memory_space=pltpu.SMEM),
                  pl.BlockSpec(bias.shape, lambda b, n: (0,) * bias.ndim),
                  pl.BlockSpec((rows, qw), lambda b, n: (b * ns + n, 0)),
                  pl.BlockSpec((rows, kvw), lambda b, n: (b * ns + n, kv_col)),
                  pl.BlockSpec((CHUNK, kvw), lambda b, n: ((b * ns + n) * nsub - jnp.minimum(n, 1), kv_col))],
        out_specs=pl.BlockSpec((rows, qw), lambda b, n: (b * ns + n, 0)),
        compiler_params=_params(("parallel", "arbitrary"), 16 * 1024 * 1024),
        name="attn_prompt",
    )(sinks, bias, qkv, qkv, qkv)


def _expand_heads(col_of, lane_low):
    cols = []
    for j in range(M_HEADS // 2):
        cols.append(jnp.where(lane_low, col_of(2 * j), col_of(2 * j + 1)))
    return jnp.concatenate(cols, axis=1)


def _ssd_prompt_body(zxd_ref, cw_ref, cb_ref, dtb_ref, a_ref, dskip_ref, nw_ref, tril_ref,
                     y_ref, st_ref, xpad_ref):
    c = pl.program_id(1)
    L = CHUNK
    inner = M_HEADS * M_HEADDIM
    conv_dim = inner + 2 * M_GROUPS * M_STATE

    @pl.when(c == 0)
    def _():
        st_ref[...] = jnp.zeros_like(st_ref)
        xpad_ref[0:SUBLANES, :] = jnp.zeros((SUBLANES, conv_dim), F32)

    lane = lax.broadcasted_iota(jnp.int32, (L, LANES), 1)
    low = lane < M_STATE
    low1 = lane[0:1, :] < M_STATE
    causal = lax.broadcasted_iota(jnp.int32, (L, L), 0) >= lax.broadcasted_iota(jnp.int32, (L, L), 1)
    srow = lax.broadcasted_iota(jnp.int32, (M_GROUPS * M_STATE, inner), 0) // M_STATE
    scol = lax.broadcasted_iota(jnp.int32, (M_GROUPS * M_STATE, inner), 1) // (inner // M_GROUPS)
    hpg = M_HEADS // M_GROUPS

    for r0 in range(0, zxd_ref.shape[0], L):
        rs = slice(r0, r0 + L)
        z = zxd_ref[rs, 0:inner]
        xbc = zxd_ref[rs, inner:inner + conv_dim]
        dtraw = zxd_ref[rs, inner + conv_dim:]
        xpad_ref[SUBLANES:SUBLANES + L, :] = xbc
        conv = xbc * cw_ref[CONV_W - 1:CONV_W, :] + cb_ref[...]
        for w in range(CONV_W - 1):
            off = SUBLANES - (CONV_W - 1) + w
            conv = conv + xpad_ref[off:off + L, :] * cw_ref[w:w + 1, :]
        xpad_ref[0:SUBLANES, :] = xbc[L - SUBLANES:L, :]
        act = _silu(conv)
        xs = act[:, :inner]
        bm = act[:, inner:inner + LANES]
        cm = act[:, inner + LANES:inner + 2 * LANES]

        dt = _softplus(dtraw + dtb_ref[...])
        dta_hi, dta_lo = _split_bf16(dt * a_ref[...])
        cum = jnp.dot(tril_ref[...], jnp.concatenate([dta_hi, dta_lo], axis=1), preferred_element_type=F32)
        a = cum[:, :LANES] + cum[:, LANES:]
        a_t = a.T
        dt_t = dt.T
        ea = jnp.exp(a)
        a_last = a[L - 1:L, :]
        coef = jnp.exp(a_last - a) * dt
        ea_last = jnp.exp(a_last)

        bmb = bm.astype(BF16)
        cmb = cm.astype(BF16)
        cb_g = [lax.dot_general(jnp.where(low, cm, 0.0).astype(BF16), bmb, NT_DIMS, preferred_element_type=F32),
                lax.dot_general(jnp.where(low, 0.0, cm).astype(BF16), bmb, NT_DIMS, preferred_element_type=F32)]

        def weights(h):
            diff = jnp.broadcast_to(a[:, h:h + 1], (L, L)) - jnp.broadcast_to(a_t[h:h + 1, :], (L, L))
            decay = jnp.exp(jnp.where(causal, diff, -jnp.inf))
            return (cb_g[h // hpg] * decay * jnp.broadcast_to(dt_t[h:h + 1, :], (L, L))).astype(BF16)

        ycols = []
        for j in range(M_HEADS // 2):
            xcol = xs[:, j * LANES:(j + 1) * LANES]
            ycols.append(jnp.dot(weights(2 * j), jnp.where(low, xcol, 0.0).astype(BF16), preferred_element_type=F32)
                         + jnp.dot(weights(2 * j + 1), jnp.where(low, 0.0, xcol).astype(BF16),
                                   preferred_element_type=F32))
        y = jnp.concatenate(ycols, axis=1)

        st = st_ref[...]
        ea_x = _expand_heads(lambda h: jnp.broadcast_to(ea[:, h:h + 1], (L, LANES)), low)
        y = y + jnp.dot(cmb, st.astype(BF16), preferred_element_type=F32) * ea_x

        coef_x = _expand_heads(lambda h: jnp.broadcast_to(coef[:, h:h + 1], (L, LANES)), low)
        cs = jnp.dot(bm.T.astype(BF16), (xs * coef_x).astype(BF16), preferred_element_type=F32)
        dec_x = _expand_heads(lambda h: jnp.broadcast_to(ea_last[:, h:h + 1], (1, LANES)), low1)
        st_ref[...] = st * dec_x + jnp.where(srow == scol, cs, 0.0)

        y = (y + dskip_ref[...] * xs) * _silu(z)
        y_ref[rs, :] = (_rms(y) * nw_ref[...]).astype(y_ref.dtype)


def _ssd_prompt(zxd, conv_w, conv_b, dtb, a_neg, dskip, norm_w, tril, layer, batch, seq, nsub):
    nc = seq // (nsub * CHUNK)
    width = zxd.shape[1]
    inner = M_HEADS * M_HEADDIM
    conv_dim = conv_w.shape[2]
    consts = (conv_w, conv_b, dtb, a_neg, dskip, norm_w)
    return pl.pallas_call(
        _ssd_prompt_body,
        out_shape=[jax.ShapeDtypeStruct((batch * seq, inner), BF16),
                   jax.ShapeDtypeStruct((batch, M_GROUPS * M_STATE, inner), F32)],
        grid=(batch, nc),
        in_specs=[pl.BlockSpec((nsub * CHUNK, width), lambda b, c: (b * nc + c, 0))]
        + [_layer_block(a, layer) for a in consts] + [pl.BlockSpec(tril.shape, lambda b, c: (0, 0))],
        out_specs=[pl.BlockSpec((nsub * CHUNK, inner), lambda b, c: (b * nc + c, 0)),
                   pl.BlockSpec((None, M_GROUPS * M_STATE, inner), lambda b, c: (b, 0, 0))],
        scratch_shapes=[pltpu.VMEM((SUBLANES + CHUNK, conv_dim), F32)],
        compiler_params=_params(("arbitrary", "arbitrary"), 16 * 1024 * 1024),
        name="ssd_prompt",
    )(zxd, *consts, tril)


HG_LEVELS = tuple(2 ** i for i in range(7))
HG_FINE = tuple(b for b in HG_LEVELS if b < SUBLANES)


def _hgrn_sum_matrix(L):
    r = np.arange(L)
    tt, rr = np.meshgrid(r, r, indexing="ij")
    mats = []
    for b in HG_FINE:
        mid = (r // (2 * b)) * 2 * b + b - 1
        upper = (r // b) % 2 == 1
        up = (mid[:, None] < rr) & (rr <= tt)
        lo = (tt < rr) & (rr <= mid[:, None])
        mats.append(np.where(upper[:, None], up, lo))
    mats.append(rr <= tt)
    m = np.concatenate(mats, axis=0).astype(np.float32)
    return np.concatenate([m, m], axis=1)


def _hgrn_level_tables(L):
    r = np.arange(L)
    later = np.concatenate([np.repeat((((r // b) % 2) == 1)[:, None], LANES, axis=1) for b in HG_LEVELS], axis=0)
    level = np.full((L, L), -1, np.int32)
    for lvl, b in enumerate(HG_LEVELS):
        t_later = ((r // b) % 2 == 1)[:, None]
        s_earlier = ((r // b) % 2 == 0)[None, :]
        same_parent = (r // (2 * b))[:, None] == (r // (2 * b))[None, :]
        level[t_later & s_earlier & same_parent] = lvl
    level[r, r] = len(HG_LEVELS)
    return np.where(later, 1.0, -1.0).astype(np.float32), level


def _hgrn_prompt_body(x_ref, sums_ref, sign_ref, level_ref, lb_ref, nw_ref, y_ref, st_ref):
    c = pl.program_id(1)
    L = CHUNK
    width = HG_HEADS * LANES
    nlev, nfine = len(HG_LEVELS), len(HG_FINE)

    @pl.when(c == 0)
    def _():
        st_ref[...] = jnp.zeros_like(st_ref)

    level = level_ref[...]
    for r0 in range(0, x_ref.shape[0], L):
        rows = slice(r0, r0 + L)
        part = lambda j, h: x_ref[rows, j * width + h * LANES:j * width + (h + 1) * LANES]
        decay = []
        for h in range(HG_HEADS):
            sl = slice(h * LANES, (h + 1) * LANES)
            logf, kk = _hgrn_decay(part(1, h), lb_ref[0:1, sl], lb_ref[1:2, sl], lb_ref[2:3, sl])
            decay.append((logf * LOG2_E, kk))

        for pair in range(HG_HEADS // 2):
            parts = [_split_bf16(decay[2 * pair + k][0]) for k in range(2)]
            w = jnp.concatenate([jnp.concatenate([parts[0][0], parts[1][0]], axis=1),
                                 jnp.concatenate([parts[0][1], parts[1][1]], axis=1)], axis=0)
            e2 = jnp.dot(sums_ref[...], w, preferred_element_type=F32)
            for k in range(2):
                h = 2 * pair + k
                sl = slice(h * LANES, (h + 1) * LANES)
                q, kk, iv, sog = _silu(part(0, h)), decay[h][1], part(2, h), _silu(part(3, h))
                e = e2[:, k * LANES:(k + 1) * LANES]
                bc = e[nfine * L:(nfine + 1) * L]
                att = jnp.where(level == nlev, jnp.sum(q * kk, axis=1, keepdims=True), 0.0)
                for lvl, b in enumerate(HG_LEVELS):
                    if lvl < nfine:
                        later = sign_ref[lvl * L:(lvl + 1) * L, :] > 0.0
                        xk = jnp.exp2(e[lvl * L:(lvl + 1) * L]) * jnp.where(later, q, kk)
                    else:
                        blocks = []
                        for p in range(0, L, 2 * b):
                            mid = bc[p + b - 1:p + b, :]
                            blocks.append(kk[p:p + b] * jnp.exp2(mid - bc[p:p + b]))
                            blocks.append(q[p + b:p + 2 * b] * jnp.exp2(bc[p + b:p + 2 * b] - mid))
                        xk = jnp.concatenate(blocks, axis=0)
                    xk = xk.astype(BF16)
                    att = jnp.where(level == lvl, lax.dot_general(xk, xk, NT_DIMS, preferred_element_type=F32), att)
                bc_last = bc[L - 1:L, :]
                qd = (q * jnp.exp2(bc)).astype(BF16)
                kd_t = (kk * jnp.exp2(bc_last - bc)).T.astype(BF16)
                ivb = iv.astype(BF16)
                st = st_ref[sl, :]
                o = jnp.dot(jnp.concatenate([att.astype(BF16), qd], axis=1),
                            jnp.concatenate([ivb, st.astype(BF16)], axis=0), preferred_element_type=F32)
                decay_col = jnp.broadcast_to(jnp.exp2(bc_last), (L, LANES)).T
                st_ref[sl, :] = st * decay_col + jnp.dot(kd_t, ivb, preferred_element_type=F32)
                y_ref[rows, sl] = (_rms(o) * nw_ref[:, sl] * sog).astype(y_ref.dtype)


def _hgrn_prompt(xh, sums, sign, level, lb_rows, nw, layer, batch, seq, nsub):
    rows = nsub * CHUNK
    nc = seq // rows
    width = HG_HEADS * LANES
    full = lambda a: pl.BlockSpec(a.shape, lambda b, c: (0,) * a.ndim)
    return pl.pallas_call(
        _hgrn_prompt_body,
        out_shape=[jax.ShapeDtypeStruct((batch * seq, width), BF16),
                   jax.ShapeDtypeStruct((batch, width, LANES), F32)],
        grid=(batch, nc),
        in_specs=[pl.BlockSpec((rows, 4 * width), lambda b, c: (b * nc + c, 0)),
                  full(sums), full(sign), full(level), _layer_block(lb_rows, layer), _layer_block(nw, layer)],
        out_specs=[pl.BlockSpec((rows, width), lambda b, c: (b * nc + c, 0)),
                   pl.BlockSpec((None, width, LANES), lambda b, c: (b, 0, 0))],
        compiler_params=_params(("arbitrary", "arbitrary"), 16 * 1024 * 1024),
        name="hgrn_prompt",
    )(xh, sums, sign, level, lb_rows, nw)


def _hi_lo_rows(x_row, row_idx):
    xb = jnp.broadcast_to(x_row, row_idx.shape)
    hi = xb.astype(BF16).astype(F32)
    return jnp.where(row_idx == 0, hi, jnp.where(row_idx == 1, xb - hi, 0.0)).astype(BF16)


def _sample_attn_hgrn_body(q8_ref, kr_ref, vr_ref, kvc_ref, kt_ref, vt_ref, sink_ref,
                           xh_ref, s0_ref, lb_ref, hnw_ref, kt_stack_ref, vt_stack_ref, s_stack_ref,
                           oa_ref, ktn_ref, vtn_ref, yh_ref, sn_ref, *, n_prev_valid):
    del kt_stack_ref, vt_stack_ref, s_stack_ref
    nb = q8_ref.shape[0]
    width = HG_HEADS * LANES
    grp = ATT_HEADS // ATT_KV_HEADS
    row8 = lax.broadcasted_iota(jnp.int32, (SUBLANES, LANES), 0)
    lane8 = lax.broadcasted_iota(jnp.int32, (SUBLANES, LANES), 1)
    hrow = lax.broadcasted_iota(jnp.int32, (ATT_HEADS, 1), 0)
    slope = _alibi_slope_col(hrow)
    jj = lax.broadcasted_iota(jnp.int32, (ATT_HEADS, WINDOW), 1)
    cache_dist = (WINDOW - jj).astype(F32)
    cache_ok = jj >= WINDOW - n_prev_valid
    kv_half = (lane8 // HEAD_DIM) == (row8 // grp)
    last_lane = lax.broadcasted_iota(jnp.int32, (LANES, WINDOW), 1) == WINDOW - 1
    rowx = lax.broadcasted_iota(jnp.int32, (SUBLANES, width), 0)
    lanex = lax.broadcasted_iota(jnp.int32, (SUBLANES, width), 1)
    ones_rows01 = jnp.where(row8 < 2, 1.0, 0.0).astype(BF16)
    kvw = ATT_KV_HEADS * HEAD_DIM

    for i in range(nb):
        q8 = q8_ref[i]
        kt, vt = kt_ref[i], vt_ref[i]
        s_c = jnp.dot(q8.astype(BF16), kt.astype(BF16), preferred_element_type=F32) * (HEAD_DIM ** -0.5)
        s_c = jnp.where(cache_ok, s_c - slope * cache_dist, -jnp.inf)
        s_n = jnp.sum(q8 * kr_ref[i], axis=1, keepdims=True) * (HEAD_DIM ** -0.5)
        sink = sink_ref[:, 0:1]
        m = jnp.maximum(jnp.maximum(jnp.max(s_c, axis=1, keepdims=True), s_n), sink)
        p_c = jnp.exp(s_c - m)
        p_n = jnp.exp(s_n - m)
        den = jnp.sum(p_c, axis=1, keepdims=True) + p_n + jnp.exp(sink - m)
        o = (lax.dot_general((p_c / den).astype(BF16), vt.astype(BF16), NT_DIMS, preferred_element_type=F32)
             + (p_n / den) * vr_ref[i])
        oa_ref[i] = jnp.where(kv_half, o, 0.0)
        ktn_ref[i] = jnp.where(last_lane, kvc_ref[0:kvw, i:i + 1], pltpu.roll(kt, WINDOW - 1, 1))
        vtn_ref[i] = jnp.where(last_lane, kvc_ref[kvw:2 * kvw, i:i + 1], pltpu.roll(vt, WINDOW - 1, 1))

        xh = xh_ref[i]
        q = _silu(xh[:, :width])
        logf, kk = _hgrn_decay(xh[:, width:2 * width], lb_ref[0:1, :], lb_ref[1:2, :], lb_ref[2:3, :])
        iv = xh[:, 2 * width:3 * width]
        og = xh[:, 3 * width:]
        f = jnp.exp(logf)
        s0 = s0_ref[i]
        head_of_lane = lanex // LANES
        qf_b = jnp.broadcast_to(q * f, (SUBLANES, width))
        kk_b = jnp.broadcast_to(kk, (SUBLANES, width))
        lhs_q = jnp.where(rowx == head_of_lane, qf_b, 0.0).astype(BF16)
        qs = jnp.dot(lhs_q, s0.astype(BF16), preferred_element_type=F32)
        qkk = q * kk
        o_parts = []
        i_rows = []
        for h in range(HG_HEADS):
            sl = slice(h * LANES, (h + 1) * LANES)
            att = jnp.sum(qkk[:, sl], axis=1, keepdims=True)
            o_h = att * iv[:, sl] + qs[h:h + 1, :]
            o_parts.append(_rms(o_h))
            i_rows.append(iv[:, sl])
        yh_ref[i] = jnp.concatenate(o_parts, axis=1) * hnw_ref[...] * _silu(og)
        i8 = jnp.concatenate(i_rows + [jnp.zeros((SUBLANES - HG_HEADS, LANES), F32)], axis=0)
        lhs_k = jnp.where(rowx == head_of_lane, kk_b, 0.0).astype(BF16)
        upd_s = lax.dot_general(lhs_k, i8.astype(BF16), TN_DIMS, preferred_element_type=F32)
        f_col = lax.dot_general(_hi_lo_rows(f, rowx), ones_rows01, TN_DIMS, preferred_element_type=F32)
        sn_ref[i] = s0 * f_col + upd_s


def _sample_attn_hgrn(q8, kr, vr, kvc, kt, vt, sink8, xh, s0, lb_rows, hnw, kt_stack, vt_stack, s_stack, layer, nb):
    n = q8.shape[0]
    per = lambda a: pl.BlockSpec((nb,) + a.shape[1:], lambda i: (i,) + (0,) * (a.ndim - 1))
    lay = lambda a: pl.BlockSpec((None, nb) + a.shape[2:], lambda i: (layer, i) + (0,) * (a.ndim - 2))
    hbm = pl.BlockSpec(memory_space=pl.ANY)
    width = HG_HEADS * LANES
    out_shape = [jax.ShapeDtypeStruct((n, ATT_HEADS, LANES), F32),
                 jax.ShapeDtypeStruct(kt_stack.shape, F32),
                 jax.ShapeDtypeStruct(vt_stack.shape, F32),
                 jax.ShapeDtypeStruct((n, 1, width), F32),
                 jax.ShapeDtypeStruct(s_stack.shape, F32)]
    return pl.pallas_call(
        functools.partial(_sample_attn_hgrn_body, n_prev_valid=min(WINDOW, PAST_LEN)),
        out_shape=out_shape,
        grid=(n // nb,),
        in_specs=[per(q8), per(kr), per(vr), pl.BlockSpec((None,) + kvc.shape[1:], lambda i: (i, 0, 0)),
                  lay(kt), lay(vt), _layer_block(sink8, layer), per(xh), lay(s0),
                  _layer_block(lb_rows, layer), _layer_block(hnw, layer), hbm, hbm, hbm],
        out_specs=[per(q8), lay(kt_stack), lay(vt_stack),
                   pl.BlockSpec((nb, 1, width), lambda i: (i, 0, 0)), lay(s_stack)],
        input_output_aliases={11: 1, 12: 2, 13: 4},
        compiler_params=_params(("parallel",), 16 * 1024 * 1024),
        name="sample_attn_hgrn",
    )(q8, kr, vr, kvc, kt, vt, sink8, xh, s0, lb_rows, hnw, kt_stack, vt_stack, s_stack)


def _ssd_sample_body(col_ref, cst_ref, h0_ref, cw_ref, cb_ref, dtb_ref, a_ref, dskip_ref, nw_ref, stack_ref,
                     ym_ref, hn_ref, xs_s, b_s, c_s, dt_s, da_s, yrow_s, y_s):
    del stack_ref
    h = pl.program_id(0)
    inner = M_HEADS * M_HEADDIM
    conv_dim = inner + 2 * M_GROUPS * M_STATE
    kvw2 = 2 * ATT_KV_HEADS * HEAD_DIM
    z0, x0, d0 = kvw2, kvw2 + inner, kvw2 + inner + conv_dim
    ns = col_ref.shape[1]

    @pl.when(h == 0)
    def _():
        conv = col_ref[x0:d0, :] * cw_ref[CONV_W - 1] + cb_ref[...]
        for w in range(CONV_W - 1):
            conv = conv + cst_ref[w] * cw_ref[w]
        act = _silu(conv)
        xs_s[...] = act[:inner]
        b_s[...] = act[inner:inner + M_GROUPS * M_STATE]
        c_s[...] = act[inner + M_GROUPS * M_STATE:]
        dt = _softplus(col_ref[d0:d0 + M_HEADS, :] + dtb_ref[...])
        da = jnp.exp(dt * a_ref[...])
        for hh in range(M_HEADS):
            dt_s[hh] = jnp.broadcast_to(dt[hh:hh + 1, :], (SUBLANES, ns))
            da_s[hh] = jnp.broadcast_to(da[hh:hh + 1, :], (SUBLANES, ns))

    hpg = M_HEADS // M_GROUPS
    x_h = xs_s[pl.ds(pl.multiple_of(h * M_HEADDIM, M_HEADDIM), M_HEADDIM), :]
    g0 = pl.multiple_of((h // hpg) * M_STATE, M_STATE)
    b_g = b_s[pl.ds(g0, M_STATE), :]
    c_g = c_s[pl.ds(g0, M_STATE), :]
    dt = dt_s[h][0:1, :]
    da = da_s[h][0:1, :]
    dtx = x_h * dt
    for p in range(M_HEADDIM):
        s_p = h0_ref[p * M_STATE:(p + 1) * M_STATE, :]
        yrow_s[p:p + 1, :] = jnp.sum(c_g * s_p, axis=0, keepdims=True)
        hn_ref[p * M_STATE:(p + 1) * M_STATE, :] = da * s_p + b_g * dtx[p:p + 1, :]
    cb = jnp.sum(c_g * b_g, axis=0, keepdims=True)
    y_s[pl.ds(pl.multiple_of(h * M_HEADDIM, M_HEADDIM), M_HEADDIM), :] = (
        da * yrow_s[...] + (dt * cb + dskip_ref[h][0:1, :]) * x_h)

    @pl.when(h == pl.num_programs(0) - 1)
    def _():
        y = y_s[...] * _silu(col_ref[z0:x0, :])
        y = y * lax.rsqrt(jnp.mean(y * y, axis=0, keepdims=True) + RMS_EPS) * nw_ref[...]
        for j in range(inner // LANES):
            ym_ref[:, j * LANES:(j + 1) * LANES] = y[j * LANES:(j + 1) * LANES, :].T


def _ssd_sample(col, cst, h0, cw, cb, dtb, a_neg, dskip, nw, stack, layer):
    ns = col.shape[1]
    inner = M_HEADS * M_HEADDIM
    state_block = pl.BlockSpec((None, None) + h0.shape[2:], lambda h: (layer, h, 0, 0))
    return pl.pallas_call(
        _ssd_sample_body,
        out_shape=[jax.ShapeDtypeStruct((ns, inner), F32), jax.ShapeDtypeStruct(stack.shape, F32)],
        grid=(M_HEADS,),
        in_specs=[pl.BlockSpec(col.shape, lambda h: (0, 0)), _layer_block(cst, layer), state_block]
        + [_layer_block(a, layer) for a in (cw, cb, dtb, a_neg, dskip, nw)] + [pl.BlockSpec(memory_space=pl.ANY)],
        out_specs=[pl.BlockSpec((ns, inner), lambda h: (0, 0)), state_block],
        scratch_shapes=[pltpu.VMEM((inner, ns), F32), pltpu.VMEM((M_GROUPS * M_STATE, ns), F32),
                        pltpu.VMEM((M_GROUPS * M_STATE, ns), F32), pltpu.VMEM((M_HEADS, SUBLANES, ns), F32),
                        pltpu.VMEM((M_HEADS, SUBLANES, ns), F32), pltpu.VMEM((M_HEADDIM, ns), F32),
                        pltpu.VMEM((inner, ns), F32)],
        input_output_aliases={9: 1},
        compiler_params=_params(("arbitrary",), 24 * 1024 * 1024),
        name="ssd_sample",
    )(col, cst, h0, cw, cb, dtb, a_neg, dskip, nw, stack)


def _row_tile(rows):
    for tm in (512, 256, 128):
        if rows % tm == 0:
            return tm
    return rows


def kernel(x_prompt, x_sample, cache_swa_k, cache_swa_v, state_conv, state_ssm, state_hgrn, ln1_g, ln1_b, ffn1_wg, ffn1_wu, ffn1_wd, w_in, b_in, att_sinks, conv_w, conv_b, dt_bias, a_log, d_skip, ssm_norm_w, hg_lb_logits, hg_norm_w, w_br_att, w_br_ssm, w_br_hg, w_out, ln2_g, ln2_b, ffn2_wg, ffn2_wu, ffn2_wd, ln3_g, ln3_b):
    bp, seq, d = x_prompt.shape
    ns = x_sample.shape[0]
    depth = w_in.shape[0]
    assert seq % CHUNK == 0 and x_sample.shape[1] == 1
    qw = ATT_HEADS * HEAD_DIM
    kvw = ATT_KV_HEADS * HEAD_DIM
    inner = M_HEADS * M_HEADDIM
    conv_dim = inner + 2 * M_GROUPS * M_STATE
    width = HG_HEADS * LANES
    hpg = M_HEADS // M_GROUPS

    bf = lambda a: a.astype(BF16)
    row3 = lambda a: a.reshape(depth, 1, a.shape[-1])
    o_z = qw + 2 * kvw
    o_dt = o_z + inner + conv_dim
    o_h = o_dt + M_HEADS
    o_g = o_h + 4 * width
    w_in_t = jnp.swapaxes(w_in, 1, 2)
    dt_pad = jnp.pad(w_in_t[:, o_dt:o_h, :], ((0, 0), (0, LANES - M_HEADS), (0, 0)))
    w_rows = bf(jnp.concatenate([w_in_t[:, :o_dt, :], dt_pad, w_in_t[:, o_h:o_g, :]], axis=1))
    b_flat = jnp.concatenate([b_in[:, :o_dt], jnp.pad(b_in[:, o_dt:o_h], ((0, 0), (0, LANES - M_HEADS))),
                              b_in[:, o_h:o_g]], axis=1)
    b_rows = row3(b_flat)
    col_rows = (qw, o_dt + LANES)
    b_cols = b_flat[:, col_rows[0]:col_rows[1], None]
    w_gate, b_gate = bf(w_in_t[:, o_g:, :]), row3(b_in[:, o_g:])
    in_widths = (qw + 2 * kvw, inner + conv_dim + LANES, 4 * width)
    ffn1 = (bf(ffn1_wg), bf(ffn1_wu), bf(ffn1_wd), row3(ln1_g), row3(ln1_b))
    ffn2 = (bf(ffn2_wg), bf(ffn2_wu), bf(ffn2_wd), row3(ln3_g), row3(ln3_b))
    merge_w = (w_gate, b_gate, bf(w_br_att), bf(w_br_ssm), bf(w_br_hg), bf(w_out), row3(ln2_g), row3(ln2_b))

    a_neg = -jnp.exp(a_log.astype(F32))
    pad_h = lambda a: row3(jnp.pad(a, ((0, 0), (0, LANES - M_HEADS))))
    ssd_rows = (conv_w, row3(conv_b), pad_h(dt_bias), pad_h(a_neg), row3(jnp.repeat(d_skip, M_HEADDIM, axis=1)),
                row3(ssm_norm_w))
    lb_all = jnp.cumsum(jax.nn.softmax(hg_lb_logits.astype(F32), axis=0), axis=0)
    lb_all = lb_all - lb_all[0]
    lb_rows = jnp.stack([jnp.log(lb_all), jnp.log1p(-lb_all), 1.0 - lb_all], axis=1)
    lb_rows = jnp.pad(lb_rows, ((0, 0), (0, SUBLANES - 3), (0, 0)))
    hnw = row3(jnp.tile(hg_norm_w, (1, HG_HEADS)))
    sums = jnp.asarray(_hgrn_sum_matrix(CHUNK), BF16)
    sign_np, level_np = _hgrn_level_tables(CHUNK)
    hg_sign, hg_level = jnp.asarray(sign_np), jnp.asarray(level_np)
    att_bias = jnp.asarray(_attn_bias(CHUNK))
    tril = jnp.asarray(np.tril(np.ones((CHUNK, CHUNK), np.float32)), BF16)
    sink8 = jnp.broadcast_to(att_sinks[:, :, None], (depth, ATT_HEADS, LANES))

    tm_p = _row_tile(bp * seq)
    tm_s = _row_tile(ns)
    nb_s = SUBLANES if ns % SUBLANES == 0 else 1
    lanes_b = lambda a: jnp.broadcast_to(a[..., None], a.shape + (ns,))
    kt_in = cache_swa_k.transpose(0, 1, 3, 4, 2).reshape(depth, ns, kvw, WINDOW)
    vt_in = cache_swa_v.transpose(0, 1, 3, 4, 2).reshape(depth, ns, kvw, WINDOW)
    ssm_in = state_ssm.transpose(0, 2, 3, 4, 1).reshape(depth, M_HEADS, M_HEADDIM * M_STATE, ns)
    conv_in = state_conv.transpose(0, 2, 3, 1)
    hg_in = state_hgrn.reshape(depth, ns, width, LANES)
    kt_out, vt_out = jnp.zeros(kt_in.shape, F32), jnp.zeros(vt_in.shape, F32)
    ssm_out, hg_out = jnp.zeros(ssm_in.shape, F32), jnp.zeros(hg_in.shape, F32)
    cw_b, cb_b = lanes_b(conv_w), lanes_b(conv_b)
    dtb_b, a_b, nw_b = lanes_b(dt_bias), lanes_b(a_neg), lanes_b(ssm_norm_w)
    dskip_b = jnp.broadcast_to(d_skip[:, :, None, None], (depth, M_HEADS, SUBLANES, ns))
    att_sub = max(k for k in (4, 2, 1) if (seq // CHUNK) % k == 0)
    scan_sub = max(k for k in (2, 1) if (seq // CHUNK) % k == 0)

    xp = x_prompt.reshape(bp * seq, d)
    xs = x_sample.reshape(ns, d)
    p_states, s_conv = [], []
    for l in range(depth):
        xp = _ffn_ln(xp, *ffn1, l, tm_p)
        qkv, zxd, xh = _inproj(xp, w_rows, b_rows, l, tm_p, in_widths)
        ya = _attn_prompt(qkv, att_sinks, att_bias, l, bp, seq, att_sub)
        ym, st_ssm = _ssd_prompt(zxd, *ssd_rows, tril, l, bp, seq, scan_sub)
        yh, st_hg = _hgrn_prompt(xh, sums, hg_sign, hg_level, lb_rows, hnw, l, bp, seq, scan_sub)
        xp = _merge(xp, ya, ym, yh, *merge_w, l, tm_p)
        xp = _ffn_ln(xp, *ffn2, l, tm_p)
        qkv3 = qkv.reshape(bp, seq, qw + 2 * kvw)
        p_k = qkv3[:, seq - WINDOW:, qw:qw + kvw].reshape(bp, WINDOW, ATT_KV_HEADS, HEAD_DIM)
        p_v = qkv3[:, seq - WINDOW:, qw + kvw:].reshape(bp, WINDOW, ATT_KV_HEADS, HEAD_DIM)
        p_conv = zxd.reshape(bp, seq, -1)[:, seq - (CONV_W - 1):, inner:inner + conv_dim]
        st6 = st_ssm.reshape(bp, M_GROUPS, M_STATE, M_GROUPS, hpg, M_HEADDIM)
        p_ssm = jnp.stack([st6[:, g, :, g] for g in range(M_GROUPS)], axis=1)
        p_ssm = p_ssm.transpose(0, 1, 3, 4, 2).reshape(bp, M_HEADS, M_HEADDIM, M_STATE)
        p_hg = st_hg.reshape(bp, HG_HEADS, LANES, LANES)
        p_states.append((p_k, p_v, p_conv, p_ssm, p_hg))

        xs = _ffn_ln(xs, *ffn1, l, tm_s)
        qkv_s, zx_s, xh_s, col_s = _inproj_sample(xs, w_rows, b_rows, b_cols, l, in_widths, col_rows)
        q4 = qkv_s[:, :qw].reshape(ns, ATT_KV_HEADS, ATT_HEADS // ATT_KV_HEADS, HEAD_DIM)
        zq = jnp.zeros_like(q4[:, 0])
        q8 = jnp.concatenate([jnp.concatenate([q4[:, 0], zq], axis=-1),
                              jnp.concatenate([zq, q4[:, 1]], axis=-1)], axis=1)
        kv_cols = col_s[:2 * kvw].reshape(2 * kvw, ns // nb_s, nb_s).transpose(1, 0, 2)
        oa, kt_out, vt_out, yh_s, hg_out = _sample_attn_hgrn(
            q8, qkv_s[:, None, qw:qw + kvw], qkv_s[:, None, qw + kvw:], kv_cols, kt_in, vt_in, sink8,
            xh_s[:, None, :], hg_in, lb_rows, hnw, kt_out, vt_out, hg_out, l, nb_s)
        ym_s, ssm_out = _ssd_sample(col_s, conv_in, ssm_in, cw_b, cb_b, dtb_b, a_b, dskip_b, nw_b, ssm_out, l)
        grp = ATT_HEADS // ATT_KV_HEADS
        ya_s = jnp.concatenate([oa[:, :grp, :HEAD_DIM].reshape(ns, grp * HEAD_DIM),
                                oa[:, grp:, HEAD_DIM:].reshape(ns, grp * HEAD_DIM)], axis=1)
        xs = _merge(xs, ya_s, ym_s, yh_s.reshape(ns, width), *merge_w, l, tm_s)
        xs = _ffn_ln(xs, *ffn2, l, tm_s)
        s_conv.append(jnp.concatenate([state_conv[l][:, 1:], zx_s[:, None, inner:inner + conv_dim]], axis=1))

    outs_p = [jnp.stack(t) for t in zip(*p_states)]
    unwind = lambda t: t.reshape(depth, ns, ATT_KV_HEADS, HEAD_DIM, WINDOW).transpose(0, 1, 4, 2, 3)
    s_ssm = ssm_out.reshape(depth, M_HEADS, M_HEADDIM, M_STATE, ns).transpose(0, 4, 1, 2, 3)
    outs_s = [unwind(kt_out), unwind(vt_out), jnp.stack(s_conv), s_ssm,
              hg_out.reshape(depth, ns, HG_HEADS, LANES, LANES)]
    return (xp.reshape(bp, seq, d), xs.reshape(ns, 1, d), *outs_p, *outs_s)
```

```python
import functools

import numpy as np
import jax
import jax.numpy as jnp
from jax import lax
from jax.experimental import pallas as pl
from jax.experimental.pallas import tpu as pltpu

F32 = jnp.float32
BF16 = jnp.bfloat16

ATT_HEADS = 8
ATT_KV_HEADS = 2
HEAD_DIM = 64
WINDOW = 128
PAST_LEN = 8192
M_HEADS = 8
M_HEADDIM = 64
M_GROUPS = 2
M_STATE = 64
CONV_W = 4
HG_HEADS = 4
DEPTH = 4
ALPHA = (2.0 * DEPTH) ** 0.25
LN_EPS = 1e-5
RMS_EPS = 1e-6
LOG2_E = 1.4426950408889634
CHUNK = 128

V7X_VMEM_BYTES = 64 * 1024 * 1024
LANES = 128
SUBLANES = 8
MXU_WIDTH = 256

NT_DIMS = (((1,), (1,)), ((), ()))
TN_DIMS = (((0,), (0,)), ((), ()))


def _vmem_limit(nbytes):
    return int(min(V7X_VMEM_BYTES - 8 * 1024 * 1024, nbytes + 16 * 1024 * 1024))


def _params(semantics, vmem_bytes):
    return pltpu.CompilerParams(dimension_semantics=semantics, vmem_limit_bytes=_vmem_limit(vmem_bytes))


def _layer_block(a, layer):
    zeros = (0,) * (a.ndim - 1)
    return pl.BlockSpec((None,) + a.shape[1:], lambda *_: (layer,) + zeros)


def _resident(block_shape, index_map):
    return pl.BlockSpec(block_shape, index_map, pipeline_mode=pl.Buffered(1))


def _silu(x):
    return x * jax.nn.sigmoid(x)


def _softplus(x):
    return jnp.maximum(x, 0.0) + jnp.log(1.0 + jnp.exp(-jnp.abs(x)))


def _layernorm(y, g, b):
    mu = jnp.mean(y, axis=-1, keepdims=True)
    d = y - mu
    var = jnp.mean(d * d, axis=-1, keepdims=True)
    return d * lax.rsqrt(var + LN_EPS) * g + b


def _rms(y):
    return y * lax.rsqrt(jnp.mean(y * y, axis=-1, keepdims=True) + RMS_EPS)


def _split_bf16(x):
    hi = x.astype(BF16)
    lo = (x - hi.astype(F32)).astype(BF16)
    return hi, lo


def _col_chunks(width, step):
    return [(c, min(step, width - c)) for c in range(0, width, step)]


def _ffn_ln_body(x_ref, wg_ref, wu_ref, wd_ref, g_ref, b_ref, o_ref, *, ff_chunk):
    x = x_ref[...]
    xb = x.astype(BF16)
    acc = None
    for c0, cw in _col_chunks(wg_ref.shape[1], ff_chunk):
        gate = jnp.dot(xb, wg_ref[:, c0:c0 + cw], preferred_element_type=F32)
        up = jnp.dot(xb, wu_ref[:, c0:c0 + cw], preferred_element_type=F32)
        hid = (_silu(gate) * up).astype(BF16)
        part = jnp.dot(hid, wd_ref[c0:c0 + cw, :], preferred_element_type=F32)
        acc = part if acc is None else acc + part
    o_ref[...] = _layernorm(ALPHA * x + 0.5 * acc, g_ref[...], b_ref[...])


def _ffn_ln(x, wg, wu, wd, g, b, layer, tm):
    rows, d = x.shape
    f = wg.shape[2]
    wspec = lambda shape: _resident((None,) + shape, lambda i: (layer, 0, 0))
    vmem = 3 * d * f * 2 + 4 * tm * d * 4 + 4 * tm * MXU_WIDTH * 4
    return pl.pallas_call(
        functools.partial(_ffn_ln_body, ff_chunk=MXU_WIDTH),
        out_shape=jax.ShapeDtypeStruct((rows, d), F32),
        grid=(rows // tm,),
        in_specs=[pl.BlockSpec((tm, d), lambda i: (i, 0)),
                  wspec((d, f)), wspec((d, f)), wspec((f, d)),
                  wspec((1, d)), wspec((1, d))],
        out_specs=pl.BlockSpec((tm, d), lambda i: (i, 0)),
        compiler_params=_params(("parallel",), vmem),
        name="ffn_ln",
    )(x, wg, wu, wd, g, b)


def _hgrn_decay(hf, log_lb, log1m_lb, one_m_lb):
    e_neg = jnp.exp(-jnp.abs(hf))
    one_p = 1.0 + e_neg
    log_sig = jnp.minimum(hf, 0.0) - jnp.log(one_p)
    inv = 1.0 / one_p
    kk = one_m_lb * jnp.where(hf >= 0.0, e_neg * inv, inv)
    b = log1m_lb + log_sig
    logf = jnp.maximum(log_lb, b) + jnp.log(1.0 + jnp.exp(-jnp.abs(log_lb - b)))
    return logf, kk


def _project_rows(xb, wt_ref, b_ref, o_refs):
    base = 0
    for o_ref in o_refs:
        for c0, cw in _col_chunks(o_ref.shape[1], 2 * MXU_WIDTH):
            w = wt_ref[base + c0:base + c0 + cw, :]
            o_ref[:, c0:c0 + cw] = (lax.dot_general(xb, w, NT_DIMS, preferred_element_type=F32)
                                    + b_ref[:, base + c0:base + c0 + cw])
        base += o_ref.shape[1]


def _inproj_body(x_ref, wt_ref, b_ref, *o_refs):
    _project_rows(x_ref[...].astype(BF16), wt_ref, b_ref, o_refs)


def _inproj(x, wt, b, layer, tm, widths):
    rows, d = x.shape
    n = wt.shape[1]
    assert n == sum(widths)
    vmem = d * n * 2 + 2 * tm * d * 4 + 2 * tm * n * 4
    return pl.pallas_call(
        _inproj_body,
        out_shape=[jax.ShapeDtypeStruct((rows, wd), F32) for wd in widths],
        grid=(rows // tm,),
        in_specs=[pl.BlockSpec((tm, d), lambda i: (i, 0)),
                  _resident((None, n, d), lambda i: (layer, 0, 0)),
                  _resident((None, 1, n), lambda i: (layer, 0, 0))],
        out_specs=[pl.BlockSpec((tm, wd), lambda i: (i, 0)) for wd in widths],
        compiler_params=_params(("parallel",), vmem),
        name="inproj",
    )(x, wt, b)


def _inproj_sample_body(x_ref, wt_ref, b_ref, bt_ref, *o_refs, col_rows):
    xb = x_ref[...].astype(BF16)
    _project_rows(xb, wt_ref, b_ref, o_refs[:-1])
    r0, r1 = col_rows
    o_refs[-1][...] = lax.dot_general(wt_ref[r0:r1, :], xb, NT_DIMS, preferred_element_type=F32) + bt_ref[...]


def _inproj_sample(x, wt, b, bt, layer, widths, col_rows):
    rows, d = x.shape
    n, nt = wt.shape[1], col_rows[1] - col_rows[0]
    vmem = d * n * 2 + 2 * rows * d * 4 + 2 * rows * (n + nt) * 4
    return pl.pallas_call(
        functools.partial(_inproj_sample_body, col_rows=col_rows),
        out_shape=[jax.ShapeDtypeStruct((rows, wd), F32) for wd in widths]
        + [jax.ShapeDtypeStruct((nt, rows), F32)],
        grid=(1,),
        in_specs=[pl.BlockSpec((rows, d), lambda i: (0, 0)),
                  pl.BlockSpec((None, n, d), lambda i: (layer, 0, 0)),
                  pl.BlockSpec((None, 1, n), lambda i: (layer, 0, 0)),
                  pl.BlockSpec((None, nt, 1), lambda i: (layer, 0, 0))],
        out_specs=[pl.BlockSpec((rows, wd), lambda i: (0, 0)) for wd in widths]
        + [pl.BlockSpec((nt, rows), lambda i: (0, 0))],
        compiler_params=_params(("arbitrary",), vmem),
        name="inproj_sample",
    )(x, wt, b, bt)


def _merge_body(x_ref, ya_ref, ym_ref, yh_ref, wgate_ref, bgate_ref, wa_ref, ws_ref, wh_ref,
                wo_ref, g_ref, b_ref, o_ref):
    x = x_ref[...]
    xb = x.astype(BF16)
    d = x.shape[1]
    branches = ((ya_ref[...].astype(BF16), wa_ref), (ym_ref[...].astype(BF16), ws_ref),
                (yh_ref[...].astype(BF16), wh_ref))
    merged = []
    for c0, cw in _col_chunks(d, MXU_WIDTH):
        m = None
        for k, (yb, w_ref) in enumerate(branches):
            col = k * d + c0
            gate = jax.nn.sigmoid(lax.dot_general(xb, wgate_ref[col:col + cw, :], NT_DIMS,
                                                  preferred_element_type=F32) + bgate_ref[:, col:col + cw])
            term = gate * jnp.dot(yb, w_ref[:, c0:c0 + cw], preferred_element_type=F32)
            m = term if m is None else m + term
        merged.append(m.astype(BF16))
    y = jnp.dot(jnp.concatenate(merged, axis=1), wo_ref[...], preferred_element_type=F32)
    o_ref[...] = _layernorm(ALPHA * x + y, g_ref[...], b_ref[...])


def _merge(x, ya, ym, yh, wgate, bgate, wa, ws, wh, wo, g, b, layer, tm):
    rows, d = x.shape
    wb = ya.shape[1]
    wspec = lambda shape: _resident((None,) + shape, lambda i: (layer, 0, 0))
    row = lambda width: pl.BlockSpec((tm, width), lambda i: (i, 0))
    vmem = (3 * d * d + 3 * wb * d + d * d) * 2 + 4 * tm * d * 4 + 6 * tm * wb * 4
    return pl.pallas_call(
        _merge_body,
        out_shape=jax.ShapeDtypeStruct((rows, d), F32),
        grid=(rows // tm,),
        in_specs=[row(d), row(wb), row(wb), row(wb),
                  wspec((3 * d, d)), wspec((1, 3 * d)),
                  wspec((wb, d)), wspec((wb, d)), wspec((wb, d)), wspec((d, d)),
                  wspec((1, d)), wspec((1, d))],
        out_specs=row(d),
        compiler_params=_params(("parallel",), vmem),
        name="merge_out_ln",
    )(x, ya, ym, yh, wgate, bgate, wa, ws, wh, wo, g, b)


def _alibi_slope_col(head_rows):
    slope = jnp.zeros(head_rows.shape, F32)
    for h in range(ATT_HEADS):
        slope = jnp.where(head_rows == h, 2.0 ** (-8.0 * (h + 1) / ATT_HEADS), slope)
    return slope


def _attn_bias(blk):
    tq = np.arange(blk)[:, None]
    j = np.arange(2 * blk)[None, :]
    dist = WINDOW + tq - j
    ok = (dist >= 0) & (dist <= WINDOW)
    slopes = 2.0 ** (-8.0 * np.arange(1, ATT_HEADS + 1) / ATT_HEADS)
    out = np.empty((2, ATT_HEADS, blk, 2 * blk), np.float32)
    for has_prev in range(2):
        vis = ok & ((j >= blk) | (has_prev == 1))
        out[has_prev] = np.where(vis[None], -slopes[:, None, None] * dist[None], -np.inf)
    return out


def _attn_prompt_body(sink_ref, bias_ref, q_ref, kvc_ref, kvp_ref, o_ref, *, layer):
    blk = kvp_ref.shape[0]
    nsub = q_ref.shape[0] // blk
    grp = ATT_HEADS // ATT_KV_HEADS
    has_prev = jnp.minimum(pl.program_id(1), 1)
    lane = lax.broadcasted_iota(jnp.int32, (blk, LANES), 1)
    low = lane < HEAD_DIM

    def kv_block(i):
        return kvp_ref[...] if i < 0 else kvc_ref[i * blk:(i + 1) * blk, :]

    kcats, vcats = [], []
    for i in range(nsub):
        prev, cur = kv_block(i - 1), kv_block(i)
        kcats.append(jnp.concatenate([prev[:, :LANES], cur[:, :LANES]], axis=0).astype(BF16))
        vcats.append(jnp.concatenate([prev[:, LANES:], cur[:, LANES:]], axis=0).astype(BF16))

    def scores(i, h):
        kv = h // grp
        qcol = q_ref[i * blk:(i + 1) * blk, (h // 2) * LANES:(h // 2 + 1) * LANES] * (HEAD_DIM ** -0.5)
        if (h % 2) != kv:
            qcol = pltpu.roll(qcol, HEAD_DIM, 1)
        qh = jnp.where(low if kv == 0 else jnp.logical_not(low), qcol, 0.0).astype(BF16)
        bias = bias_ref[has_prev, h] if i == 0 else bias_ref[1, h]
        return lax.dot_general(qh, kcats[i], NT_DIMS, preferred_element_type=F32) + bias

    ih = [(i, h) for i in range(nsub) for h in range(ATT_HEADS)]
    ss = {k: scores(*k) for k in ih}
    ms = {k: jnp.maximum(jnp.max(ss[k], axis=1, keepdims=True), sink_ref[layer, k[1]]) for k in ih}
    ps = {k: jnp.exp(ss[k] - ms[k]) for k in ih}
    dens = {k: jnp.sum(ps[k], axis=1, keepdims=True) + jnp.exp(sink_ref[layer, k[1]] - ms[k]) for k in ih}
    ps = {k: (ps[k] * (1.0 / dens[k])).astype(BF16) for k in ih}
    outs = {k: jnp.dot(ps[k], vcats[k[0]], preferred_element_type=F32) for k in ih}
    for i in range(nsub):
        for c in range(ATT_HEADS // 2):
            even, odd = outs[i, 2 * c], outs[i, 2 * c + 1]
            if (2 * c) // grp == 0:
                col = jnp.where(low, even, pltpu.roll(odd, HEAD_DIM, 1))
            else:
                col = jnp.where(low, pltpu.roll(even, HEAD_DIM, 1), odd)
            o_ref[i * blk:(i + 1) * blk, c * LANES:(c + 1) * LANES] = col.astype(o_ref.dtype)


def _attn_prompt(qkv, sinks, bias, layer, batch, seq, nsub):
    rows = nsub * CHUNK
    ns = seq // rows
    qw = ATT_HEADS * HEAD_DIM
    kvw = 2 * ATT_KV_HEADS * HEAD_DIM
    kv_col = qw // kvw
    return pl.pallas_call(
        functools.partial(_attn_prompt_body, layer=layer),
        out_shape=jax.ShapeDtypeStruct((batch * seq, qw), BF16),
        grid=(batch, ns),
        in_specs=[pl.BlockSpec(memory_space=pltpu.SMEM),
                  pl.BlockSpec(bias.shape, lambda b, n: (0,) * bias.ndim),
                  pl.BlockSpec((rows, qw), lambda b, n: (b * ns + n, 0)),
                  pl.BlockSpec((rows, kvw), lambda b, n: (b * ns + n, kv_col)),
                  pl.BlockSpec((CHUNK, kvw), lambda b, n: ((b * ns + n) * nsub - jnp.minimum(n, 1), kv_col))],
        out_specs=pl.BlockSpec((rows, qw), lambda b, n: (b * ns + n, 0)),
        compiler_params=_params(("parallel", "arbitrary"), 16 * 1024 * 1024),
        name="attn_prompt",
    )(sinks, bias, qkv, qkv, qkv)


def _expand_heads(col_of, lane_low):
    cols = []
    for j in range(M_HEADS // 2):
        cols.append(jnp.where(lane_low, col_of(2 * j), col_of(2 * j + 1)))
    return jnp.concatenate(cols, axis=1)


def _ssd_prompt_body(zxd_ref, cw_ref, cb_ref, dtb_ref, a_ref, dskip_ref, nw_ref, tril_ref,
                     y_ref, st_ref, xpad_ref):
    c = pl.program_id(1)
    L = CHUNK
    inner = M_HEADS * M_HEADDIM
    conv_dim = inner + 2 * M_GROUPS * M_STATE

    @pl.when(c == 0)
    def _():
        st_ref[...] = jnp.zeros_like(st_ref)
        xpad_ref[0:SUBLANES, :] = jnp.zeros((SUBLANES, conv_dim), F32)

    lane = lax.broadcasted_iota(jnp.int32, (L, LANES), 1)
    low = lane < M_STATE
    low1 = lane[0:1, :] < M_STATE
    causal = lax.broadcasted_iota(jnp.int32, (L, L), 0) >= lax.broadcasted_iota(jnp.int32, (L, L), 1)
    srow = lax.broadcasted_iota(jnp.int32, (M_GROUPS * M_STATE, inner), 0) // M_STATE
    scol = lax.broadcasted_iota(jnp.int32, (M_GROUPS * M_STATE, inner), 1) // (inner // M_GROUPS)
    hpg = M_HEADS // M_GROUPS

    subs = range(zxd_ref.shape[0] // L)
    heads = range(M_HEADS)
    xs, bm, cm, dt, a = {}, {}, {}, {}, {}
    for r in subs:
        rs = slice(r * L, (r + 1) * L)
        xbc = zxd_ref[rs, inner:inner + conv_dim]
        xpad_ref[SUBLANES:SUBLANES + L, :] = xbc
        conv = xbc * cw_ref[CONV_W - 1:CONV_W, :] + cb_ref[...]
        for w in range(CONV_W - 1):
            off = SUBLANES - (CONV_W - 1) + w
            conv = conv + xpad_ref[off:off + L, :] * cw_ref[w:w + 1, :]
        xpad_ref[0:SUBLANES, :] = xbc[L - SUBLANES:L, :]
        act = _silu(conv)
        xs[r] = act[:, :inner]
        bm[r] = act[:, inner:inner + LANES]
        cm[r] = act[:, inner + LANES:inner + 2 * LANES]
        dt[r] = _softplus(zxd_ref[rs, inner + conv_dim:] + dtb_ref[...])
        dta_hi, dta_lo = _split_bf16(dt[r] * a_ref[...])
        cum = jnp.dot(tril_ref[...], jnp.concatenate([dta_hi, dta_lo], axis=1), preferred_element_type=F32)
        a[r] = cum[:, :LANES] + cum[:, LANES:]
    a_t = {r: a[r].T for r in subs}
    dt_t = {r: dt[r].T for r in subs}
    bmb = {r: bm[r].astype(BF16) for r in subs}
    cb_g = {(r, g): lax.dot_general(jnp.where(low if g == 0 else jnp.logical_not(low), cm[r], 0.0).astype(BF16),
                                    bmb[r], NT_DIMS, preferred_element_type=F32)
            for r in subs for g in range(M_GROUPS)}

    def weights(r, h):
        diff = jnp.broadcast_to(a[r][:, h:h + 1], (L, L)) - jnp.broadcast_to(a_t[r][h:h + 1, :], (L, L))
        decay = jnp.exp(jnp.where(causal, diff, -jnp.inf))
        return (cb_g[r, h // hpg] * decay * jnp.broadcast_to(dt_t[r][h:h + 1, :], (L, L))).astype(BF16)

    ws = {(r, h): weights(r, h) for r in subs for h in heads}
    xhalf = {(r, h): jnp.where(low if h % 2 == 0 else jnp.logical_not(low),
                               xs[r][:, (h // 2) * LANES:(h // 2 + 1) * LANES], 0.0).astype(BF16)
             for r in subs for h in heads}
    yh = {k: jnp.dot(ws[k], xhalf[k], preferred_element_type=F32) for k in ws}
    y_intra = {r: jnp.concatenate([yh[r, 2 * j] + yh[r, 2 * j + 1] for j in range(M_HEADS // 2)], axis=1) for r in subs}
    ea = {r: jnp.exp(a[r]) for r in subs}
    ea_x = {r: _expand_heads(lambda h: jnp.broadcast_to(ea[r][:, h:h + 1], (L, LANES)), low) for r in subs}
    coef = {r: jnp.exp(a[r][L - 1:L, :] - a[r]) * dt[r] for r in subs}
    coef_x = {r: _expand_heads(lambda h: jnp.broadcast_to(coef[r][:, h:h + 1], (L, LANES)), low) for r in subs}
    cs = {r: jnp.dot(bm[r].T.astype(BF16), (xs[r] * coef_x[r]).astype(BF16), preferred_element_type=F32)
          for r in subs}
    dec_x = {}
    for r in subs:
        ea_last = jnp.exp(a[r][L - 1:L, :])
        dec_x[r] = _expand_heads(lambda h: jnp.broadcast_to(ea_last[:, h:h + 1], (1, LANES)), low1)
    gate = {r: _silu(zxd_ref[r * L:(r + 1) * L, 0:inner]) for r in subs}
    for r in subs:
        st = st_ref[...]
        y = y_intra[r] + jnp.dot(cm[r].astype(BF16), st.astype(BF16), preferred_element_type=F32) * ea_x[r]
        st_ref[...] = st * dec_x[r] + jnp.where(srow == scol, cs[r], 0.0)
        y = (y + dskip_ref[...] * xs[r]) * gate[r]
        y_ref[r * L:(r + 1) * L, :] = (_rms(y) * nw_ref[...]).astype(y_ref.dtype)


def _ssd_prompt(zxd, conv_w, conv_b, dtb, a_neg, dskip, norm_w, tril, layer, batch, seq, nsub):
    nc = seq // (nsub * CHUNK)
    width = zxd.shape[1]
    inner = M_HEADS * M_HEADDIM
    conv_dim = conv_w.shape[2]
    consts = (conv_w, conv_b, dtb, a_neg, dskip, norm_w)
    return pl.pallas_call(
        _ssd_prompt_body,
        out_shape=[jax.ShapeDtypeStruct((batch * seq, inner), BF16),
                   jax.ShapeDtypeStruct((batch, M_GROUPS * M_STATE, inner), F32)],
        grid=(batch, nc),
        in_specs=[pl.BlockSpec((nsub * CHUNK, width), lambda b, c: (b * nc + c, 0))]
        + [_layer_block(a, layer) for a in consts] + [pl.BlockSpec(tril.shape, lambda b, c: (0, 0))],
        out_specs=[pl.BlockSpec((nsub * CHUNK, inner), lambda b, c: (b * nc + c, 0)),
                   pl.BlockSpec((None, M_GROUPS * M_STATE, inner), lambda b, c: (b, 0, 0))],
        scratch_shapes=[pltpu.VMEM((SUBLANES + CHUNK, conv_dim), F32)],
        compiler_params=_params(("arbitrary", "arbitrary"), 16 * 1024 * 1024),
        name="ssd_prompt",
    )(zxd, *consts, tril)


HG_LEVELS = tuple(2 ** i for i in range(7))
HG_FINE = tuple(b for b in HG_LEVELS if b < SUBLANES)


def _hgrn_sum_matrix(L):
    r = np.arange(L)
    tt, rr = np.meshgrid(r, r, indexing="ij")
    mats = []
    for b in HG_FINE:
        mid = (r // (2 * b)) * 2 * b + b - 1
        upper = (r // b) % 2 == 1
        up = (mid[:, None] < rr) & (rr <= tt)
        lo = (tt < rr) & (rr <= mid[:, None])
        mats.append(np.where(upper[:, None], up, lo))
    mats.append(rr <= tt)
    m = np.concatenate(mats, axis=0).astype(np.float32)
    return np.concatenate([m, m], axis=1)


def _hgrn_level_tables(L):
    r = np.arange(L)
    later = np.concatenate([np.repeat((((r // b) % 2) == 1)[:, None], LANES, axis=1) for b in HG_LEVELS], axis=0)
    level = np.full((L, L), -1, np.int32)
    for lvl, b in enumerate(HG_LEVELS):
        t_later = ((r // b) % 2 == 1)[:, None]
        s_earlier = ((r // b) % 2 == 0)[None, :]
        same_parent = (r // (2 * b))[:, None] == (r // (2 * b))[None, :]
        level[t_later & s_earlier & same_parent] = lvl
    level[r, r] = len(HG_LEVELS)
    return np.where(later, 1.0, -1.0).astype(np.float32), level


def _hgrn_prompt_body(x_ref, sums_ref, sign_ref, level_ref, lb_ref, nw_ref, y_ref, st_ref):
    c = pl.program_id(1)
    L = CHUNK
    width = HG_HEADS * LANES
    nlev, nfine = len(HG_LEVELS), len(HG_FINE)

    @pl.when(c == 0)
    def _():
        st_ref[...] = jnp.zeros_like(st_ref)

    level = level_ref[...]
    nsub = x_ref.shape[0] // L
    heads = range(HG_HEADS)
    ch = [(r, h) for r in range(nsub) for h in heads]
    part = lambda j, r, h: x_ref[r * L:(r + 1) * L, j * width + h * LANES:j * width + (h + 1) * LANES]
    g2, kks = {}, {}
    for r, h in ch:
        sl = slice(h * LANES, (h + 1) * LANES)
        logf, kks[r, h] = _hgrn_decay(part(1, r, h), lb_ref[0:1, sl], lb_ref[1:2, sl], lb_ref[2:3, sl])
        g2[r, h] = logf * LOG2_E
    es = {}
    for r in range(nsub):
        for pair in range(HG_HEADS // 2):
            parts = [_split_bf16(g2[r, 2 * pair + k]) for k in range(2)]
            w = jnp.concatenate([jnp.concatenate([parts[0][0], parts[1][0]], axis=1),
                                 jnp.concatenate([parts[0][1], parts[1][1]], axis=1)], axis=0)
            e2 = jnp.dot(sums_ref[...], w, preferred_element_type=F32)
            es[r, 2 * pair], es[r, 2 * pair + 1] = e2[:, :LANES], e2[:, LANES:]
    qs = {k: _silu(part(0, *k)) for k in ch}
    bcs = {k: es[k][nfine * L:(nfine + 1) * L] for k in ch}
    atts = {k: jnp.where(level == nlev, jnp.sum(qs[k] * kks[k], axis=1, keepdims=True), 0.0) for k in ch}
    for lvl, b in enumerate(HG_LEVELS):
        for k in ch:
            q, kk, bc = qs[k], kks[k], bcs[k]
            if lvl < nfine:
                later = sign_ref[lvl * L:(lvl + 1) * L, :] > 0.0
                xk = jnp.exp2(es[k][lvl * L:(lvl + 1) * L]) * jnp.where(later, q, kk)
            else:
                blocks = []
                for p in range(0, L, 2 * b):
                    mid = bc[p + b - 1:p + b, :]
                    blocks.append(kk[p:p + b] * jnp.exp2(mid - bc[p:p + b]))
                    blocks.append(q[p + b:p + 2 * b] * jnp.exp2(bc[p + b:p + 2 * b] - mid))
                xk = jnp.concatenate(blocks, axis=0)
            xk = xk.astype(BF16)
            atts[k] = jnp.where(level == lvl, lax.dot_general(xk, xk, NT_DIMS, preferred_element_type=F32), atts[k])
    qds = {k: (qs[k] * jnp.exp2(bcs[k])).astype(BF16) for k in ch}
    kdts = {k: (kks[k] * jnp.exp2(bcs[k][L - 1:L, :] - bcs[k])).T.astype(BF16) for k in ch}
    dcols = {k: jnp.broadcast_to(jnp.exp2(bcs[k][L - 1:L, :]), (L, LANES)).T for k in ch}
    ivbs = {k: part(2, *k).astype(BF16) for k in ch}
    sogs = {k: _silu(part(3, *k)) for k in ch}
    for r in range(nsub):
        for h in heads:
            sl = slice(h * LANES, (h + 1) * LANES)
            st = st_ref[sl, :]
            o = jnp.dot(jnp.concatenate([atts[r, h].astype(BF16), qds[r, h]], axis=1),
                        jnp.concatenate([ivbs[r, h], st.astype(BF16)], axis=0), preferred_element_type=F32)
            st_ref[sl, :] = st * dcols[r, h] + jnp.dot(kdts[r, h], ivbs[r, h], preferred_element_type=F32)
            y_ref[r * L:(r + 1) * L, sl] = (_rms(o) * nw_ref[:, sl] * sogs[r, h]).astype(y_ref.dtype)


def _hgrn_prompt(xh, sums, sign, level, lb_rows, nw, layer, batch, seq, nsub):
    rows = nsub * CHUNK
    nc = seq // rows
    width = HG_HEADS * LANES
    full = lambda a: pl.BlockSpec(a.shape, lambda b, c: (0,) * a.ndim)
    return pl.pallas_call(
        _hgrn_prompt_body,
        out_shape=[jax.ShapeDtypeStruct((batch * seq, width), BF16),
                   jax.ShapeDtypeStruct((batch, width, LANES), F32)],
        grid=(batch, nc),
        in_specs=[pl.BlockSpec((rows, 4 * width), lambda b, c: (b * nc + c, 0)),
                  full(sums), full(sign), full(level), _layer_block(lb_rows, layer), _layer_block(nw, layer)],
        out_specs=[pl.BlockSpec((rows, width), lambda b, c: (b * nc + c, 0)),
                   pl.BlockSpec((None, width, LANES), lambda b, c: (b, 0, 0))],
        compiler_params=_params(("arbitrary", "arbitrary"), 16 * 1024 * 1024),
        name="hgrn_prompt",
    )(xh, sums, sign, level, lb_rows, nw)


def _hi_lo_rows(x_row, row_idx):
    xb = jnp.broadcast_to(x_row, row_idx.shape)
    hi = xb.astype(BF16).astype(F32)
    return jnp.where(row_idx == 0, hi, jnp.where(row_idx == 1, xb - hi, 0.0)).astype(BF16)


def _sample_attn_hgrn_body(q8_ref, kr_ref, vr_ref, kvc_ref, kt_ref, vt_ref, sink_ref,
                           xh_ref, s0_ref, lb_ref, hnw_ref, kt_stack_ref, vt_stack_ref, s_stack_ref,
                           oa_ref, ktn_ref, vtn_ref, yh_ref, sn_ref, *, n_prev_valid):
    del kt_stack_ref, vt_stack_ref, s_stack_ref
    nb = q8_ref.shape[0]
    width = HG_HEADS * LANES
    grp = ATT_HEADS // ATT_KV_HEADS
    row8 = lax.broadcasted_iota(jnp.int32, (SUBLANES, LANES), 0)
    lane8 = lax.broadcasted_iota(jnp.int32, (SUBLANES, LANES), 1)
    hrow = lax.broadcasted_iota(jnp.int32, (ATT_HEADS, 1), 0)
    slope = _alibi_slope_col(hrow)
    jj = lax.broadcasted_iota(jnp.int32, (ATT_HEADS, WINDOW), 1)
    cache_dist = (WINDOW - jj).astype(F32)
    cache_ok = jj >= WINDOW - n_prev_valid
    kv_half = (lane8 // HEAD_DIM) == (row8 // grp)
    last_lane = lax.broadcasted_iota(jnp.int32, (LANES, WINDOW), 1) == WINDOW - 1
    rowx = lax.broadcasted_iota(jnp.int32, (SUBLANES, width), 0)
    lanex = lax.broadcasted_iota(jnp.int32, (SUBLANES, width), 1)
    ones_rows01 = jnp.where(row8 < 2, 1.0, 0.0).astype(BF16)
    kvw = ATT_KV_HEADS * HEAD_DIM

    smp = range(nb)
    sink = sink_ref[:, 0:1]
    head_of_lane = lanex // LANES

    s_c = [jnp.dot(q8_ref[i].astype(BF16), kt_ref[i].astype(BF16), preferred_element_type=F32) * (HEAD_DIM ** -0.5)
           for i in smp]
    s_c = [jnp.where(cache_ok, s_c[i] - slope * cache_dist, -jnp.inf) for i in smp]
    s_n = [jnp.sum(q8_ref[i] * kr_ref[i], axis=1, keepdims=True) * (HEAD_DIM ** -0.5) for i in smp]
    m = [jnp.maximum(jnp.maximum(jnp.max(s_c[i], axis=1, keepdims=True), s_n[i]), sink) for i in smp]
    p_c = [jnp.exp(s_c[i] - m[i]) for i in smp]
    p_n = [jnp.exp(s_n[i] - m[i]) for i in smp]
    den = [jnp.sum(p_c[i], axis=1, keepdims=True) + p_n[i] + jnp.exp(sink - m[i]) for i in smp]
    o = [lax.dot_general((p_c[i] / den[i]).astype(BF16), vt_ref[i].astype(BF16), NT_DIMS,
                         preferred_element_type=F32) + (p_n[i] / den[i]) * vr_ref[i] for i in smp]
    for i in smp:
        oa_ref[i] = jnp.where(kv_half, o[i], 0.0)
        ktn_ref[i] = jnp.where(last_lane, kvc_ref[0:kvw, i:i + 1], pltpu.roll(kt_ref[i], WINDOW - 1, 1))
        vtn_ref[i] = jnp.where(last_lane, kvc_ref[kvw:2 * kvw, i:i + 1], pltpu.roll(vt_ref[i], WINDOW - 1, 1))

    q = [_silu(xh_ref[i][:, :width]) for i in smp]
    dk = [_hgrn_decay(xh_ref[i][:, width:2 * width], lb_ref[0:1, :], lb_ref[1:2, :], lb_ref[2:3, :]) for i in smp]
    f = [jnp.exp(dk[i][0]) for i in smp]
    kk = [dk[i][1] for i in smp]
    iv = [xh_ref[i][:, 2 * width:3 * width] for i in smp]
    lhs_q = [jnp.where(rowx == head_of_lane, jnp.broadcast_to(q[i] * f[i], (SUBLANES, width)), 0.0).astype(BF16)
             for i in smp]
    qs = [jnp.dot(lhs_q[i], s0_ref[i].astype(BF16), preferred_element_type=F32) for i in smp]
    for i in smp:
        qkk = q[i] * kk[i]
        o_parts = []
        for h in range(HG_HEADS):
            sl = slice(h * LANES, (h + 1) * LANES)
            att = jnp.sum(qkk[:, sl], axis=1, keepdims=True)
            o_parts.append(_rms(att * iv[i][:, sl] + qs[i][h:h + 1, :]))
        yh_ref[i] = jnp.concatenate(o_parts, axis=1) * hnw_ref[...] * _silu(xh_ref[i][:, 3 * width:])
    i8 = [jnp.concatenate([iv[i][:, h * LANES:(h + 1) * LANES] for h in range(HG_HEADS)]
                          + [jnp.zeros((SUBLANES - HG_HEADS, LANES), F32)], axis=0).astype(BF16) for i in smp]
    lhs_k = [jnp.where(rowx == head_of_lane, jnp.broadcast_to(kk[i], (SUBLANES, width)), 0.0).astype(BF16)
             for i in smp]
    upd_s = [lax.dot_general(lhs_k[i], i8[i], TN_DIMS, preferred_element_type=F32) for i in smp]
    f_col = [lax.dot_general(_hi_lo_rows(f[i], rowx), ones_rows01, TN_DIMS, preferred_element_type=F32) for i in smp]
    for i in smp:
        sn_ref[i] = s0_ref[i] * f_col[i] + upd_s[i]


def _sample_attn_hgrn(q8, kr, vr, kvc, kt, vt, sink8, xh, s0, lb_rows, hnw, kt_stack, vt_stack, s_stack, layer, nb):
    n = q8.shape[0]
    per = lambda a: pl.BlockSpec((nb,) + a.shape[1:], lambda i: (i,) + (0,) * (a.ndim - 1))
    lay = lambda a: pl.BlockSpec((None, nb) + a.shape[2:], lambda i: (layer, i) + (0,) * (a.ndim - 2))
    hbm = pl.BlockSpec(memory_space=pl.ANY)
    width = HG_HEADS * LANES
    out_shape = [jax.ShapeDtypeStruct((n, ATT_HEADS, LANES), F32),
                 jax.ShapeDtypeStruct(kt_stack.shape, F32),
                 jax.ShapeDtypeStruct(vt_stack.shape, F32),
                 jax.ShapeDtypeStruct((n, 1, width), F32),
                 jax.ShapeDtypeStruct(s_stack.shape, F32)]
    return pl.pallas_call(
        functools.partial(_sample_attn_hgrn_body, n_prev_valid=min(WINDOW, PAST_LEN)),
        out_shape=out_shape,
        grid=(n // nb,),
        in_specs=[per(q8), per(kr), per(vr), pl.BlockSpec((None,) + kvc.shape[1:], lambda i: (i, 0, 0)),
                  lay(kt), lay(vt), _layer_block(sink8, layer), per(xh), lay(s0),
                  _layer_block(lb_rows, layer), _layer_block(hnw, layer), hbm, hbm, hbm],
        out_specs=[per(q8), lay(kt_stack), lay(vt_stack),
                   pl.BlockSpec((nb, 1, width), lambda i: (i, 0, 0)), lay(s_stack)],
        input_output_aliases={11: 1, 12: 2, 13: 4},
        compiler_params=_params(("parallel",), 4 * nb * (2 * kt.shape[2] + s0.shape[2]) * LANES * 4),
        name="sample_attn_hgrn",
    )(q8, kr, vr, kvc, kt, vt, sink8, xh, s0, lb_rows, hnw, kt_stack, vt_stack, s_stack)


def _ssd_sample_body(col_ref, cst_ref, h0_ref, cw_ref, cb_ref, dtb_ref, a_ref, dskip_ref, nw_ref, stack_ref,
                     ym_ref, hn_ref, xs_s, b_s, c_s, dt_s, da_s, yrow_s, y_s):
    del stack_ref
    h = pl.program_id(0)
    inner = M_HEADS * M_HEADDIM
    conv_dim = inner + 2 * M_GROUPS * M_STATE
    kvw2 = 2 * ATT_KV_HEADS * HEAD_DIM
    z0, x0, d0 = kvw2, kvw2 + inner, kvw2 + inner + conv_dim
    ns = col_ref.shape[1]

    @pl.when(h == 0)
    def _():
        conv = col_ref[x0:d0, :] * cw_ref[CONV_W - 1] + cb_ref[...]
        for w in range(CONV_W - 1):
            conv = conv + cst_ref[w] * cw_ref[w]
        act = _silu(conv)
        xs_s[...] = act[:inner]
        b_s[...] = act[inner:inner + M_GROUPS * M_STATE]
        c_s[...] = act[inner + M_GROUPS * M_STATE:]
        dt = _softplus(col_ref[d0:d0 + M_HEADS, :] + dtb_ref[...])
        da = jnp.exp(dt * a_ref[...])
        for hh in range(M_HEADS):
            dt_s[hh] = jnp.broadcast_to(dt[hh:hh + 1, :], (SUBLANES, ns))
            da_s[hh] = jnp.broadcast_to(da[hh:hh + 1, :], (SUBLANES, ns))

    hpg = M_HEADS // M_GROUPS
    x_h = xs_s[pl.ds(pl.multiple_of(h * M_HEADDIM, M_HEADDIM), M_HEADDIM), :]
    g0 = pl.multiple_of((h // hpg) * M_STATE, M_STATE)
    b_g = b_s[pl.ds(g0, M_STATE), :]
    c_g = c_s[pl.ds(g0, M_STATE), :]
    dt = dt_s[h][0:1, :]
    da = da_s[h][0:1, :]
    dtx = x_h * dt
    for p in range(M_HEADDIM):
        s_p = h0_ref[p * M_STATE:(p + 1) * M_STATE, :]
        yrow_s[p:p + 1, :] = jnp.sum(c_g * s_p, axis=0, keepdims=True)
        hn_ref[p * M_STATE:(p + 1) * M_STATE, :] = da * s_p + b_g * dtx[p:p + 1, :]
    cb = jnp.sum(c_g * b_g, axis=0, keepdims=True)
    y_s[pl.ds(pl.multiple_of(h * M_HEADDIM, M_HEADDIM), M_HEADDIM), :] = (
        da * yrow_s[...] + (dt * cb + dskip_ref[h][0:1, :]) * x_h)

    @pl.when(h == pl.num_programs(0) - 1)
    def _():
        y = y_s[...] * _silu(col_ref[z0:x0, :])
        y = y * lax.rsqrt(jnp.mean(y * y, axis=0, keepdims=True) + RMS_EPS) * nw_ref[...]
        for j in range(inner // LANES):
            ym_ref[:, j * LANES:(j + 1) * LANES] = y[j * LANES:(j + 1) * LANES, :].T


def _ssd_sample(col, cst, h0, cw, cb, dtb, a_neg, dskip, nw, stack, layer):
    ns = col.shape[1]
    inner = M_HEADS * M_HEADDIM
    state_block = pl.BlockSpec((None, None) + h0.shape[2:], lambda h: (layer, h, 0, 0))
    return pl.pallas_call(
        _ssd_sample_body,
        out_shape=[jax.ShapeDtypeStruct((ns, inner), F32), jax.ShapeDtypeStruct(stack.shape, F32)],
        grid=(M_HEADS,),
        in_specs=[pl.BlockSpec(col.shape, lambda h: (0, 0)), _layer_block(cst, layer), state_block]
        + [_layer_block(a, layer) for a in (cw, cb, dtb, a_neg, dskip, nw)] + [pl.BlockSpec(memory_space=pl.ANY)],
        out_specs=[pl.BlockSpec((ns, inner), lambda h: (0, 0)), state_block],
        scratch_shapes=[pltpu.VMEM((inner, ns), F32), pltpu.VMEM((M_GROUPS * M_STATE, ns), F32),
                        pltpu.VMEM((M_GROUPS * M_STATE, ns), F32), pltpu.VMEM((M_HEADS, SUBLANES, ns), F32),
                        pltpu.VMEM((M_HEADS, SUBLANES, ns), F32), pltpu.VMEM((M_HEADDIM, ns), F32),
                        pltpu.VMEM((inner, ns), F32)],
        input_output_aliases={9: 1},
        compiler_params=_params(("arbitrary",), 24 * 1024 * 1024),
        name="ssd_sample",
    )(col, cst, h0, cw, cb, dtb, a_neg, dskip, nw, stack)


def _row_tile(rows):
    for tm in (512, 256, 128):
        if rows % tm == 0:
            return tm
    return rows


def kernel(x_prompt, x_sample, cache_swa_k, cache_swa_v, state_conv, state_ssm, state_hgrn, ln1_g, ln1_b, ffn1_wg, ffn1_wu, ffn1_wd, w_in, b_in, att_sinks, conv_w, conv_b, dt_bias, a_log, d_skip, ssm_norm_w, hg_lb_logits, hg_norm_w, w_br_att, w_br_ssm, w_br_hg, w_out, ln2_g, ln2_b, ffn2_wg, ffn2_wu, ffn2_wd, ln3_g, ln3_b):
    bp, seq, d = x_prompt.shape
    ns = x_sample.shape[0]
    depth = w_in.shape[0]
    assert seq % CHUNK == 0 and x_sample.shape[1] == 1
    qw = ATT_HEADS * HEAD_DIM
    kvw = ATT_KV_HEADS * HEAD_DIM
    inner = M_HEADS * M_HEADDIM
    conv_dim = inner + 2 * M_GROUPS * M_STATE
    width = HG_HEADS * LANES
    hpg = M_HEADS // M_GROUPS

    bf = lambda a: a.astype(BF16)
    row3 = lambda a: a.reshape(depth, 1, a.shape[-1])
    o_z = qw + 2 * kvw
    o_dt = o_z + inner + conv_dim
    o_h = o_dt + M_HEADS
    o_g = o_h + 4 * width
    w_in_t = jnp.swapaxes(w_in, 1, 2)
    dt_pad = jnp.pad(w_in_t[:, o_dt:o_h, :], ((0, 0), (0, LANES - M_HEADS), (0, 0)))
    w_rows = bf(jnp.concatenate([w_in_t[:, :o_dt, :], dt_pad, w_in_t[:, o_h:o_g, :]], axis=1))
    b_flat = jnp.concatenate([b_in[:, :o_dt], jnp.pad(b_in[:, o_dt:o_h], ((0, 0), (0, LANES - M_HEADS))),
                              b_in[:, o_h:o_g]], axis=1)
    b_rows = row3(b_flat)
    col_rows = (qw, o_dt + LANES)
    b_cols = b_flat[:, col_rows[0]:col_rows[1], None]
    w_gate, b_gate = bf(w_in_t[:, o_g:, :]), row3(b_in[:, o_g:])
    in_widths = (qw + 2 * kvw, inner + conv_dim + LANES, 4 * width)
    ffn1 = (bf(ffn1_wg), bf(ffn1_wu), bf(ffn1_wd), row3(ln1_g), row3(ln1_b))
    ffn2 = (bf(ffn2_wg), bf(ffn2_wu), bf(ffn2_wd), row3(ln3_g), row3(ln3_b))
    merge_w = (w_gate, b_gate, bf(w_br_att), bf(w_br_ssm), bf(w_br_hg), bf(w_out), row3(ln2_g), row3(ln2_b))

    a_neg = -jnp.exp(a_log.astype(F32))
    pad_h = lambda a: row3(jnp.pad(a, ((0, 0), (0, LANES - M_HEADS))))
    ssd_rows = (conv_w, row3(conv_b), pad_h(dt_bias), pad_h(a_neg), row3(jnp.repeat(d_skip, M_HEADDIM, axis=1)),
                row3(ssm_norm_w))
    lb_all = jnp.cumsum(jax.nn.softmax(hg_lb_logits.astype(F32), axis=0), axis=0)
    lb_all = lb_all - lb_all[0]
    lb_rows = jnp.stack([jnp.log(lb_all), jnp.log1p(-lb_all), 1.0 - lb_all], axis=1)
    lb_rows = jnp.pad(lb_rows, ((0, 0), (0, SUBLANES - 3), (0, 0)))
    hnw = row3(jnp.tile(hg_norm_w, (1, HG_HEADS)))
    sums = jnp.asarray(_hgrn_sum_matrix(CHUNK), BF16)
    sign_np, level_np = _hgrn_level_tables(CHUNK)
    hg_sign, hg_level = jnp.asarray(sign_np), jnp.asarray(level_np)
    att_bias = jnp.asarray(_attn_bias(CHUNK))
    tril = jnp.asarray(np.tril(np.ones((CHUNK, CHUNK), np.float32)), BF16)
    sink8 = jnp.broadcast_to(att_sinks[:, :, None], (depth, ATT_HEADS, LANES))

    tm_p = _row_tile(bp * seq)
    tm_s = _row_tile(ns)
    nb_s = max(k for k in (2 * SUBLANES, SUBLANES, 1) if ns % k == 0)
    lanes_b = lambda a: jnp.broadcast_to(a[..., None], a.shape + (ns,))
    kt_in = cache_swa_k.transpose(0, 1, 3, 4, 2).reshape(depth, ns, kvw, WINDOW)
    vt_in = cache_swa_v.transpose(0, 1, 3, 4, 2).reshape(depth, ns, kvw, WINDOW)
    ssm_in = state_ssm.transpose(0, 2, 3, 4, 1).reshape(depth, M_HEADS, M_HEADDIM * M_STATE, ns)
    conv_in = state_conv.transpose(0, 2, 3, 1)
    hg_in = state_hgrn.reshape(depth, ns, width, LANES)
    kt_out, vt_out = jnp.zeros(kt_in.shape, F32), jnp.zeros(vt_in.shape, F32)
    ssm_out, hg_out = jnp.zeros(ssm_in.shape, F32), jnp.zeros(hg_in.shape, F32)
    cw_b, cb_b = lanes_b(conv_w), lanes_b(conv_b)
    dtb_b, a_b, nw_b = lanes_b(dt_bias), lanes_b(a_neg), lanes_b(ssm_norm_w)
    dskip_b = jnp.broadcast_to(d_skip[:, :, None, None], (depth, M_HEADS, SUBLANES, ns))
    att_sub = max(k for k in (4, 2, 1) if (seq // CHUNK) % k == 0)
    ssd_sub = max(k for k in (4, 2, 1) if (seq // CHUNK) % k == 0)
    scan_sub = max(k for k in (2, 1) if (seq // CHUNK) % k == 0)

    xp = x_prompt.reshape(bp * seq, d)
    xs = x_sample.reshape(ns, d)
    p_states, s_conv = [], []
    for l in range(depth):
        xp = _ffn_ln(xp, *ffn1, l, tm_p)
        qkv, zxd, xh = _inproj(xp, w_rows, b_rows, l, tm_p, in_widths)
        ya = _attn_prompt(qkv, att_sinks, att_bias, l, bp, seq, att_sub)
        ym, st_ssm = _ssd_prompt(zxd, *ssd_rows, tril, l, bp, seq, ssd_sub)
        yh, st_hg = _hgrn_prompt(xh, sums, hg_sign, hg_level, lb_rows, hnw, l, bp, seq, scan_sub)
        xp = _merge(xp, ya, ym, yh, *merge_w, l, tm_p)
        xp = _ffn_ln(xp, *ffn2, l, tm_p)
        qkv3 = qkv.reshape(bp, seq, qw + 2 * kvw)
        p_k = qkv3[:, seq - WINDOW:, qw:qw + kvw].reshape(bp, WINDOW, ATT_KV_HEADS, HEAD_DIM)
        p_v = qkv3[:, seq - WINDOW:, qw + kvw:].reshape(bp, WINDOW, ATT_KV_HEADS, HEAD_DIM)
        p_conv = zxd.reshape(bp, seq, -1)[:, seq - (CONV_W - 1):, inner:inner + conv_dim]
        st6 = st_ssm.reshape(bp, M_GROUPS, M_STATE, M_GROUPS, hpg, M_HEADDIM)
        p_ssm = jnp.stack([st6[:, g, :, g] for g in range(M_GROUPS)], axis=1)
        p_ssm = p_ssm.transpose(0, 1, 3, 4, 2).reshape(bp, M_HEADS, M_HEADDIM, M_STATE)
        p_hg = st_hg.reshape(bp, HG_HEADS, LANES, LANES)
        p_states.append((p_k, p_v, p_conv, p_ssm, p_hg))

        xs = _ffn_ln(xs, *ffn1, l, tm_s)
        qkv_s, zx_s, xh_s, col_s = _inproj_sample(xs, w_rows, b_rows, b_cols, l, in_widths, col_rows)
        q4 = qkv_s[:, :qw].reshape(ns, ATT_KV_HEADS, ATT_HEADS // ATT_KV_HEADS, HEAD_DIM)
        zq = jnp.zeros_like(q4[:, 0])
        q8 = jnp.concatenate([jnp.concatenate([q4[:, 0], zq], axis=-1),
                              jnp.concatenate([zq, q4[:, 1]], axis=-1)], axis=1)
        kv_cols = col_s[:2 * kvw].reshape(2 * kvw, ns // nb_s, nb_s).transpose(1, 0, 2)
        oa, kt_out, vt_out, yh_s, hg_out = _sample_attn_hgrn(
            q8, qkv_s[:, None, qw:qw + kvw], qkv_s[:, None, qw + kvw:], kv_cols, kt_in, vt_in, sink8,
            xh_s[:, None, :], hg_in, lb_rows, hnw, kt_out, vt_out, hg_out, l, nb_s)
        ym_s, ssm_out = _ssd_sample(col_s, conv_in, ssm_in, cw_b, cb_b, dtb_b, a_b, dskip_b, nw_b, ssm_out, l)
        grp = ATT_HEADS // ATT_KV_HEADS
        ya_s = jnp.concatenate([oa[:, :grp, :HEAD_DIM].reshape(ns, grp * HEAD_DIM),
                                oa[:, grp:, HEAD_DIM:].reshape(ns, grp * HEAD_DIM)], axis=1)
        xs = _merge(xs, ya_s, ym_s, yh_s.reshape(ns, width), *merge_w, l, tm_s)
        xs = _ffn_ln(xs, *ffn2, l, tm_s)
        s_conv.append(jnp.concatenate([state_conv[l][:, 1:], zx_s[:, None, inner:inner + conv_dim]], axis=1))

    outs_p = [jnp.stack(t) for t in zip(*p_states)]
    unwind = lambda t: t.reshape(depth, ns, ATT_KV_HEADS, HEAD_DIM, WINDOW).transpose(0, 1, 4, 2, 3)
    s_ssm = ssm_out.reshape(depth, M_HEADS, M_HEADDIM, M_STATE, ns).transpose(0, 4, 1, 2, 3)
    outs_s = [unwind(kt_out), unwind(vt_out), jnp.stack(s_conv), s_ssm,
              hg_out.reshape(depth, ns, HG_HEADS, LANES, LANES)]
    return (xp.reshape(bp, seq, d), xs.reshape(ns, 1, d), *outs_p, *outs_s)
```

```python
import functools

import numpy as np
import jax
import jax.numpy as jnp
from jax import lax
from jax.experimental import pallas as pl
from jax.experimental.pallas import tpu as pltpu

F32 = jnp.float32
BF16 = jnp.bfloat16

ATT_HEADS = 8
ATT_KV_HEADS = 2
HEAD_DIM = 64
WINDOW = 128
PAST_LEN = 8192
M_HEADS = 8
M_HEADDIM = 64
M_GROUPS = 2
M_STATE = 64
CONV_W = 4
HG_HEADS = 4
DEPTH = 4
ALPHA = (2.0 * DEPTH) ** 0.25
LN_EPS = 1e-5
RMS_EPS = 1e-6
LOG2_E = 1.4426950408889634
CHUNK = 128

V7X_VMEM_BYTES = 64 * 1024 * 1024
LANES = 128
SUBLANES = 8
MXU_WIDTH = 256

NT_DIMS = (((1,), (1,)), ((), ()))
TN_DIMS = (((0,), (0,)), ((), ()))


def _vmem_limit(nbytes):
    return int(min(V7X_VMEM_BYTES - 8 * 1024 * 1024, nbytes + 16 * 1024 * 1024))


def _params(semantics, vmem_bytes):
    return pltpu.CompilerParams(dimension_semantics=semantics, vmem_limit_bytes=_vmem_limit(vmem_bytes))


def _layer_block(a, layer):
    zeros = (0,) * (a.ndim - 1)
    return pl.BlockSpec((None,) + a.shape[1:], lambda *_: (layer,) + zeros)


def _resident(block_shape, index_map):
    return pl.BlockSpec(block_shape, index_map, pipeline_mode=pl.Buffered(1))


def _silu(x):
    return x * jax.nn.sigmoid(x)


def _softplus(x):
    return jnp.maximum(x, 0.0) + jnp.log(1.0 + jnp.exp(-jnp.abs(x)))


def _layernorm(y, g, b):
    mu = jnp.mean(y, axis=-1, keepdims=True)
    d = y - mu
    var = jnp.mean(d * d, axis=-1, keepdims=True)
    return d * lax.rsqrt(var + LN_EPS) * g + b


def _rms(y):
    return y * lax.rsqrt(jnp.mean(y * y, axis=-1, keepdims=True) + RMS_EPS)


def _split_bf16(x):
    hi = x.astype(BF16)
    lo = (x - hi.astype(F32)).astype(BF16)
    return hi, lo


def _col_chunks(width, step):
    return [(c, min(step, width - c)) for c in range(0, width, step)]


def _ffn_ln_body(x_ref, wg_ref, wu_ref, wd_ref, g_ref, b_ref, o_ref, *, ff_chunk):
    x = x_ref[...]
    xb = x.astype(BF16)
    acc = None
    for c0, cw in _col_chunks(wg_ref.shape[1], ff_chunk):
        gate = jnp.dot(xb, wg_ref[:, c0:c0 + cw], preferred_element_type=F32)
        up = jnp.dot(xb, wu_ref[:, c0:c0 + cw], preferred_element_type=F32)
        hid = (_silu(gate) * up).astype(BF16)
        part = jnp.dot(hid, wd_ref[c0:c0 + cw, :], preferred_element_type=F32)
        acc = part if acc is None else acc + part
    o_ref[...] = _layernorm(ALPHA * x + 0.5 * acc, g_ref[...], b_ref[...])


def _ffn_ln(x, wg, wu, wd, g, b, layer, tm):
    rows, d = x.shape
    f = wg.shape[2]
    wspec = lambda shape: _resident((None,) + shape, lambda i: (layer, 0, 0))
    vmem = 3 * d * f * 2 + 4 * tm * d * 4 + 4 * tm * MXU_WIDTH * 4
    return pl.pallas_call(
        functools.partial(_ffn_ln_body, ff_chunk=MXU_WIDTH),
        out_shape=jax.ShapeDtypeStruct((rows, d), F32),
        grid=(rows // tm,),
        in_specs=[pl.BlockSpec((tm, d), lambda i: (i, 0)),
                  wspec((d, f)), wspec((d, f)), wspec((f, d)),
                  wspec((1, d)), wspec((1, d))],
        out_specs=pl.BlockSpec((tm, d), lambda i: (i, 0)),
        compiler_params=_params(("parallel",), vmem),
        name="ffn_ln",
    )(x, wg, wu, wd, g, b)


def _hgrn_decay(hf, log_lb, log1m_lb, one_m_lb):
    e_neg = jnp.exp(-jnp.abs(hf))
    one_p = 1.0 + e_neg
    log_sig = jnp.minimum(hf, 0.0) - jnp.log(one_p)
    inv = 1.0 / one_p
    kk = one_m_lb * jnp.where(hf >= 0.0, e_neg * inv, inv)
    b = log1m_lb + log_sig
    logf = jnp.maximum(log_lb, b) + jnp.log(1.0 + jnp.exp(-jnp.abs(log_lb - b)))
    return logf, kk


def _project_rows(xb, wt_ref, b_ref, o_refs):
    base = 0
    for o_ref in o_refs:
        for c0, cw in _col_chunks(o_ref.shape[1], 2 * MXU_WIDTH):
            w = wt_ref[base + c0:base + c0 + cw, :]
            o_ref[:, c0:c0 + cw] = (lax.dot_general(xb, w, NT_DIMS, preferred_element_type=F32)
                                    + b_ref[:, base + c0:base + c0 + cw])
        base += o_ref.shape[1]


def _inproj_body(x_ref, wt_ref, b_ref, *o_refs):
    _project_rows(x_ref[...].astype(BF16), wt_ref, b_ref, o_refs)


def _inproj(x, wt, b, layer, tm, widths):
    rows, d = x.shape
    n = wt.shape[1]
    assert n == sum(widths)
    vmem = d * n * 2 + 2 * tm * d * 4 + 2 * tm * n * 4
    return pl.pallas_call(
        _inproj_body,
        out_shape=[jax.ShapeDtypeStruct((rows, wd), F32) for wd in widths],
        grid=(rows // tm,),
        in_specs=[pl.BlockSpec((tm, d), lambda i: (i, 0)),
                  _resident((None, n, d), lambda i: (layer, 0, 0)),
                  _resident((None, 1, n), lambda i: (layer, 0, 0))],
        out_specs=[pl.BlockSpec((tm, wd), lambda i: (i, 0)) for wd in widths],
        compiler_params=_params(("parallel",), vmem),
        name="inproj",
    )(x, wt, b)


def _inproj_sample_body(x_ref, wt_ref, b_ref, bt_ref, *o_refs, col_rows):
    xb = x_ref[...].astype(BF16)
    _project_rows(xb, wt_ref, b_ref, o_refs[:-1])
    r0, r1 = col_rows
    o_refs[-1][...] = lax.dot_general(wt_ref[r0:r1, :], xb, NT_DIMS, preferred_element_type=F32) + bt_ref[...]


def _inproj_sample(x, wt, b, bt, layer, widths, col_rows):
    rows, d = x.shape
    n, nt = wt.shape[1], col_rows[1] - col_rows[0]
    vmem = d * n * 2 + 2 * rows * d * 4 + 2 * rows * (n + nt) * 4
    return pl.pallas_call(
        functools.partial(_inproj_sample_body, col_rows=col_rows),
        out_shape=[jax.ShapeDtypeStruct((rows, wd), F32) for wd in widths]
        + [jax.ShapeDtypeStruct((nt, rows), F32)],
        grid=(1,),
        in_specs=[pl.BlockSpec((rows, d), lambda i: (0, 0)),
                  pl.BlockSpec((None, n, d), lambda i: (layer, 0, 0)),
                  pl.BlockSpec((None, 1, n), lambda i: (layer, 0, 0)),
                  pl.BlockSpec((None, nt, 1), lambda i: (layer, 0, 0))],
        out_specs=[pl.BlockSpec((rows, wd), lambda i: (0, 0)) for wd in widths]
        + [pl.BlockSpec((nt, rows), lambda i: (0, 0))],
        compiler_params=_params(("arbitrary",), vmem),
        name="inproj_sample",
    )(x, wt, b, bt)


def _merge_body(x_ref, ya_ref, ym_ref, yh_ref, wgate_ref, bgate_ref, wa_ref, ws_ref, wh_ref,
                wo_ref, g_ref, b_ref, o_ref):
    x = x_ref[...]
    xb = x.astype(BF16)
    d = x.shape[1]
    branches = ((ya_ref[...].astype(BF16), wa_ref), (ym_ref[...].astype(BF16), ws_ref),
                (yh_ref[...].astype(BF16), wh_ref))
    merged = []
    for c0, cw in _col_chunks(d, MXU_WIDTH):
        m = None
        for k, (yb, w_ref) in enumerate(branches):
            col = k * d + c0
            gate = jax.nn.sigmoid(lax.dot_general(xb, wgate_ref[col:col + cw, :], NT_DIMS,
                                                  preferred_element_type=F32) + bgate_ref[:, col:col + cw])
            term = gate * jnp.dot(yb, w_ref[:, c0:c0 + cw], preferred_element_type=F32)
            m = term if m is None else m + term
        merged.append(m.astype(BF16))
    y = jnp.dot(jnp.concatenate(merged, axis=1), wo_ref[...], preferred_element_type=F32)
    o_ref[...] = _layernorm(ALPHA * x + y, g_ref[...], b_ref[...])


def _merge(x, ya, ym, yh, wgate, bgate, wa, ws, wh, wo, g, b, layer, tm):
    rows, d = x.shape
    wb = ya.shape[1]
    wspec = lambda shape: _resident((None,) + shape, lambda i: (layer, 0, 0))
    row = lambda width: pl.BlockSpec((tm, width), lambda i: (i, 0))
    vmem = (3 * d * d + 3 * wb * d + d * d) * 2 + 4 * tm * d * 4 + 6 * tm * wb * 4
    return pl.pallas_call(
        _merge_body,
        out_shape=jax.ShapeDtypeStruct((rows, d), F32),
        grid=(rows // tm,),
        in_specs=[row(d), row(wb), row(wb), row(wb),
                  wspec((3 * d, d)), wspec((1, 3 * d)),
                  wspec((wb, d)), wspec((wb, d)), wspec((wb, d)), wspec((d, d)),
                  wspec((1, d)), wspec((1, d))],
        out_specs=row(d),
        compiler_params=_params(("parallel",), vmem),
        name="merge_out_ln",
    )(x, ya, ym, yh, wgate, bgate, wa, ws, wh, wo, g, b)


def _alibi_slope_col(head_rows):
    slope = jnp.zeros(head_rows.shape, F32)
    for h in range(ATT_HEADS):
        slope = jnp.where(head_rows == h, 2.0 ** (-8.0 * (h + 1) / ATT_HEADS), slope)
    return slope


def _attn_bias(blk):
    tq = np.arange(blk)[:, None]
    j = np.arange(2 * blk)[None, :]
    dist = WINDOW + tq - j
    ok = (dist >= 0) & (dist <= WINDOW)
    slopes = 2.0 ** (-8.0 * np.arange(1, ATT_HEADS + 1) / ATT_HEADS)
    out = np.empty((2, ATT_HEADS, blk, 2 * blk), np.float32)
    for has_prev in range(2):
        vis = ok & ((j >= blk) | (has_prev == 1))
        out[has_prev] = np.where(vis[None], -slopes[:, None, None] * dist[None], -np.inf)
    return out


def _attn_prompt_body(sink_ref, bias_ref, q_ref, kvc_ref, kvp_ref, o_ref, *, layer):
    blk = kvp_ref.shape[0]
    nsub = q_ref.shape[0] // blk
    grp = ATT_HEADS // ATT_KV_HEADS
    has_prev = jnp.minimum(pl.program_id(1), 1)
    lane = lax.broadcasted_iota(jnp.int32, (blk, LANES), 1)
    low = lane < HEAD_DIM

    def kv_block(i):
        return kvp_ref[...] if i < 0 else kvc_ref[i * blk:(i + 1) * blk, :]

    kcats, vcats = [], []
    for i in range(nsub):
        prev, cur = kv_block(i - 1), kv_block(i)
        kcats.append(jnp.concatenate([prev[:, :LANES], cur[:, :LANES]], axis=0).astype(BF16))
        vcats.append(jnp.concatenate([prev[:, LANES:], cur[:, LANES:]], axis=0).astype(BF16))

    def scores(i, h):
        kv = h // grp
        qcol = q_ref[i * blk:(i + 1) * blk, (h // 2) * LANES:(h // 2 + 1) * LANES] * (HEAD_DIM ** -0.5)
        if (h % 2) != kv:
            qcol = pltpu.roll(qcol, HEAD_DIM, 1)
        qh = jnp.where(low if kv == 0 else jnp.logical_not(low), qcol, 0.0).astype(BF16)
        bias = bias_ref[has_prev, h] if i == 0 else bias_ref[1, h]
        return lax.dot_general(qh, kcats[i], NT_DIMS, preferred_element_type=F32) + bias

    ih = [(i, h) for i in range(nsub) for h in range(ATT_HEADS)]
    ss = {k: scores(*k) for k in ih}
    ms = {k: jnp.maximum(jnp.max(ss[k], axis=1, keepdims=True), sink_ref[layer, k[1]]) for k in ih}
    ps = {k: jnp.exp(ss[k] - ms[k]) for k in ih}
    dens = {k: jnp.sum(ps[k], axis=1, keepdims=True) + jnp.exp(sink_ref[layer, k[1]] - ms[k]) for k in ih}
    ps = {k: (ps[k] * (1.0 / dens[k])).astype(BF16) for k in ih}
    outs = {k: jnp.dot(ps[k], vcats[k[0]], preferred_element_type=F32) for k in ih}
    for i in range(nsub):
        for c in range(ATT_HEADS // 2):
            even, odd = outs[i, 2 * c], outs[i, 2 * c + 1]
            if (2 * c) // grp == 0:
                col = jnp.where(low, even, pltpu.roll(odd, HEAD_DIM, 1))
            else:
                col = jnp.where(low, pltpu.roll(even, HEAD_DIM, 1), odd)
            o_ref[i * blk:(i + 1) * blk, c * LANES:(c + 1) * LANES] = col.astype(o_ref.dtype)


def _attn_prompt(qkv, sinks, bias, layer, batch, seq, nsub):
    rows = nsub * CHUNK
    ns = seq // rows
    qw = ATT_HEADS * HEAD_DIM
    kvw = 2 * ATT_KV_HEADS * HEAD_DIM
    kv_col = qw // kvw
    return pl.pallas_call(
        functools.partial(_attn_prompt_body, layer=layer),
        out_shape=jax.ShapeDtypeStruct((batch * seq, qw), BF16),
        grid=(batch, ns),
        in_specs=[pl.BlockSpec(memory_space=pltpu.SMEM),
                  pl.BlockSpec(bias.shape, lambda b, n: (0,) * bias.ndim),
                  pl.BlockSpec((rows, qw), lambda b, n: (b * ns + n, 0)),
                  pl.BlockSpec((rows, kvw), lambda b, n: (b * ns + n, kv_col)),
                  pl.BlockSpec((CHUNK, kvw), lambda b, n: ((b * ns + n) * nsub - jnp.minimum(n, 1), kv_col))],
        out_specs=pl.BlockSpec((rows, qw), lambda b, n: (b * ns + n, 0)),
        compiler_params=_params(("parallel", "arbitrary"), 16 * 1024 * 1024),
        name="attn_prompt",
    )(sinks, bias, qkv, qkv, qkv)


def _expand_heads(col_of, lane_low):
    cols = []
    for j in range(M_HEADS // 2):
        cols.append(jnp.where(lane_low, col_of(2 * j), col_of(2 * j + 1)))
    return jnp.concatenate(cols, axis=1)


def _ssd_prompt_body(zxd_ref, cw_ref, cb_ref, dtb_ref, a_ref, dskip_ref, nw_ref, tril_ref,
                     y_ref, st_ref, xpad_ref):
    c = pl.program_id(1)
    L = CHUNK
    inner = M_HEADS * M_HEADDIM
    conv_dim = inner + 2 * M_GROUPS * M_STATE

    @pl.when(c == 0)
    def _():
        st_ref[...] = jnp.zeros_like(st_ref)
        xpad_ref[0:SUBLANES, :] = jnp.zeros((SUBLANES, conv_dim), F32)

    lane = lax.broadcasted_iota(jnp.int32, (L, LANES), 1)
    low = lane < M_STATE
    low1 = lane[0:1, :] < M_STATE
    causal = lax.broadcasted_iota(jnp.int32, (L, L), 0) >= lax.broadcasted_iota(jnp.int32, (L, L), 1)
    srow = lax.broadcasted_iota(jnp.int32, (M_GROUPS * M_STATE, inner), 0) // M_STATE
    scol = lax.broadcasted_iota(jnp.int32, (M_GROUPS * M_STATE, inner), 1) // (inner // M_GROUPS)
    hpg = M_HEADS // M_GROUPS

    subs = range(zxd_ref.shape[0] // L)
    heads = range(M_HEADS)
    xs, bm, cm, dt, a = {}, {}, {}, {}, {}
    for r in subs:
        rs = slice(r * L, (r + 1) * L)
        xbc = zxd_ref[rs, inner:inner + conv_dim]
        xpad_ref[SUBLANES:SUBLANES + L, :] = xbc
        conv = xbc * cw_ref[CONV_W - 1:CONV_W, :] + cb_ref[...]
        for w in range(CONV_W - 1):
            off = SUBLANES - (CONV_W - 1) + w
            conv = conv + xpad_ref[off:off + L, :] * cw_ref[w:w + 1, :]
        xpad_ref[0:SUBLANES, :] = xbc[L - SUBLANES:L, :]
        act = _silu(conv)
        xs[r] = act[:, :inner]
        bm[r] = act[:, inner:inner + LANES]
        cm[r] = act[:, inner + LANES:inner + 2 * LANES]
        dt[r] = _softplus(zxd_ref[rs, inner + conv_dim:] + dtb_ref[...])
        dta_hi, dta_lo = _split_bf16(dt[r] * a_ref[...])
        cum = jnp.dot(tril_ref[...], jnp.concatenate([dta_hi, dta_lo], axis=1), preferred_element_type=F32)
        a[r] = cum[:, :LANES] + cum[:, LANES:]
    a_t = {r: a[r].T for r in subs}
    dt_t = {r: dt[r].T for r in subs}
    bmb = {r: bm[r].astype(BF16) for r in subs}
    cb_g = {(r, g): lax.dot_general(jnp.where(low if g == 0 else jnp.logical_not(low), cm[r], 0.0).astype(BF16),
                                    bmb[r], NT_DIMS, preferred_element_type=F32)
            for r in subs for g in range(M_GROUPS)}

    def weights(r, h):
        diff = jnp.broadcast_to(a[r][:, h:h + 1], (L, L)) - jnp.broadcast_to(a_t[r][h:h + 1, :], (L, L))
        decay = jnp.exp(jnp.where(causal, diff, -jnp.inf))
        return (cb_g[r, h // hpg] * decay * jnp.broadcast_to(dt_t[r][h:h + 1, :], (L, L))).astype(BF16)

    ws = {(r, h): weights(r, h) for r in subs for h in heads}
    xhalf = {(r, h): jnp.where(low if h % 2 == 0 else jnp.logical_not(low),
                               xs[r][:, (h // 2) * LANES:(h // 2 + 1) * LANES], 0.0).astype(BF16)
             for r in subs for h in heads}
    yh = {k: jnp.dot(ws[k], xhalf[k], preferred_element_type=F32) for k in ws}
    y_intra = {r: jnp.concatenate([yh[r, 2 * j] + yh[r, 2 * j + 1] for j in range(M_HEADS // 2)], axis=1) for r in subs}
    ea = {r: jnp.exp(a[r]) for r in subs}
    ea_x = {r: _expand_heads(lambda h: jnp.broadcast_to(ea[r][:, h:h + 1], (L, LANES)), low) for r in subs}
    coef = {r: jnp.exp(a[r][L - 1:L, :] - a[r]) * dt[r] for r in subs}
    coef_x = {r: _expand_heads(lambda h: jnp.broadcast_to(coef[r][:, h:h + 1], (L, LANES)), low) for r in subs}
    cs = {r: jnp.dot(bm[r].T.astype(BF16), (xs[r] * coef_x[r]).astype(BF16), preferred_element_type=F32)
          for r in subs}
    dec_x = {}
    for r in subs:
        ea_last = jnp.exp(a[r][L - 1:L, :])
        dec_x[r] = _expand_heads(lambda h: jnp.broadcast_to(ea_last[:, h:h + 1], (1, LANES)), low1)
    gate = {r: _silu(zxd_ref[r * L:(r + 1) * L, 0:inner]) for r in subs}
    for r in subs:
        st = st_ref[...]
        y = y_intra[r] + jnp.dot(cm[r].astype(BF16), st.astype(BF16), preferred_element_type=F32) * ea_x[r]
        st_ref[...] = st * dec_x[r] + jnp.where(srow == scol, cs[r], 0.0)
        y = (y + dskip_ref[...] * xs[r]) * gate[r]
        y_ref[r * L:(r + 1) * L, :] = (_rms(y) * nw_ref[...]).astype(y_ref.dtype)


def _ssd_prompt(zxd, conv_w, conv_b, dtb, a_neg, dskip, norm_w, tril, layer, batch, seq, nsub):
    nc = seq // (nsub * CHUNK)
    width = zxd.shape[1]
    inner = M_HEADS * M_HEADDIM
    conv_dim = conv_w.shape[2]
    consts = (conv_w, conv_b, dtb, a_neg, dskip, norm_w)
    return pl.pallas_call(
        _ssd_prompt_body,
        out_shape=[jax.ShapeDtypeStruct((batch * seq, inner), BF16),
                   jax.ShapeDtypeStruct((batch, M_GROUPS * M_STATE, inner), F32)],
        grid=(batch, nc),
        in_specs=[pl.BlockSpec((nsub * CHUNK, width), lambda b, c: (b * nc + c, 0))]
        + [_layer_block(a, layer) for a in consts] + [pl.BlockSpec(tril.shape, lambda b, c: (0, 0))],
        out_specs=[pl.BlockSpec((nsub * CHUNK, inner), lambda b, c: (b * nc + c, 0)),
                   pl.BlockSpec((None, M_GROUPS * M_STATE, inner), lambda b, c: (b, 0, 0))],
        scratch_shapes=[pltpu.VMEM((SUBLANES + CHUNK, conv_dim), F32)],
        compiler_params=_params(("arbitrary", "arbitrary"), 16 * 1024 * 1024),
        name="ssd_prompt",
    )(zxd, *consts, tril)


HG_LEVELS = tuple(2 ** i for i in range(7))
HG_FINE = tuple(b for b in HG_LEVELS if b < SUBLANES)


def _hgrn_sum_matrix(L):
    r = np.arange(L)
    tt, rr = np.meshgrid(r, r, indexing="ij")
    mats = []
    for b in HG_FINE:
        mid = (r // (2 * b)) * 2 * b + b - 1
        upper = (r // b) % 2 == 1
        up = (mid[:, None] < rr) & (rr <= tt)
        lo = (tt < rr) & (rr <= mid[:, None])
        mats.append(np.where(upper[:, None], up, lo))
    mats.append(rr <= tt)
    m = np.concatenate(mats, axis=0).astype(np.float32)
    return np.concatenate([m, m], axis=1)


def _hgrn_level_tables(L):
    r = np.arange(L)
    later = np.concatenate([np.repeat((((r // b) % 2) == 1)[:, None], LANES, axis=1) for b in HG_LEVELS], axis=0)
    level = np.full((L, L), -1, np.int32)
    for lvl, b in enumerate(HG_LEVELS):
        t_later = ((r // b) % 2 == 1)[:, None]
        s_earlier = ((r // b) % 2 == 0)[None, :]
        same_parent = (r // (2 * b))[:, None] == (r // (2 * b))[None, :]
        level[t_later & s_earlier & same_parent] = lvl
    level[r, r] = len(HG_LEVELS)
    return np.where(later, 1.0, -1.0).astype(np.float32), level


def _hgrn_prompt_body(x_ref, sums_ref, sign_ref, level_ref, lb_ref, nw_ref, y_ref, st_ref):
    c = pl.program_id(1)
    L = CHUNK
    width = HG_HEADS * LANES
    nlev, nfine = len(HG_LEVELS), len(HG_FINE)

    @pl.when(c == 0)
    def _():
        st_ref[...] = jnp.zeros_like(st_ref)

    level = level_ref[...]
    nsub = x_ref.shape[0] // L
    heads = range(HG_HEADS)
    ch = [(r, h) for r in range(nsub) for h in heads]
    part = lambda j, r, h: x_ref[r * L:(r + 1) * L, j * width + h * LANES:j * width + (h + 1) * LANES]
    g2, kks = {}, {}
    for r, h in ch:
        sl = slice(h * LANES, (h + 1) * LANES)
        logf, kks[r, h] = _hgrn_decay(part(1, r, h), lb_ref[0:1, sl], lb_ref[1:2, sl], lb_ref[2:3, sl])
        g2[r, h] = logf * LOG2_E
    es = {}
    for r in range(nsub):
        for pair in range(HG_HEADS // 2):
            parts = [_split_bf16(g2[r, 2 * pair + k]) for k in range(2)]
            w = jnp.concatenate([jnp.concatenate([parts[0][0], parts[1][0]], axis=1),
                                 jnp.concatenate([parts[0][1], parts[1][1]], axis=1)], axis=0)
            e2 = jnp.dot(sums_ref[...], w, preferred_element_type=F32)
            es[r, 2 * pair], es[r, 2 * pair + 1] = e2[:, :LANES], e2[:, LANES:]
    qs = {k: _silu(part(0, *k)) for k in ch}
    bcs = {k: es[k][nfine * L:(nfine + 1) * L] for k in ch}
    atts = {k: jnp.where(level == nlev, jnp.sum(qs[k] * kks[k], axis=1, keepdims=True), 0.0) for k in ch}
    for lvl, b in enumerate(HG_LEVELS):
        for k in ch:
            q, kk, bc = qs[k], kks[k], bcs[k]
            if lvl < nfine:
                later = sign_ref[lvl * L:(lvl + 1) * L, :] > 0.0
                xk = jnp.exp2(es[k][lvl * L:(lvl + 1) * L]) * jnp.where(later, q, kk)
            else:
                blocks = []
                for p in range(0, L, 2 * b):
                    mid = bc[p + b - 1:p + b, :]
                    blocks.append(kk[p:p + b] * jnp.exp2(mid - bc[p:p + b]))
                    blocks.append(q[p + b:p + 2 * b] * jnp.exp2(bc[p + b:p + 2 * b] - mid))
                xk = jnp.concatenate(blocks, axis=0)
            xk = xk.astype(BF16)
            atts[k] = jnp.where(level == lvl, lax.dot_general(xk, xk, NT_DIMS, preferred_element_type=F32), atts[k])
    qds = {k: (qs[k] * jnp.exp2(bcs[k])).astype(BF16) for k in ch}
    kdts = {k: (kks[k] * jnp.exp2(bcs[k][L - 1:L, :] - bcs[k])).T.astype(BF16) for k in ch}
    dcols = {k: jnp.broadcast_to(jnp.exp2(bcs[k][L - 1:L, :]), (L, LANES)).T for k in ch}
    ivbs = {k: part(2, *k).astype(BF16) for k in ch}
    sogs = {k: _silu(part(3, *k)) for k in ch}
    for r in range(nsub):
        for h in heads:
            sl = slice(h * LANES, (h + 1) * LANES)
            st = st_ref[sl, :]
            o = jnp.dot(jnp.concatenate([atts[r, h].astype(BF16), qds[r, h]], axis=1),
                        jnp.concatenate([ivbs[r, h], st.astype(BF16)], axis=0), preferred_element_type=F32)
            st_ref[sl, :] = st * dcols[r, h] + jnp.dot(kdts[r, h], ivbs[r, h], preferred_element_type=F32)
            y_ref[r * L:(r + 1) * L, sl] = (_rms(o) * nw_ref[:, sl] * sogs[r, h]).astype(y_ref.dtype)


def _hgrn_prompt(xh, sums, sign, level, lb_rows, nw, layer, batch, seq, nsub):
    rows = nsub * CHUNK
    nc = seq // rows
    width = HG_HEADS * LANES
    full = lambda a: pl.BlockSpec(a.shape, lambda b, c: (0,) * a.ndim)
    return pl.pallas_call(
        _hgrn_prompt_body,
        out_shape=[jax.ShapeDtypeStruct((batch * seq, width), BF16),
                   jax.ShapeDtypeStruct((batch, width, LANES), F32)],
        grid=(batch, nc),
        in_specs=[pl.BlockSpec((rows, 4 * width), lambda b, c: (b * nc + c, 0)),
                  full(sums), full(sign), full(level), _layer_block(lb_rows, layer), _layer_block(nw, layer)],
        out_specs=[pl.BlockSpec((rows, width), lambda b, c: (b * nc + c, 0)),
                   pl.BlockSpec((None, width, LANES), lambda b, c: (b, 0, 0))],
        compiler_params=_params(("arbitrary", "arbitrary"), 16 * 1024 * 1024),
        name="hgrn_prompt",
    )(xh, sums, sign, level, lb_rows, nw)


def _hi_lo_rows(x_row, row_idx):
    xb = jnp.broadcast_to(x_row, row_idx.shape)
    hi = xb.astype(BF16).astype(F32)
    return jnp.where(row_idx == 0, hi, jnp.where(row_idx == 1, xb - hi, 0.0)).astype(BF16)


def _sample_attn_hgrn_body(q8_ref, kr_ref, vr_ref, kvc_ref, kt_ref, vt_ref, sink_ref,
                           xh_ref, s0_ref, lb_ref, hnw_ref, kt_stack_ref, vt_stack_ref, s_stack_ref,
                           oa_ref, ktn_ref, vtn_ref, yh_ref, sn_ref, *, n_prev_valid):
    del kt_stack_ref, vt_stack_ref, s_stack_ref
    nb = q8_ref.shape[0]
    width = HG_HEADS * LANES
    grp = ATT_HEADS // ATT_KV_HEADS
    row8 = lax.broadcasted_iota(jnp.int32, (SUBLANES, LANES), 0)
    lane8 = lax.broadcasted_iota(jnp.int32, (SUBLANES, LANES), 1)
    hrow = lax.broadcasted_iota(jnp.int32, (ATT_HEADS, 1), 0)
    slope = _alibi_slope_col(hrow)
    jj = lax.broadcasted_iota(jnp.int32, (ATT_HEADS, WINDOW), 1)
    cache_dist = (WINDOW - jj).astype(F32)
    cache_ok = jj >= WINDOW - n_prev_valid
    kv_half = (lane8 // HEAD_DIM) == (row8 // grp)
    last_lane = lax.broadcasted_iota(jnp.int32, (LANES, WINDOW), 1) == WINDOW - 1
    rowx = lax.broadcasted_iota(jnp.int32, (SUBLANES, width), 0)
    lanex = lax.broadcasted_iota(jnp.int32, (SUBLANES, width), 1)
    ones_rows01 = jnp.where(row8 < 2, 1.0, 0.0).astype(BF16)
    kvw = ATT_KV_HEADS * HEAD_DIM

    smp = range(nb)
    sink = sink_ref[:, 0:1]
    head_of_lane = lanex // LANES

    s_c = [jnp.dot(q8_ref[i].astype(BF16), kt_ref[i].astype(BF16), preferred_element_type=F32) * (HEAD_DIM ** -0.5)
           for i in smp]
    s_c = [jnp.where(cache_ok, s_c[i] - slope * cache_dist, -jnp.inf) for i in smp]
    s_n = [jnp.sum(q8_ref[i] * kr_ref[i], axis=1, keepdims=True) * (HEAD_DIM ** -0.5) for i in smp]
    m = [jnp.maximum(jnp.maximum(jnp.max(s_c[i], axis=1, keepdims=True), s_n[i]), sink) for i in smp]
    p_c = [jnp.exp(s_c[i] - m[i]) for i in smp]
    p_n = [jnp.exp(s_n[i] - m[i]) for i in smp]
    den = [jnp.sum(p_c[i], axis=1, keepdims=True) + p_n[i] + jnp.exp(sink - m[i]) for i in smp]
    o = [lax.dot_general((p_c[i] / den[i]).astype(BF16), vt_ref[i].astype(BF16), NT_DIMS,
                         preferred_element_type=F32) + (p_n[i] / den[i]) * vr_ref[i] for i in smp]
    for i in smp:
        oa_ref[i] = jnp.where(kv_half, o[i], 0.0)
        ktn_ref[i] = jnp.where(last_lane, kvc_ref[0:kvw, i:i + 1], pltpu.roll(kt_ref[i], WINDOW - 1, 1))
        vtn_ref[i] = jnp.where(last_lane, kvc_ref[kvw:2 * kvw, i:i + 1], pltpu.roll(vt_ref[i], WINDOW - 1, 1))

    q = [_silu(xh_ref[i][:, :width]) for i in smp]
    dk = [_hgrn_decay(xh_ref[i][:, width:2 * width], lb_ref[0:1, :], lb_ref[1:2, :], lb_ref[2:3, :]) for i in smp]
    f = [jnp.exp(dk[i][0]) for i in smp]
    kk = [dk[i][1] for i in smp]
    iv = [xh_ref[i][:, 2 * width:3 * width] for i in smp]
    lhs_q = [jnp.where(rowx == head_of_lane, jnp.broadcast_to(q[i] * f[i], (SUBLANES, width)), 0.0).astype(BF16)
             for i in smp]
    qs = [jnp.dot(lhs_q[i], s0_ref[i].astype(BF16), preferred_element_type=F32) for i in smp]
    for i in smp:
        qkk = q[i] * kk[i]
        o_parts = []
        for h in range(HG_HEADS):
            sl = slice(h * LANES, (h + 1) * LANES)
            att = jnp.sum(qkk[:, sl], axis=1, keepdims=True)
            o_parts.append(_rms(att * iv[i][:, sl] + qs[i][h:h + 1, :]))
        yh_ref[i] = jnp.concatenate(o_parts, axis=1) * hnw_ref[...] * _silu(xh_ref[i][:, 3 * width:])
    i8 = [jnp.concatenate([iv[i][:, h * LANES:(h + 1) * LANES] for h in range(HG_HEADS)]
                          + [jnp.zeros((SUBLANES - HG_HEADS, LANES), F32)], axis=0).astype(BF16) for i in smp]
    lhs_k = [jnp.where(rowx == head_of_lane, jnp.broadcast_to(kk[i], (SUBLANES, width)), 0.0).astype(BF16)
             for i in smp]
    upd_s = [lax.dot_general(lhs_k[i], i8[i], TN_DIMS, preferred_element_type=F32) for i in smp]
    f_col = [lax.dot_general(_hi_lo_rows(f[i], rowx), ones_rows01, TN_DIMS, preferred_element_type=F32) for i in smp]
    for i in smp:
        sn_ref[i] = s0_ref[i] * f_col[i] + upd_s[i]


def _sample_attn_hgrn(q8, kr, vr, kvc, kt, vt, sink8, xh, s0, lb_rows, hnw, kt_stack, vt_stack, s_stack, layer, nb):
    n = q8.shape[0]
    per = lambda a: pl.BlockSpec((nb,) + a.shape[1:], lambda i: (i,) + (0,) * (a.ndim - 1))
    lay = lambda a: pl.BlockSpec((None, nb) + a.shape[2:], lambda i: (layer, i) + (0,) * (a.ndim - 2))
    hbm = pl.BlockSpec(memory_space=pl.ANY)
    width = HG_HEADS * LANES
    out_shape = [jax.ShapeDtypeStruct((n, ATT_HEADS, LANES), F32),
                 jax.ShapeDtypeStruct(kt_stack.shape, F32),
                 jax.ShapeDtypeStruct(vt_stack.shape, F32),
                 jax.ShapeDtypeStruct((n, 1, width), F32),
                 jax.ShapeDtypeStruct(s_stack.shape, F32)]
    return pl.pallas_call(
        functools.partial(_sample_attn_hgrn_body, n_prev_valid=min(WINDOW, PAST_LEN)),
        out_shape=out_shape,
        grid=(n // nb,),
        in_specs=[per(q8), per(kr), per(vr), pl.BlockSpec((None,) + kvc.shape[1:], lambda i: (i, 0, 0)),
                  lay(kt), lay(vt), _layer_block(sink8, layer), per(xh), lay(s0),
                  _layer_block(lb_rows, layer), _layer_block(hnw, layer), hbm, hbm, hbm],
        out_specs=[per(q8), lay(kt_stack), lay(vt_stack),
                   pl.BlockSpec((nb, 1, width), lambda i: (i, 0, 0)), lay(s_stack)],
        input_output_aliases={11: 1, 12: 2, 13: 4},
        compiler_params=_params(("parallel",), 4 * nb * (2 * kt.shape[2] + s0.shape[2]) * LANES * 4),
        name="sample_attn_hgrn",
    )(q8, kr, vr, kvc, kt, vt, sink8, xh, s0, lb_rows, hnw, kt_stack, vt_stack, s_stack)


def _ssd_sample_body(col_ref, cst_ref, h0_ref, cw_ref, cb_ref, dtb_ref, a_ref, dskip_ref, nw_ref, stack_ref,
                     ym_ref, hn_ref, xs_s, b_s, c_s, dt_s, da_s, yrow_s, y_s):
    del stack_ref
    h = pl.program_id(0)
    inner = M_HEADS * M_HEADDIM
    conv_dim = inner + 2 * M_GROUPS * M_STATE
    kvw2 = 2 * ATT_KV_HEADS * HEAD_DIM
    z0, x0, d0 = kvw2, kvw2 + inner, kvw2 + inner + conv_dim
    ns = col_ref.shape[1]

    @pl.when(h == 0)
    def _():
        conv = col_ref[x0:d0, :] * cw_ref[CONV_W - 1] + cb_ref[...]
        for w in range(CONV_W - 1):
            conv = conv + cst_ref[w] * cw_ref[w]
        act = _silu(conv)
        xs_s[...] = act[:inner]
        b_s[...] = act[inner:inner + M_GROUPS * M_STATE]
        c_s[...] = act[inner + M_GROUPS * M_STATE:]
        dt = _softplus(col_ref[d0:d0 + M_HEADS, :] + dtb_ref[...])
        da = jnp.exp(dt * a_ref[...])
        for hh in range(M_HEADS):
            dt_s[hh] = jnp.broadcast_to(dt[hh:hh + 1, :], (SUBLANES, ns))
            da_s[hh] = jnp.broadcast_to(da[hh:hh + 1, :], (SUBLANES, ns))

    hpg = M_HEADS // M_GROUPS
    x_h = xs_s[pl.ds(pl.multiple_of(h * M_HEADDIM, M_HEADDIM), M_HEADDIM), :]
    g0 = pl.multiple_of((h // hpg) * M_STATE, M_STATE)
    b_g = b_s[pl.ds(g0, M_STATE), :]
    c_g = c_s[pl.ds(g0, M_STATE), :]
    dt = dt_s[h][0:1, :]
    da = da_s[h][0:1, :]
    dtx = x_h * dt
    for p in range(M_HEADDIM):
        s_p = h0_ref[p * M_STATE:(p + 1) * M_STATE, :]
        yrow_s[p:p + 1, :] = jnp.sum(c_g * s_p, axis=0, keepdims=True)
        hn_ref[p * M_STATE:(p + 1) * M_STATE, :] = da * s_p + b_g * dtx[p:p + 1, :]
    cb = jnp.sum(c_g * b_g, axis=0, keepdims=True)
    y_s[pl.ds(pl.multiple_of(h * M_HEADDIM, M_HEADDIM), M_HEADDIM), :] = (
        da * yrow_s[...] + (dt * cb + dskip_ref[h][0:1, :]) * x_h)

    @pl.when(h == pl.num_programs(0) - 1)
    def _():
        y = y_s[...] * _silu(col_ref[z0:x0, :])
        y = y * lax.rsqrt(jnp.mean(y * y, axis=0, keepdims=True) + RMS_EPS) * nw_ref[...]
        for j in range(inner // LANES):
            ym_ref[:, j * LANES:(j + 1) * LANES] = y[j * LANES:(j + 1) * LANES, :].T


def _ssd_sample(col, cst, h0, cw, cb, dtb, a_neg, dskip, nw, stack, layer):
    ns = col.shape[1]
    inner = M_HEADS * M_HEADDIM
    state_block = pl.BlockSpec((None, None) + h0.shape[2:], lambda h: (layer, h, 0, 0))
    return pl.pallas_call(
        _ssd_sample_body,
        out_shape=[jax.ShapeDtypeStruct((ns, inner), F32), jax.ShapeDtypeStruct(stack.shape, F32)],
        grid=(M_HEADS,),
        in_specs=[pl.BlockSpec(col.shape, lambda h: (0, 0)), _layer_block(cst, layer), state_block]
        + [_layer_block(a, layer) for a in (cw, cb, dtb, a_neg, dskip, nw)] + [pl.BlockSpec(memory_space=pl.ANY)],
        out_specs=[pl.BlockSpec((ns, inner), lambda h: (0, 0)), state_block],
        scratch_shapes=[pltpu.VMEM((inner, ns), F32), pltpu.VMEM((M_GROUPS * M_STATE, ns), F32),
                        pltpu.VMEM((M_GROUPS * M_STATE, ns), F32), pltpu.VMEM((M_HEADS, SUBLANES, ns), F32),
                        pltpu.VMEM((M_HEADS, SUBLANES, ns), F32), pltpu.VMEM((M_HEADDIM, ns), F32),
                        pltpu.VMEM((inner, ns), F32)],
        input_output_aliases={9: 1},
        compiler_params=_params(("arbitrary",), 24 * 1024 * 1024),
        name="ssd_sample",
    )(col, cst, h0, cw, cb, dtb, a_neg, dskip, nw, stack)


def _row_tile(rows):
    for tm in (1024, 512, 256, 128):
        if rows % tm == 0:
            return tm
    return rows


def kernel(x_prompt, x_sample, cache_swa_k, cache_swa_v, state_conv, state_ssm, state_hgrn, ln1_g, ln1_b, ffn1_wg, ffn1_wu, ffn1_wd, w_in, b_in, att_sinks, conv_w, conv_b, dt_bias, a_log, d_skip, ssm_norm_w, hg_lb_logits, hg_norm_w, w_br_att, w_br_ssm, w_br_hg, w_out, ln2_g, ln2_b, ffn2_wg, ffn2_wu, ffn2_wd, ln3_g, ln3_b):
    bp, seq, d = x_prompt.shape
    ns = x_sample.shape[0]
    depth = w_in.shape[0]
    assert seq % CHUNK == 0 and x_sample.shape[1] == 1
    qw = ATT_HEADS * HEAD_DIM
    kvw = ATT_KV_HEADS * HEAD_DIM
    inner = M_HEADS * M_HEADDIM
    conv_dim = inner + 2 * M_GROUPS * M_STATE
    width = HG_HEADS * LANES
    hpg = M_HEADS // M_GROUPS

    bf = lambda a: a.astype(BF16)
    row3 = lambda a: a.reshape(depth, 1, a.shape[-1])
    o_z = qw + 2 * kvw
    o_dt = o_z + inner + conv_dim
    o_h = o_dt + M_HEADS
    o_g = o_h + 4 * width
    w_in_t = jnp.swapaxes(w_in, 1, 2)
    dt_pad = jnp.pad(w_in_t[:, o_dt:o_h, :], ((0, 0), (0, LANES - M_HEADS), (0, 0)))
    w_rows = bf(jnp.concatenate([w_in_t[:, :o_dt, :], dt_pad, w_in_t[:, o_h:o_g, :]], axis=1))
    b_flat = jnp.concatenate([b_in[:, :o_dt], jnp.pad(b_in[:, o_dt:o_h], ((0, 0), (0, LANES - M_HEADS))),
                              b_in[:, o_h:o_g]], axis=1)
    b_rows = row3(b_flat)
    col_rows = (qw, o_dt + LANES)
    b_cols = b_flat[:, col_rows[0]:col_rows[1], None]
    w_gate, b_gate = bf(w_in_t[:, o_g:, :]), row3(b_in[:, o_g:])
    in_widths = (qw + 2 * kvw, inner + conv_dim + LANES, 4 * width)
    ffn1 = (bf(ffn1_wg), bf(ffn1_wu), bf(ffn1_wd), row3(ln1_g), row3(ln1_b))
    ffn2 = (bf(ffn2_wg), bf(ffn2_wu), bf(ffn2_wd), row3(ln3_g), row3(ln3_b))
    merge_w = (w_gate, b_gate, bf(w_br_att), bf(w_br_ssm), bf(w_br_hg), bf(w_out), row3(ln2_g), row3(ln2_b))

    a_neg = -jnp.exp(a_log.astype(F32))
    pad_h = lambda a: row3(jnp.pad(a, ((0, 0), (0, LANES - M_HEADS))))
    ssd_rows = (conv_w, row3(conv_b), pad_h(dt_bias), pad_h(a_neg), row3(jnp.repeat(d_skip, M_HEADDIM, axis=1)),
                row3(ssm_norm_w))
    lb_all = jnp.cumsum(jax.nn.softmax(hg_lb_logits.astype(F32), axis=0), axis=0)
    lb_all = lb_all - lb_all[0]
    lb_rows = jnp.stack([jnp.log(lb_all), jnp.log1p(-lb_all), 1.0 - lb_all], axis=1)
    lb_rows = jnp.pad(lb_rows, ((0, 0), (0, SUBLANES - 3), (0, 0)))
    hnw = row3(jnp.tile(hg_norm_w, (1, HG_HEADS)))
    sums = jnp.asarray(_hgrn_sum_matrix(CHUNK), BF16)
    sign_np, level_np = _hgrn_level_tables(CHUNK)
    hg_sign, hg_level = jnp.asarray(sign_np), jnp.asarray(level_np)
    att_bias = jnp.asarray(_attn_bias(CHUNK))
    tril = jnp.asarray(np.tril(np.ones((CHUNK, CHUNK), np.float32)), BF16)
    sink8 = jnp.broadcast_to(att_sinks[:, :, None], (depth, ATT_HEADS, LANES))

    tm_p = _row_tile(bp * seq)
    tm_s = _row_tile(ns)
    nb_s = max(k for k in (2 * SUBLANES, SUBLANES, 1) if ns % k == 0)
    lanes_b = lambda a: jnp.broadcast_to(a[..., None], a.shape + (ns,))
    kt_in = cache_swa_k.transpose(0, 1, 3, 4, 2).reshape(depth, ns, kvw, WINDOW)
    vt_in = cache_swa_v.transpose(0, 1, 3, 4, 2).reshape(depth, ns, kvw, WINDOW)
    ssm_in = state_ssm.transpose(0, 2, 3, 4, 1).reshape(depth, M_HEADS, M_HEADDIM * M_STATE, ns)
    conv_in = state_conv.transpose(0, 2, 3, 1)
    hg_in = state_hgrn.reshape(depth, ns, width, LANES)
    kt_out, vt_out = jnp.zeros(kt_in.shape, F32), jnp.zeros(vt_in.shape, F32)
    ssm_out, hg_out = jnp.zeros(ssm_in.shape, F32), jnp.zeros(hg_in.shape, F32)
    cw_b, cb_b = lanes_b(conv_w), lanes_b(conv_b)
    dtb_b, a_b, nw_b = lanes_b(dt_bias), lanes_b(a_neg), lanes_b(ssm_norm_w)
    dskip_b = jnp.broadcast_to(d_skip[:, :, None, None], (depth, M_HEADS, SUBLANES, ns))
    att_sub = max(k for k in (4, 2, 1) if (seq // CHUNK) % k == 0)
    ssd_sub = max(k for k in (4, 2, 1) if (seq // CHUNK) % k == 0)
    scan_sub = max(k for k in (2, 1) if (seq // CHUNK) % k == 0)

    xp = x_prompt.reshape(bp * seq, d)
    xs = x_sample.reshape(ns, d)
    p_states, s_conv = [], []
    for l in range(depth):
        xp = _ffn_ln(xp, *ffn1, l, tm_p)
        qkv, zxd, xh = _inproj(xp, w_rows, b_rows, l, tm_p, in_widths)
        ya = _attn_prompt(qkv, att_sinks, att_bias, l, bp, seq, att_sub)
        ym, st_ssm = _ssd_prompt(zxd, *ssd_rows, tril, l, bp, seq, ssd_sub)
        yh, st_hg = _hgrn_prompt(xh, sums, hg_sign, hg_level, lb_rows, hnw, l, bp, seq, scan_sub)
        xp = _merge(xp, ya, ym, yh, *merge_w, l, tm_p)
        xp = _ffn_ln(xp, *ffn2, l, tm_p)
        qkv3 = qkv.reshape(bp, seq, qw + 2 * kvw)
        p_k = qkv3[:, seq - WINDOW:, qw:qw + kvw].reshape(bp, WINDOW, ATT_KV_HEADS, HEAD_DIM)
        p_v = qkv3[:, seq - WINDOW:, qw + kvw:].reshape(bp, WINDOW, ATT_KV_HEADS, HEAD_DIM)
        p_conv = zxd.reshape(bp, seq, -1)[:, seq - (CONV_W - 1):, inner:inner + conv_dim]
        st6 = st_ssm.reshape(bp, M_GROUPS, M_STATE, M_GROUPS, hpg, M_HEADDIM)
        p_ssm = jnp.stack([st6[:, g, :, g] for g in range(M_GROUPS)], axis=1)
        p_ssm = p_ssm.transpose(0, 1, 3, 4, 2).reshape(bp, M_HEADS, M_HEADDIM, M_STATE)
        p_hg = st_hg.reshape(bp, HG_HEADS, LANES, LANES)
        p_states.append((p_k, p_v, p_conv, p_ssm, p_hg))

        xs = _ffn_ln(xs, *ffn1, l, tm_s)
        qkv_s, zx_s, xh_s, col_s = _inproj_sample(xs, w_rows, b_rows, b_cols, l, in_widths, col_rows)
        q4 = qkv_s[:, :qw].reshape(ns, ATT_KV_HEADS, ATT_HEADS // ATT_KV_HEADS, HEAD_DIM)
        zq = jnp.zeros_like(q4[:, 0])
        q8 = jnp.concatenate([jnp.concatenate([q4[:, 0], zq], axis=-1),
                              jnp.concatenate([zq, q4[:, 1]], axis=-1)], axis=1)
        kv_cols = col_s[:2 * kvw].reshape(2 * kvw, ns // nb_s, nb_s).transpose(1, 0, 2)
        oa, kt_out, vt_out, yh_s, hg_out = _sample_attn_hgrn(
            q8, qkv_s[:, None, qw:qw + kvw], qkv_s[:, None, qw + kvw:], kv_cols, kt_in, vt_in, sink8,
            xh_s[:, None, :], hg_in, lb_rows, hnw, kt_out, vt_out, hg_out, l, nb_s)
        ym_s, ssm_out = _ssd_sample(col_s, conv_in, ssm_in, cw_b, cb_b, dtb_b, a_b, dskip_b, nw_b, ssm_out, l)
        grp = ATT_HEADS // ATT_KV_HEADS
        ya_s = jnp.concatenate([oa[:, :grp, :HEAD_DIM].reshape(ns, grp * HEAD_DIM),
                                oa[:, grp:, HEAD_DIM:].reshape(ns, grp * HEAD_DIM)], axis=1)
        xs = _merge(xs, ya_s, ym_s, yh_s.reshape(ns, width), *merge_w, l, tm_s)
        xs = _ffn_ln(xs, *ffn2, l, tm_s)
        s_conv.append(jnp.concatenate([state_conv[l][:, 1:], zx_s[:, None, inner:inner + conv_dim]], axis=1))

    outs_p = [jnp.stack(t) for t in zip(*p_states)]
    unwind = lambda t: t.reshape(depth, ns, ATT_KV_HEADS, HEAD_DIM, WINDOW).transpose(0, 1, 4, 2, 3)
    s_ssm = ssm_out.reshape(depth, M_HEADS, M_HEADDIM, M_STATE, ns).transpose(0, 4, 1, 2, 3)
    outs_s = [unwind(kt_out), unwind(vt_out), jnp.stack(s_conv), s_ssm,
              hg_out.reshape(depth, ns, HG_HEADS, LANES, LANES)]
    return (xp.reshape(bp, seq, d), xs.reshape(ns, 1, d), *outs_p, *outs_s)
```

```python
import functools

import numpy as np
import jax
import jax.numpy as jnp
from jax import lax
from jax.experimental import pallas as pl
from jax.experimental.pallas import tpu as pltpu

F32 = jnp.float32
BF16 = jnp.bfloat16

ATT_HEADS = 8
ATT_KV_HEADS = 2
HEAD_DIM = 64
WINDOW = 128
PAST_LEN = 8192
M_HEADS = 8
M_HEADDIM = 64
M_GROUPS = 2
M_STATE = 64
CONV_W = 4
HG_HEADS = 4
DEPTH = 4
ALPHA = (2.0 * DEPTH) ** 0.25
LN_EPS = 1e-5
RMS_EPS = 1e-6
LOG2_E = 1.4426950408889634
CHUNK = 128

V7X_VMEM_BYTES = 64 * 1024 * 1024
LANES = 128
SUBLANES = 8
MXU_WIDTH = 256

NT_DIMS = (((1,), (1,)), ((), ()))
TN_DIMS = (((0,), (0,)), ((), ()))


def _vmem_limit(nbytes):
    return int(min(V7X_VMEM_BYTES - 8 * 1024 * 1024, nbytes + 16 * 1024 * 1024))


def _params(semantics, vmem_bytes):
    return pltpu.CompilerParams(dimension_semantics=semantics, vmem_limit_bytes=_vmem_limit(vmem_bytes))


def _layer_block(a, layer):
    zeros = (0,) * (a.ndim - 1)
    return pl.BlockSpec((None,) + a.shape[1:], lambda *_: (layer,) + zeros)


def _resident(block_shape, index_map):
    return pl.BlockSpec(block_shape, index_map, pipeline_mode=pl.Buffered(1))


def _silu(x):
    return x * jax.nn.sigmoid(x)


def _softplus(x):
    return jnp.maximum(x, 0.0) + jnp.log(1.0 + jnp.exp(-jnp.abs(x)))


def _layernorm(y, g, b):
    mu = jnp.mean(y, axis=-1, keepdims=True)
    d = y - mu
    var = jnp.mean(d * d, axis=-1, keepdims=True)
    return d * lax.rsqrt(var + LN_EPS) * g + b


def _rms(y):
    return y * lax.rsqrt(jnp.mean(y * y, axis=-1, keepdims=True) + RMS_EPS)


def _split_bf16(x):
    hi = x.astype(BF16)
    lo = (x - hi.astype(F32)).astype(BF16)
    return hi, lo


def _col_chunks(width, step):
    return [(c, min(step, width - c)) for c in range(0, width, step)]


def _ffn_ln_body(x_ref, wg_ref, wu_ref, wd_ref, g_ref, b_ref, o_ref, *, ff_chunk):
    x = x_ref[...]
    xb = x.astype(BF16)
    acc = None
    for c0, cw in _col_chunks(wg_ref.shape[1], ff_chunk):
        gate = jnp.dot(xb, wg_ref[:, c0:c0 + cw], preferred_element_type=F32)
        up = jnp.dot(xb, wu_ref[:, c0:c0 + cw], preferred_element_type=F32)
        hid = (_silu(gate) * up).astype(BF16)
        part = jnp.dot(hid, wd_ref[c0:c0 + cw, :], preferred_element_type=F32)
        acc = part if acc is None else acc + part
    o_ref[...] = _layernorm(ALPHA * x + 0.5 * acc, g_ref[...], b_ref[...])


def _ffn_ln(x, wg, wu, wd, g, b, layer, tm):
    rows, d = x.shape
    f = wg.shape[2]
    wspec = lambda shape: _resident((None,) + shape, lambda i: (layer, 0, 0))
    vmem = 3 * d * f * 2 + 4 * tm * d * 4 + 4 * tm * MXU_WIDTH * 4
    return pl.pallas_call(
        functools.partial(_ffn_ln_body, ff_chunk=MXU_WIDTH),
        out_shape=jax.ShapeDtypeStruct((rows, d), F32),
        grid=(rows // tm,),
        in_specs=[pl.BlockSpec((tm, d), lambda i: (i, 0)),
                  wspec((d, f)), wspec((d, f)), wspec((f, d)),
                  wspec((1, d)), wspec((1, d))],
        out_specs=pl.BlockSpec((tm, d), lambda i: (i, 0)),
        compiler_params=_params(("parallel",), vmem),
        name="ffn_ln",
    )(x, wg, wu, wd, g, b)


def _hgrn_decay(hf, log_lb, log1m_lb, one_m_lb):
    e_neg = jnp.exp(-jnp.abs(hf))
    one_p = 1.0 + e_neg
    log_sig = jnp.minimum(hf, 0.0) - jnp.log(one_p)
    inv = 1.0 / one_p
    kk = one_m_lb * jnp.where(hf >= 0.0, e_neg * inv, inv)
    b = log1m_lb + log_sig
    logf = jnp.maximum(log_lb, b) + jnp.log(1.0 + jnp.exp(-jnp.abs(log_lb - b)))
    return logf, kk


def _project_rows(xb, wt_ref, b_ref, o_refs):
    base = 0
    for o_ref in o_refs:
        for c0, cw in _col_chunks(o_ref.shape[1], 2 * MXU_WIDTH):
            w = wt_ref[base + c0:base + c0 + cw, :]
            o_ref[:, c0:c0 + cw] = (lax.dot_general(xb, w, NT_DIMS, preferred_element_type=F32)
                                    + b_ref[:, base + c0:base + c0 + cw])
        base += o_ref.shape[1]


def _inproj_body(x_ref, wt_ref, b_ref, *o_refs):
    _project_rows(x_ref[...].astype(BF16), wt_ref, b_ref, o_refs)


def _inproj(x, wt, b, layer, tm, widths):
    rows, d = x.shape
    n = wt.shape[1]
    assert n == sum(widths)
    vmem = d * n * 2 + 2 * tm * d * 4 + 2 * tm * n * 4
    return pl.pallas_call(
        _inproj_body,
        out_shape=[jax.ShapeDtypeStruct((rows, wd), F32) for wd in widths],
        grid=(rows // tm,),
        in_specs=[pl.BlockSpec((tm, d), lambda i: (i, 0)),
                  _resident((None, n, d), lambda i: (layer, 0, 0)),
                  _resident((None, 1, n), lambda i: (layer, 0, 0))],
        out_specs=[pl.BlockSpec((tm, wd), lambda i: (i, 0)) for wd in widths],
        compiler_params=_params(("parallel",), vmem),
        name="inproj",
    )(x, wt, b)


def _inproj_sample_body(x_ref, wt_ref, b_ref, bt_ref, *o_refs, col_rows):
    xb = x_ref[...].astype(BF16)
    _project_rows(xb, wt_ref, b_ref, o_refs[:-1])
    r0, r1 = col_rows
    o_refs[-1][...] = lax.dot_general(wt_ref[r0:r1, :], xb, NT_DIMS, preferred_element_type=F32) + bt_ref[...]


def _inproj_sample(x, wt, b, bt, layer, widths, col_rows):
    rows, d = x.shape
    n, nt = wt.shape[1], col_rows[1] - col_rows[0]
    vmem = d * n * 2 + 2 * rows * d * 4 + 2 * rows * (n + nt) * 4
    return pl.pallas_call(
        functools.partial(_inproj_sample_body, col_rows=col_rows),
        out_shape=[jax.ShapeDtypeStruct((rows, wd), F32) for wd in widths]
        + [jax.ShapeDtypeStruct((nt, rows), F32)],
        grid=(1,),
        in_specs=[pl.BlockSpec((rows, d), lambda i: (0, 0)),
                  pl.BlockSpec((None, n, d), lambda i: (layer, 0, 0)),
                  pl.BlockSpec((None, 1, n), lambda i: (layer, 0, 0)),
                  pl.BlockSpec((None, nt, 1), lambda i: (layer, 0, 0))],
        out_specs=[pl.BlockSpec((rows, wd), lambda i: (0, 0)) for wd in widths]
        + [pl.BlockSpec((nt, rows), lambda i: (0, 0))],
        compiler_params=_params(("arbitrary",), vmem),
        name="inproj_sample",
    )(x, wt, b, bt)


N_BRANCH = 3


def _merge_body(x_ref, y_ref, w_ref, bgate_ref, g_ref, b_ref, o_ref):
    x = x_ref[...]
    xb = x.astype(BF16)
    d = x.shape[1]
    wb = y_ref.shape[1] // N_BRANCH
    br0, out0 = N_BRANCH * d, N_BRANCH * d + N_BRANCH * wb
    ybs = [y_ref[:, k * wb:(k + 1) * wb].astype(BF16) for k in range(N_BRANCH)]
    merged = []
    for c0, cw in _col_chunks(d, MXU_WIDTH):
        m = None
        for k in range(N_BRANCH):
            col = k * d + c0
            gate = jax.nn.sigmoid(lax.dot_general(xb, w_ref[col:col + cw, :], NT_DIMS,
                                                  preferred_element_type=F32) + bgate_ref[:, col:col + cw])
            term = gate * jnp.dot(ybs[k], w_ref[br0 + k * wb:br0 + (k + 1) * wb, c0:c0 + cw],
                                  preferred_element_type=F32)
            m = term if m is None else m + term
        merged.append(m.astype(BF16))
    y = jnp.dot(jnp.concatenate(merged, axis=1), w_ref[out0:out0 + d, :], preferred_element_type=F32)
    o_ref[...] = _layernorm(ALPHA * x + y, g_ref[...], b_ref[...])


def _merge(x, y, w, bgate, g, b, layer, tm):
    rows, d = x.shape
    wspec = lambda a: _resident((None,) + a.shape[1:], lambda i: (layer, 0, 0))
    row = lambda width: pl.BlockSpec((tm, width), lambda i: (i, 0))
    vmem = w.shape[1] * d * 2 + 4 * tm * d * 4 + 2 * tm * y.shape[1] * y.dtype.itemsize
    return pl.pallas_call(
        _merge_body,
        out_shape=jax.ShapeDtypeStruct((rows, d), F32),
        grid=(rows // tm,),
        in_specs=[row(d), row(y.shape[1]), wspec(w), wspec(bgate), wspec(g), wspec(b)],
        out_specs=row(d),
        compiler_params=_params(("parallel",), vmem),
        name="merge_out_ln",
    )(x, y, w, bgate, g, b)


def _alibi_slope_col(head_rows):
    slope = jnp.zeros(head_rows.shape, F32)
    for h in range(ATT_HEADS):
        slope = jnp.where(head_rows == h, 2.0 ** (-8.0 * (h + 1) / ATT_HEADS), slope)
    return slope


def _attn_bias(blk):
    tq = np.arange(blk)[:, None]
    j = np.arange(2 * blk)[None, :]
    dist = WINDOW + tq - j
    ok = (dist >= 0) & (dist <= WINDOW)
    slopes = 2.0 ** (-8.0 * np.arange(1, ATT_HEADS + 1) / ATT_HEADS)
    out = np.empty((2, ATT_HEADS, blk, 2 * blk), np.float32)
    for has_prev in range(2):
        vis = ok & ((j >= blk) | (has_prev == 1))
        out[has_prev] = np.where(vis[None], -slopes[:, None, None] * dist[None], -np.inf)
    return out


def _attn_prompt_body(sink_ref, bias_ref, q_ref, kvc_ref, kvp_ref, o_ref, *, layer):
    blk = kvp_ref.shape[0]
    nsub = q_ref.shape[0] // blk
    grp = ATT_HEADS // ATT_KV_HEADS
    has_prev = jnp.minimum(pl.program_id(1), 1)
    lane = lax.broadcasted_iota(jnp.int32, (blk, LANES), 1)
    low = lane < HEAD_DIM

    def kv_block(i):
        return kvp_ref[...] if i < 0 else kvc_ref[i * blk:(i + 1) * blk, :]

    kcats, vcats = [], []
    for i in range(nsub):
        prev, cur = kv_block(i - 1), kv_block(i)
        kcats.append(jnp.concatenate([prev[:, :LANES], cur[:, :LANES]], axis=0).astype(BF16))
        vcats.append(jnp.concatenate([prev[:, LANES:], cur[:, LANES:]], axis=0).astype(BF16))

    def scores(i, h):
        kv = h // grp
        qcol = q_ref[i * blk:(i + 1) * blk, (h // 2) * LANES:(h // 2 + 1) * LANES] * (HEAD_DIM ** -0.5)
        if (h % 2) != kv:
            qcol = pltpu.roll(qcol, HEAD_DIM, 1)
        qh = jnp.where(low if kv == 0 else jnp.logical_not(low), qcol, 0.0).astype(BF16)
        bias = bias_ref[has_prev, h] if i == 0 else bias_ref[1, h]
        return lax.dot_general(qh, kcats[i], NT_DIMS, preferred_element_type=F32) + bias

    ih = [(i, h) for i in range(nsub) for h in range(ATT_HEADS)]
    ss = {k: scores(*k) for k in ih}
    ms = {k: jnp.maximum(jnp.max(ss[k], axis=1, keepdims=True), sink_ref[layer, k[1]]) for k in ih}
    ps = {k: jnp.exp(ss[k] - ms[k]) for k in ih}
    dens = {k: jnp.sum(ps[k], axis=1, keepdims=True) + jnp.exp(sink_ref[layer, k[1]] - ms[k]) for k in ih}
    ps = {k: (ps[k] * (1.0 / dens[k])).astype(BF16) for k in ih}
    outs = {k: jnp.dot(ps[k], vcats[k[0]], preferred_element_type=F32) for k in ih}
    for i in range(nsub):
        for c in range(ATT_HEADS // 2):
            even, odd = outs[i, 2 * c], outs[i, 2 * c + 1]
            if (2 * c) // grp == 0:
                col = jnp.where(low, even, pltpu.roll(odd, HEAD_DIM, 1))
            else:
                col = jnp.where(low, pltpu.roll(even, HEAD_DIM, 1), odd)
            o_ref[i * blk:(i + 1) * blk, c * LANES:(c + 1) * LANES] = col.astype(o_ref.dtype)
    qw = ATT_HEADS * HEAD_DIM
    o_ref[:, qw:] = jnp.zeros((o_ref.shape[0], o_ref.shape[1] - qw), o_ref.dtype)


def _attn_prompt(qkv, sinks, bias, layer, batch, seq, nsub):
    rows = nsub * CHUNK
    ns = seq // rows
    qw = ATT_HEADS * HEAD_DIM
    kvw = 2 * ATT_KV_HEADS * HEAD_DIM
    kv_col = qw // kvw
    return pl.pallas_call(
        functools.partial(_attn_prompt_body, layer=layer),
        out_shape=jax.ShapeDtypeStruct((batch * seq, N_BRANCH * qw), BF16),
        grid=(batch, ns),
        in_specs=[pl.BlockSpec(memory_space=pltpu.SMEM),
                  pl.BlockSpec(bias.shape, lambda b, n: (0,) * bias.ndim),
                  pl.BlockSpec((rows, qw), lambda b, n: (b * ns + n, 0)),
                  pl.BlockSpec((rows, kvw), lambda b, n: (b * ns + n, kv_col)),
                  pl.BlockSpec((CHUNK, kvw), lambda b, n: ((b * ns + n) * nsub - jnp.minimum(n, 1), kv_col))],
        out_specs=pl.BlockSpec((rows, N_BRANCH * qw), lambda b, n: (b * ns + n, 0)),
        compiler_params=_params(("parallel", "arbitrary"), 16 * 1024 * 1024),
        name="attn_prompt",
    )(sinks, bias, qkv, qkv, qkv)


def _expand_heads(col_of, lane_low):
    cols = []
    for j in range(M_HEADS // 2):
        cols.append(jnp.where(lane_low, col_of(2 * j), col_of(2 * j + 1)))
    return jnp.concatenate(cols, axis=1)


def _ssd_prompt_body(zxd_ref, cw_ref, cb_ref, dtb_ref, a_ref, dskip_ref, nw_ref, tril_ref, y_shared_ref,
                     y_ref, st_ref, xpad_ref):
    del y_shared_ref
    c = pl.program_id(1)
    L = CHUNK
    inner = M_HEADS * M_HEADDIM
    conv_dim = inner + 2 * M_GROUPS * M_STATE

    @pl.when(c == 0)
    def _():
        st_ref[...] = jnp.zeros_like(st_ref)
        xpad_ref[0:SUBLANES, :] = jnp.zeros((SUBLANES, conv_dim), F32)

    lane = lax.broadcasted_iota(jnp.int32, (L, LANES), 1)
    low = lane < M_STATE
    low1 = lane[0:1, :] < M_STATE
    causal = lax.broadcasted_iota(jnp.int32, (L, L), 0) >= lax.broadcasted_iota(jnp.int32, (L, L), 1)
    srow = lax.broadcasted_iota(jnp.int32, (M_GROUPS * M_STATE, inner), 0) // M_STATE
    scol = lax.broadcasted_iota(jnp.int32, (M_GROUPS * M_STATE, inner), 1) // (inner // M_GROUPS)
    hpg = M_HEADS // M_GROUPS

    subs = range(zxd_ref.shape[0] // L)
    heads = range(M_HEADS)
    xs, bm, cm, dt, a = {}, {}, {}, {}, {}
    for r in subs:
        rs = slice(r * L, (r + 1) * L)
        xbc = zxd_ref[rs, inner:inner + conv_dim]
        xpad_ref[SUBLANES:SUBLANES + L, :] = xbc
        conv = xbc * cw_ref[CONV_W - 1:CONV_W, :] + cb_ref[...]
        for w in range(CONV_W - 1):
            off = SUBLANES - (CONV_W - 1) + w
            conv = conv + xpad_ref[off:off + L, :] * cw_ref[w:w + 1, :]
        xpad_ref[0:SUBLANES, :] = xbc[L - SUBLANES:L, :]
        act = _silu(conv)
        xs[r] = act[:, :inner]
        bm[r] = act[:, inner:inner + LANES]
        cm[r] = act[:, inner + LANES:inner + 2 * LANES]
        dt[r] = _softplus(zxd_ref[rs, inner + conv_dim:] + dtb_ref[...])
        dta_hi, dta_lo = _split_bf16(dt[r] * a_ref[...])
        cum = jnp.dot(tril_ref[...], jnp.concatenate([dta_hi, dta_lo], axis=1), preferred_element_type=F32)
        a[r] = cum[:, :LANES] + cum[:, LANES:]
    a_t = {r: a[r].T for r in subs}
    dt_t = {r: dt[r].T for r in subs}
    bmb = {r: bm[r].astype(BF16) for r in subs}
    cb_g = {(r, g): lax.dot_general(jnp.where(low if g == 0 else jnp.logical_not(low), cm[r], 0.0).astype(BF16),
                                    bmb[r], NT_DIMS, preferred_element_type=F32)
            for r in subs for g in range(M_GROUPS)}

    def weights(r, h):
        diff = jnp.broadcast_to(a[r][:, h:h + 1], (L, L)) - jnp.broadcast_to(a_t[r][h:h + 1, :], (L, L))
        decay = jnp.exp(jnp.where(causal, diff, -jnp.inf))
        return (cb_g[r, h // hpg] * decay * jnp.broadcast_to(dt_t[r][h:h + 1, :], (L, L))).astype(BF16)

    ws = {(r, h): weights(r, h) for r in subs for h in heads}
    xhalf = {(r, h): jnp.where(low if h % 2 == 0 else jnp.logical_not(low),
                               xs[r][:, (h // 2) * LANES:(h // 2 + 1) * LANES], 0.0).astype(BF16)
             for r in subs for h in heads}
    yh = {k: jnp.dot(ws[k], xhalf[k], preferred_element_type=F32) for k in ws}
    y_intra = {r: jnp.concatenate([yh[r, 2 * j] + yh[r, 2 * j + 1] for j in range(M_HEADS // 2)], axis=1) for r in subs}
    ea = {r: jnp.exp(a[r]) for r in subs}
    ea_x = {r: _expand_heads(lambda h: jnp.broadcast_to(ea[r][:, h:h + 1], (L, LANES)), low) for r in subs}
    coef = {r: jnp.exp(a[r][L - 1:L, :] - a[r]) * dt[r] for r in subs}
    coef_x = {r: _expand_heads(lambda h: jnp.broadcast_to(coef[r][:, h:h + 1], (L, LANES)), low) for r in subs}
    cs = {r: jnp.dot(bm[r].T.astype(BF16), (xs[r] * coef_x[r]).astype(BF16), preferred_element_type=F32)
          for r in subs}
    dec_x = {}
    for r in subs:
        ea_last = jnp.exp(a[r][L - 1:L, :])
        dec_x[r] = _expand_heads(lambda h: jnp.broadcast_to(ea_last[:, h:h + 1], (1, LANES)), low1)
    gate = {r: _silu(zxd_ref[r * L:(r + 1) * L, 0:inner]) for r in subs}
    for r in subs:
        st = st_ref[...]
        y = y_intra[r] + jnp.dot(cm[r].astype(BF16), st.astype(BF16), preferred_element_type=F32) * ea_x[r]
        st_ref[...] = st * dec_x[r] + jnp.where(srow == scol, cs[r], 0.0)
        y = (y + dskip_ref[...] * xs[r]) * gate[r]
        y_ref[r * L:(r + 1) * L, :] = (_rms(y) * nw_ref[...]).astype(y_ref.dtype)


def _ssd_prompt(zxd, conv_w, conv_b, dtb, a_neg, dskip, norm_w, tril, y_shared, layer, batch, seq, nsub):
    nc = seq // (nsub * CHUNK)
    width = zxd.shape[1]
    inner = M_HEADS * M_HEADDIM
    conv_dim = conv_w.shape[2]
    consts = (conv_w, conv_b, dtb, a_neg, dskip, norm_w)
    return pl.pallas_call(
        _ssd_prompt_body,
        out_shape=[jax.ShapeDtypeStruct(y_shared.shape, y_shared.dtype),
                   jax.ShapeDtypeStruct((batch, M_GROUPS * M_STATE, inner), F32)],
        grid=(batch, nc),
        in_specs=[pl.BlockSpec((nsub * CHUNK, width), lambda b, c: (b * nc + c, 0))]
        + [_layer_block(a, layer) for a in consts] + [pl.BlockSpec(tril.shape, lambda b, c: (0, 0)),
                                                      pl.BlockSpec(memory_space=pl.ANY)],
        out_specs=[pl.BlockSpec((nsub * CHUNK, inner), lambda b, c: (b * nc + c, 1)),
                   pl.BlockSpec((None, M_GROUPS * M_STATE, inner), lambda b, c: (b, 0, 0))],
        scratch_shapes=[pltpu.VMEM((SUBLANES + CHUNK, conv_dim), F32)],
        input_output_aliases={len(consts) + 2: 0},
        compiler_params=_params(("arbitrary", "arbitrary"), 16 * 1024 * 1024),
        name="ssd_prompt",
    )(zxd, *consts, tril, y_shared)


HG_LEVELS = tuple(2 ** i for i in range(7))
HG_FINE = tuple(b for b in HG_LEVELS if b < SUBLANES)


def _hgrn_sum_matrix(L):
    r = np.arange(L)
    tt, rr = np.meshgrid(r, r, indexing="ij")
    mats = []
    for b in HG_FINE:
        mid = (r // (2 * b)) * 2 * b + b - 1
        upper = (r // b) % 2 == 1
        up = (mid[:, None] < rr) & (rr <= tt)
        lo = (tt < rr) & (rr <= mid[:, None])
        mats.append(np.where(upper[:, None], up, lo))
    mats.append(rr <= tt)
    m = np.concatenate(mats, axis=0).astype(np.float32)
    return np.concatenate([m, m], axis=1)


def _hgrn_level_tables(L):
    r = np.arange(L)
    later = np.concatenate([np.repeat((((r // b) % 2) == 1)[:, None], LANES, axis=1) for b in HG_LEVELS], axis=0)
    level = np.full((L, L), -1, np.int32)
    for lvl, b in enumerate(HG_LEVELS):
        t_later = ((r // b) % 2 == 1)[:, None]
        s_earlier = ((r // b) % 2 == 0)[None, :]
        same_parent = (r // (2 * b))[:, None] == (r // (2 * b))[None, :]
        level[t_later & s_earlier & same_parent] = lvl
    level[r, r] = len(HG_LEVELS)
    return np.where(later, 1.0, -1.0).astype(np.float32), level


def _hgrn_prompt_body(x_ref, sums_ref, sign_ref, level_ref, lb_ref, nw_ref, y_shared_ref, y_ref, st_ref):
    del y_shared_ref
    c = pl.program_id(1)
    L = CHUNK
    width = HG_HEADS * LANES
    nlev, nfine = len(HG_LEVELS), len(HG_FINE)

    @pl.when(c == 0)
    def _():
        st_ref[...] = jnp.zeros_like(st_ref)

    level = level_ref[...]
    nsub = x_ref.shape[0] // L
    heads = range(HG_HEADS)
    ch = [(r, h) for r in range(nsub) for h in heads]
    part = lambda j, r, h: x_ref[r * L:(r + 1) * L, j * width + h * LANES:j * width + (h + 1) * LANES]
    g2, kks = {}, {}
    for r, h in ch:
        sl = slice(h * LANES, (h + 1) * LANES)
        logf, kks[r, h] = _hgrn_decay(part(1, r, h), lb_ref[0:1, sl], lb_ref[1:2, sl], lb_ref[2:3, sl])
        g2[r, h] = logf * LOG2_E
    es = {}
    for r in range(nsub):
        for pair in range(HG_HEADS // 2):
            parts = [_split_bf16(g2[r, 2 * pair + k]) for k in range(2)]
            w = jnp.concatenate([jnp.concatenate([parts[0][0], parts[1][0]], axis=1),
                                 jnp.concatenate([parts[0][1], parts[1][1]], axis=1)], axis=0)
            e2 = jnp.dot(sums_ref[...], w, preferred_element_type=F32)
            es[r, 2 * pair], es[r, 2 * pair + 1] = e2[:, :LANES], e2[:, LANES:]
    qs = {k: _silu(part(0, *k)) for k in ch}
    bcs = {k: es[k][nfine * L:(nfine + 1) * L] for k in ch}
    atts = {k: jnp.where(level == nlev, jnp.sum(qs[k] * kks[k], axis=1, keepdims=True), 0.0) for k in ch}
    for lvl, b in enumerate(HG_LEVELS):
        for k in ch:
            q, kk, bc = qs[k], kks[k], bcs[k]
            if lvl < nfine:
                later = sign_ref[lvl * L:(lvl + 1) * L, :] > 0.0
                xk = jnp.exp2(es[k][lvl * L:(lvl + 1) * L]) * jnp.where(later, q, kk)
            else:
                blocks = []
                for p in range(0, L, 2 * b):
                    mid = bc[p + b - 1:p + b, :]
                    blocks.append(kk[p:p + b] * jnp.exp2(mid - bc[p:p + b]))
                    blocks.append(q[p + b:p + 2 * b] * jnp.exp2(bc[p + b:p + 2 * b] - mid))
                xk = jnp.concatenate(blocks, axis=0)
            xk = xk.astype(BF16)
            atts[k] = jnp.where(level == lvl, lax.dot_general(xk, xk, NT_DIMS, preferred_element_type=F32), atts[k])
    qds = {k: (qs[k] * jnp.exp2(bcs[k])).astype(BF16) for k in ch}
    kdts = {k: (kks[k] * jnp.exp2(bcs[k][L - 1:L, :] - bcs[k])).T.astype(BF16) for k in ch}
    dcols = {k: jnp.broadcast_to(jnp.exp2(bcs[k][L - 1:L, :]), (L, LANES)).T for k in ch}
    ivbs = {k: part(2, *k).astype(BF16) for k in ch}
    sogs = {k: _silu(part(3, *k)) for k in ch}
    for r in range(nsub):
        for h in heads:
            sl = slice(h * LANES, (h + 1) * LANES)
            st = st_ref[sl, :]
            o = jnp.dot(jnp.concatenate([atts[r, h].astype(BF16), qds[r, h]], axis=1),
                        jnp.concatenate([ivbs[r, h], st.astype(BF16)], axis=0), preferred_element_type=F32)
            st_ref[sl, :] = st * dcols[r, h] + jnp.dot(kdts[r, h], ivbs[r, h], preferred_element_type=F32)
            y_ref[r * L:(r + 1) * L, sl] = (_rms(o) * nw_ref[:, sl] * sogs[r, h]).astype(y_ref.dtype)


def _hgrn_prompt(xh, sums, sign, level, lb_rows, nw, y_shared, layer, batch, seq, nsub):
    rows = nsub * CHUNK
    nc = seq // rows
    width = HG_HEADS * LANES
    full = lambda a: pl.BlockSpec(a.shape, lambda b, c: (0,) * a.ndim)
    return pl.pallas_call(
        _hgrn_prompt_body,
        out_shape=[jax.ShapeDtypeStruct(y_shared.shape, y_shared.dtype),
                   jax.ShapeDtypeStruct((batch, width, LANES), F32)],
        grid=(batch, nc),
        in_specs=[pl.BlockSpec((rows, 4 * width), lambda b, c: (b * nc + c, 0)),
                  full(sums), full(sign), full(level), _layer_block(lb_rows, layer), _layer_block(nw, layer),
                  pl.BlockSpec(memory_space=pl.ANY)],
        out_specs=[pl.BlockSpec((rows, width), lambda b, c: (b * nc + c, 2)),
                   pl.BlockSpec((None, width, LANES), lambda b, c: (b, 0, 0))],
        input_output_aliases={6: 0},
        compiler_params=_params(("arbitrary", "arbitrary"), 16 * 1024 * 1024),
        name="hgrn_prompt",
    )(xh, sums, sign, level, lb_rows, nw, y_shared)


def _hi_lo_rows(x_row, row_idx):
    xb = jnp.broadcast_to(x_row, row_idx.shape)
    hi = xb.astype(BF16).astype(F32)
    return jnp.where(row_idx == 0, hi, jnp.where(row_idx == 1, xb - hi, 0.0)).astype(BF16)


def _sample_attn_hgrn_body(q8_ref, kr_ref, vr_ref, kvc_ref, kt_ref, vt_ref, sink_ref,
                           xh_ref, s0_ref, lb_ref, hnw_ref, kt_stack_ref, vt_stack_ref, s_stack_ref,
                           oa_ref, ktn_ref, vtn_ref, yh_ref, sn_ref, *, n_prev_valid):
    del kt_stack_ref, vt_stack_ref, s_stack_ref
    nb = q8_ref.shape[0]
    width = HG_HEADS * LANES
    grp = ATT_HEADS // ATT_KV_HEADS
    row8 = lax.broadcasted_iota(jnp.int32, (SUBLANES, LANES), 0)
    lane8 = lax.broadcasted_iota(jnp.int32, (SUBLANES, LANES), 1)
    hrow = lax.broadcasted_iota(jnp.int32, (ATT_HEADS, 1), 0)
    slope = _alibi_slope_col(hrow)
    jj = lax.broadcasted_iota(jnp.int32, (ATT_HEADS, WINDOW), 1)
    cache_dist = (WINDOW - jj).astype(F32)
    cache_ok = jj >= WINDOW - n_prev_valid
    kv_half = (lane8 // HEAD_DIM) == (row8 // grp)
    last_lane = lax.broadcasted_iota(jnp.int32, (LANES, WINDOW), 1) == WINDOW - 1
    rowx = lax.broadcasted_iota(jnp.int32, (SUBLANES, width), 0)
    lanex = lax.broadcasted_iota(jnp.int32, (SUBLANES, width), 1)
    ones_rows01 = jnp.where(row8 < 2, 1.0, 0.0).astype(BF16)
    kvw = ATT_KV_HEADS * HEAD_DIM

    smp = range(nb)
    sink = sink_ref[:, 0:1]
    head_of_lane = lanex // LANES

    s_c = [jnp.dot(q8_ref[i].astype(BF16), kt_ref[i].astype(BF16), preferred_element_type=F32) * (HEAD_DIM ** -0.5)
           for i in smp]
    s_c = [jnp.where(cache_ok, s_c[i] - slope * cache_dist, -jnp.inf) for i in smp]
    s_n = [jnp.sum(q8_ref[i] * kr_ref[i], axis=1, keepdims=True) * (HEAD_DIM ** -0.5) for i in smp]
    m = [jnp.maximum(jnp.maximum(jnp.max(s_c[i], axis=1, keepdims=True), s_n[i]), sink) for i in smp]
    p_c = [jnp.exp(s_c[i] - m[i]) for i in smp]
    p_n = [jnp.exp(s_n[i] - m[i]) for i in smp]
    den = [jnp.sum(p_c[i], axis=1, keepdims=True) + p_n[i] + jnp.exp(sink - m[i]) for i in smp]
    o = [lax.dot_general((p_c[i] / den[i]).astype(BF16), vt_ref[i].astype(BF16), NT_DIMS,
                         preferred_element_type=F32) + (p_n[i] / den[i]) * vr_ref[i] for i in smp]
    for i in smp:
        oa_ref[i] = jnp.where(kv_half, o[i], 0.0)
        ktn_ref[i] = jnp.where(last_lane, kvc_ref[0:kvw, i:i + 1], pltpu.roll(kt_ref[i], WINDOW - 1, 1))
        vtn_ref[i] = jnp.where(last_lane, kvc_ref[kvw:2 * kvw, i:i + 1], pltpu.roll(vt_ref[i], WINDOW - 1, 1))

    q = [_silu(xh_ref[i][:, :width]) for i in smp]
    dk = [_hgrn_decay(xh_ref[i][:, width:2 * width], lb_ref[0:1, :], lb_ref[1:2, :], lb_ref[2:3, :]) for i in smp]
    f = [jnp.exp(dk[i][0]) for i in smp]
    kk = [dk[i][1] for i in smp]
    iv = [xh_ref[i][:, 2 * width:3 * width] for i in smp]
    lhs_q = [jnp.where(rowx == head_of_lane, jnp.broadcast_to(q[i] * f[i], (SUBLANES, width)), 0.0).astype(BF16)
             for i in smp]
    qs = [jnp.dot(lhs_q[i], s0_ref[i].astype(BF16), preferred_element_type=F32) for i in smp]
    for i in smp:
        qkk = q[i] * kk[i]
        o_parts = []
        for h in range(HG_HEADS):
            sl = slice(h * LANES, (h + 1) * LANES)
            att = jnp.sum(qkk[:, sl], axis=1, keepdims=True)
            o_parts.append(_rms(att * iv[i][:, sl] + qs[i][h:h + 1, :]))
        yh_ref[i] = jnp.concatenate(o_parts, axis=1) * hnw_ref[...] * _silu(xh_ref[i][:, 3 * width:])
    i8 = [jnp.concatenate([iv[i][:, h * LANES:(h + 1) * LANES] for h in range(HG_HEADS)]
                          + [jnp.zeros((SUBLANES - HG_HEADS, LANES), F32)], axis=0).astype(BF16) for i in smp]
    lhs_k = [jnp.where(rowx == head_of_lane, jnp.broadcast_to(kk[i], (SUBLANES, width)), 0.0).astype(BF16)
             for i in smp]
    upd_s = [lax.dot_general(lhs_k[i], i8[i], TN_DIMS, preferred_element_type=F32) for i in smp]
    f_col = [lax.dot_general(_hi_lo_rows(f[i], rowx), ones_rows01, TN_DIMS, preferred_element_type=F32) for i in smp]
    for i in smp:
        sn_ref[i] = s0_ref[i] * f_col[i] + upd_s[i]


def _sample_attn_hgrn(q8, kr, vr, kvc, kt, vt, sink8, xh, s0, lb_rows, hnw, kt_stack, vt_stack, s_stack, layer, nb):
    n = q8.shape[0]
    per = lambda a: pl.BlockSpec((nb,) + a.shape[1:], lambda i: (i,) + (0,) * (a.ndim - 1))
    lay = lambda a: pl.BlockSpec((None, nb) + a.shape[2:], lambda i: (layer, i) + (0,) * (a.ndim - 2))
    hbm = pl.BlockSpec(memory_space=pl.ANY)
    width = HG_HEADS * LANES
    out_shape = [jax.ShapeDtypeStruct((n, ATT_HEADS, LANES), F32),
                 jax.ShapeDtypeStruct(kt_stack.shape, F32),
                 jax.ShapeDtypeStruct(vt_stack.shape, F32),
                 jax.ShapeDtypeStruct((n, 1, width), F32),
                 jax.ShapeDtypeStruct(s_stack.shape, F32)]
    return pl.pallas_call(
        functools.partial(_sample_attn_hgrn_body, n_prev_valid=min(WINDOW, PAST_LEN)),
        out_shape=out_shape,
        grid=(n // nb,),
        in_specs=[per(q8), per(kr), per(vr), pl.BlockSpec((None,) + kvc.shape[1:], lambda i: (i, 0, 0)),
                  lay(kt), lay(vt), _layer_block(sink8, layer), per(xh), lay(s0),
                  _layer_block(lb_rows, layer), _layer_block(hnw, layer), hbm, hbm, hbm],
        out_specs=[per(q8), lay(kt_stack), lay(vt_stack),
                   pl.BlockSpec((nb, 1, width), lambda i: (i, 0, 0)), lay(s_stack)],
        input_output_aliases={11: 1, 12: 2, 13: 4},
        compiler_params=_params(("parallel",), 4 * nb * (2 * kt.shape[2] + s0.shape[2]) * LANES * 4),
        name="sample_attn_hgrn",
    )(q8, kr, vr, kvc, kt, vt, sink8, xh, s0, lb_rows, hnw, kt_stack, vt_stack, s_stack)


def _ssd_sample_body(col_ref, cst_ref, h0_ref, cw_ref, cb_ref, dtb_ref, a_ref, dskip_ref, nw_ref, stack_ref,
                     ym_ref, hn_ref, xs_s, b_s, c_s, dt_s, da_s, yrow_s, y_s):
    del stack_ref
    h = pl.program_id(0)
    inner = M_HEADS * M_HEADDIM
    conv_dim = inner + 2 * M_GROUPS * M_STATE
    kvw2 = 2 * ATT_KV_HEADS * HEAD_DIM
    z0, x0, d0 = kvw2, kvw2 + inner, kvw2 + inner + conv_dim
    ns = col_ref.shape[1]

    @pl.when(h == 0)
    def _():
        conv = col_ref[x0:d0, :] * cw_ref[CONV_W - 1] + cb_ref[...]
        for w in range(CONV_W - 1):
            conv = conv + cst_ref[w] * cw_ref[w]
        act = _silu(conv)
        xs_s[...] = act[:inner]
        b_s[...] = act[inner:inner + M_GROUPS * M_STATE]
        c_s[...] = act[inner + M_GROUPS * M_STATE:]
        dt = _softplus(col_ref[d0:d0 + M_HEADS, :] + dtb_ref[...])
        da = jnp.exp(dt * a_ref[...])
        for hh in range(M_HEADS):
            dt_s[hh] = jnp.broadcast_to(dt[hh:hh + 1, :], (SUBLANES, ns))
            da_s[hh] = jnp.broadcast_to(da[hh:hh + 1, :], (SUBLANES, ns))

    hpg = M_HEADS // M_GROUPS
    x_h = xs_s[pl.ds(pl.multiple_of(h * M_HEADDIM, M_HEADDIM), M_HEADDIM), :]
    g0 = pl.multiple_of((h // hpg) * M_STATE, M_STATE)
    b_g = b_s[pl.ds(g0, M_STATE), :]
    c_g = c_s[pl.ds(g0, M_STATE), :]
    dt = dt_s[h][0:1, :]
    da = da_s[h][0:1, :]
    dtx = x_h * dt
    for p in range(M_HEADDIM):
        s_p = h0_ref[p * M_STATE:(p + 1) * M_STATE, :]
        yrow_s[p:p + 1, :] = jnp.sum(c_g * s_p, axis=0, keepdims=True)
        hn_ref[p * M_STATE:(p + 1) * M_STATE, :] = da * s_p + b_g * dtx[p:p + 1, :]
    cb = jnp.sum(c_g * b_g, axis=0, keepdims=True)
    y_s[pl.ds(pl.multiple_of(h * M_HEADDIM, M_HEADDIM), M_HEADDIM), :] = (
        da * yrow_s[...] + (dt * cb + dskip_ref[h][0:1, :]) * x_h)

    @pl.when(h == pl.num_programs(0) - 1)
    def _():
        y = y_s[...] * _silu(col_ref[z0:x0, :])
        y = y * lax.rsqrt(jnp.mean(y * y, axis=0, keepdims=True) + RMS_EPS) * nw_ref[...]
        for j in range(inner // LANES):
            ym_ref[:, j * LANES:(j + 1) * LANES] = y[j * LANES:(j + 1) * LANES, :].T


def _ssd_sample(col, cst, h0, cw, cb, dtb, a_neg, dskip, nw, stack, layer):
    ns = col.shape[1]
    inner = M_HEADS * M_HEADDIM
    state_block = pl.BlockSpec((None, None) + h0.shape[2:], lambda h: (layer, h, 0, 0))
    return pl.pallas_call(
        _ssd_sample_body,
        out_shape=[jax.ShapeDtypeStruct((ns, inner), F32), jax.ShapeDtypeStruct(stack.shape, F32)],
        grid=(M_HEADS,),
        in_specs=[pl.BlockSpec(col.shape, lambda h: (0, 0)), _layer_block(cst, layer), state_block]
        + [_layer_block(a, layer) for a in (cw, cb, dtb, a_neg, dskip, nw)] + [pl.BlockSpec(memory_space=pl.ANY)],
        out_specs=[pl.BlockSpec((ns, inner), lambda h: (0, 0)), state_block],
        scratch_shapes=[pltpu.VMEM((inner, ns), F32), pltpu.VMEM((M_GROUPS * M_STATE, ns), F32),
                        pltpu.VMEM((M_GROUPS * M_STATE, ns), F32), pltpu.VMEM((M_HEADS, SUBLANES, ns), F32),
                        pltpu.VMEM((M_HEADS, SUBLANES, ns), F32), pltpu.VMEM((M_HEADDIM, ns), F32),
                        pltpu.VMEM((inner, ns), F32)],
        input_output_aliases={9: 1},
        compiler_params=_params(("arbitrary",), 24 * 1024 * 1024),
        name="ssd_sample",
    )(col, cst, h0, cw, cb, dtb, a_neg, dskip, nw, stack)


def _row_tile(rows):
    for tm in (1024, 512, 256, 128):
        if rows % tm == 0:
            return tm
    return rows


def kernel(x_prompt, x_sample, cache_swa_k, cache_swa_v, state_conv, state_ssm, state_hgrn, ln1_g, ln1_b, ffn1_wg, ffn1_wu, ffn1_wd, w_in, b_in, att_sinks, conv_w, conv_b, dt_bias, a_log, d_skip, ssm_norm_w, hg_lb_logits, hg_norm_w, w_br_att, w_br_ssm, w_br_hg, w_out, ln2_g, ln2_b, ffn2_wg, ffn2_wu, ffn2_wd, ln3_g, ln3_b):
    bp, seq, d = x_prompt.shape
    ns = x_sample.shape[0]
    depth = w_in.shape[0]
    assert seq % CHUNK == 0 and x_sample.shape[1] == 1
    qw = ATT_HEADS * HEAD_DIM
    kvw = ATT_KV_HEADS * HEAD_DIM
    inner = M_HEADS * M_HEADDIM
    conv_dim = inner + 2 * M_GROUPS * M_STATE
    width = HG_HEADS * LANES
    hpg = M_HEADS // M_GROUPS

    bf = lambda a: a.astype(BF16)
    row3 = lambda a: a.reshape(depth, 1, a.shape[-1])
    o_z = qw + 2 * kvw
    o_dt = o_z + inner + conv_dim
    o_h = o_dt + M_HEADS
    o_g = o_h + 4 * width
    w_in_t = jnp.swapaxes(w_in, 1, 2)
    dt_pad = jnp.pad(w_in_t[:, o_dt:o_h, :], ((0, 0), (0, LANES - M_HEADS), (0, 0)))
    w_rows = bf(jnp.concatenate([w_in_t[:, :o_dt, :], dt_pad, w_in_t[:, o_h:o_g, :]], axis=1))
    b_flat = jnp.concatenate([b_in[:, :o_dt], jnp.pad(b_in[:, o_dt:o_h], ((0, 0), (0, LANES - M_HEADS))),
                              b_in[:, o_h:o_g]], axis=1)
    b_rows = row3(b_flat)
    col_rows = (qw, o_dt + LANES)
    b_cols = b_flat[:, col_rows[0]:col_rows[1], None]
    in_widths = (qw + 2 * kvw, inner + conv_dim + LANES, 4 * width)
    ffn1 = (bf(ffn1_wg), bf(ffn1_wu), bf(ffn1_wd), row3(ln1_g), row3(ln1_b))
    ffn2 = (bf(ffn2_wg), bf(ffn2_wu), bf(ffn2_wd), row3(ln3_g), row3(ln3_b))
    w_merge = bf(jnp.concatenate([w_in_t[:, o_g:, :], w_br_att, w_br_ssm, w_br_hg, w_out], axis=1))
    merge_w = (w_merge, row3(b_in[:, o_g:]), row3(ln2_g), row3(ln2_b))

    a_neg = -jnp.exp(a_log.astype(F32))
    pad_h = lambda a: row3(jnp.pad(a, ((0, 0), (0, LANES - M_HEADS))))
    ssd_rows = (conv_w, row3(conv_b), pad_h(dt_bias), pad_h(a_neg), row3(jnp.repeat(d_skip, M_HEADDIM, axis=1)),
                row3(ssm_norm_w))
    lb_all = jnp.cumsum(jax.nn.softmax(hg_lb_logits.astype(F32), axis=0), axis=0)
    lb_all = lb_all - lb_all[0]
    lb_rows = jnp.stack([jnp.log(lb_all), jnp.log1p(-lb_all), 1.0 - lb_all], axis=1)
    lb_rows = jnp.pad(lb_rows, ((0, 0), (0, SUBLANES - 3), (0, 0)))
    hnw = row3(jnp.tile(hg_norm_w, (1, HG_HEADS)))
    sums = jnp.asarray(_hgrn_sum_matrix(CHUNK), BF16)
    sign_np, level_np = _hgrn_level_tables(CHUNK)
    hg_sign, hg_level = jnp.asarray(sign_np), jnp.asarray(level_np)
    att_bias = jnp.asarray(_attn_bias(CHUNK))
    tril = jnp.asarray(np.tril(np.ones((CHUNK, CHUNK), np.float32)), BF16)
    sink8 = jnp.broadcast_to(att_sinks[:, :, None], (depth, ATT_HEADS, LANES))

    tm_p = _row_tile(bp * seq)
    tm_s = _row_tile(ns)
    nb_s = max(k for k in (2 * SUBLANES, SUBLANES, 1) if ns % k == 0)
    lanes_b = lambda a: jnp.broadcast_to(a[..., None], a.shape + (ns,))
    kt_in = cache_swa_k.transpose(0, 1, 3, 4, 2).reshape(depth, ns, kvw, WINDOW)
    vt_in = cache_swa_v.transpose(0, 1, 3, 4, 2).reshape(depth, ns, kvw, WINDOW)
    ssm_in = state_ssm.transpose(0, 2, 3, 4, 1).reshape(depth, M_HEADS, M_HEADDIM * M_STATE, ns)
    conv_in = state_conv.transpose(0, 2, 3, 1)
    hg_in = state_hgrn.reshape(depth, ns, width, LANES)
    kt_out, vt_out = jnp.zeros(kt_in.shape, F32), jnp.zeros(vt_in.shape, F32)
    ssm_out, hg_out = jnp.zeros(ssm_in.shape, F32), jnp.zeros(hg_in.shape, F32)
    cw_b, cb_b = lanes_b(conv_w), lanes_b(conv_b)
    dtb_b, a_b, nw_b = lanes_b(dt_bias), lanes_b(a_neg), lanes_b(ssm_norm_w)
    dskip_b = jnp.broadcast_to(d_skip[:, :, None, None], (depth, M_HEADS, SUBLANES, ns))
    att_sub = max(k for k in (4, 2, 1) if (seq // CHUNK) % k == 0)
    ssd_sub = max(k for k in (4, 2, 1) if (seq // CHUNK) % k == 0)
    scan_sub = max(k for k in (2, 1) if (seq // CHUNK) % k == 0)

    xp = x_prompt.reshape(bp * seq, d)
    xs = x_sample.reshape(ns, d)
    p_states, s_conv = [], []
    for l in range(depth):
        xp = _ffn_ln(xp, *ffn1, l, tm_p)
        qkv, zxd, xh = _inproj(xp, w_rows, b_rows, l, tm_p, in_widths)
        y3 = _attn_prompt(qkv, att_sinks, att_bias, l, bp, seq, att_sub)
        y3, st_ssm = _ssd_prompt(zxd, *ssd_rows, tril, y3, l, bp, seq, ssd_sub)
        y3, st_hg = _hgrn_prompt(xh, sums, hg_sign, hg_level, lb_rows, hnw, y3, l, bp, seq, scan_sub)
        xp = _merge(xp, y3, *merge_w, l, tm_p)
        xp = _ffn_ln(xp, *ffn2, l, tm_p)
        qkv3 = qkv.reshape(bp, seq, qw + 2 * kvw)
        p_k = qkv3[:, seq - WINDOW:, qw:qw + kvw].reshape(bp, WINDOW, ATT_KV_HEADS, HEAD_DIM)
        p_v = qkv3[:, seq - WINDOW:, qw + kvw:].reshape(bp, WINDOW, ATT_KV_HEADS, HEAD_DIM)
        p_conv = zxd.reshape(bp, seq, -1)[:, seq - (CONV_W - 1):, inner:inner + conv_dim]
        st6 = st_ssm.reshape(bp, M_GROUPS, M_STATE, M_GROUPS, hpg, M_HEADDIM)
        p_ssm = jnp.stack([st6[:, g, :, g] for g in range(M_GROUPS)], axis=1)
        p_ssm = p_ssm.transpose(0, 1, 3, 4, 2).reshape(bp, M_HEADS, M_HEADDIM, M_STATE)
        p_hg = st_hg.reshape(bp, HG_HEADS, LANES, LANES)
        p_states.append((p_k, p_v, p_conv, p_ssm, p_hg))

        xs = _ffn_ln(xs, *ffn1, l, tm_s)
        qkv_s, zx_s, xh_s, col_s = _inproj_sample(xs, w_rows, b_rows, b_cols, l, in_widths, col_rows)
        q4 = qkv_s[:, :qw].reshape(ns, ATT_KV_HEADS, ATT_HEADS // ATT_KV_HEADS, HEAD_DIM)
        zq = jnp.zeros_like(q4[:, 0])
        q8 = jnp.concatenate([jnp.concatenate([q4[:, 0], zq], axis=-1),
                              jnp.concatenate([zq, q4[:, 1]], axis=-1)], axis=1)
        kv_cols = col_s[:2 * kvw].reshape(2 * kvw, ns // nb_s, nb_s).transpose(1, 0, 2)
        oa, kt_out, vt_out, yh_s, hg_out = _sample_attn_hgrn(
            q8, qkv_s[:, None, qw:qw + kvw], qkv_s[:, None, qw + kvw:], kv_cols, kt_in, vt_in, sink8,
            xh_s[:, None, :], hg_in, lb_rows, hnw, kt_out, vt_out, hg_out, l, nb_s)
        ym_s, ssm_out = _ssd_sample(col_s, conv_in, ssm_in, cw_b, cb_b, dtb_b, a_b, dskip_b, nw_b, ssm_out, l)
        grp = ATT_HEADS // ATT_KV_HEADS
        y3_s = jnp.concatenate([oa[:, :grp, :HEAD_DIM].reshape(ns, grp * HEAD_DIM),
                                oa[:, grp:, HEAD_DIM:].reshape(ns, grp * HEAD_DIM),
                                ym_s, yh_s.reshape(ns, width)], axis=1)
        xs = _merge(xs, y3_s, *merge_w, l, tm_s)
        xs = _ffn_ln(xs, *ffn2, l, tm_s)
        s_conv.append(jnp.concatenate([state_conv[l][:, 1:], zx_s[:, None, inner:inner + conv_dim]], axis=1))

    outs_p = [jnp.stack(t) for t in zip(*p_states)]
    unwind = lambda t: t.reshape(depth, ns, ATT_KV_HEADS, HEAD_DIM, WINDOW).transpose(0, 1, 4, 2, 3)
    s_ssm = ssm_out.reshape(depth, M_HEADS, M_HEADDIM, M_STATE, ns).transpose(0, 4, 1, 2, 3)
    outs_s = [unwind(kt_out), unwind(vt_out), jnp.stack(s_conv), s_ssm,
              hg_out.reshape(depth, ns, HG_HEADS, LANES, LANES)]
    return (xp.reshape(bp, seq, d), xs.reshape(ns, 1, d), *outs_p, *outs_s)
```

```python
import functools

import numpy as np
import jax
import jax.numpy as jnp
from jax import lax
from jax.experimental import pallas as pl
from jax.experimental.pallas import tpu as pltpu

F32 = jnp.float32
BF16 = jnp.bfloat16

ATT_HEADS = 8
ATT_KV_HEADS = 2
HEAD_DIM = 64
WINDOW = 128
PAST_LEN = 8192
M_HEADS = 8
M_HEADDIM = 64
M_GROUPS = 2
M_STATE = 64
CONV_W = 4
HG_HEADS = 4
DEPTH = 4
ALPHA = (2.0 * DEPTH) ** 0.25
LN_EPS = 1e-5
RMS_EPS = 1e-6
LOG2_E = 1.4426950408889634
CHUNK = 128

V7X_VMEM_BYTES = 64 * 1024 * 1024
LANES = 128
SUBLANES = 8
MXU_WIDTH = 256

NT_DIMS = (((1,), (1,)), ((), ()))
TN_DIMS = (((0,), (0,)), ((), ()))


def _vmem_limit(nbytes):
    return int(min(V7X_VMEM_BYTES - 8 * 1024 * 1024, nbytes + 16 * 1024 * 1024))


def _params(semantics, vmem_bytes):
    return pltpu.CompilerParams(dimension_semantics=semantics, vmem_limit_bytes=_vmem_limit(vmem_bytes))


def _layer_block(a, layer):
    zeros = (0,) * (a.ndim - 1)
    return pl.BlockSpec((None,) + a.shape[1:], lambda *_: (layer,) + zeros)


def _resident(block_shape, index_map):
    return pl.BlockSpec(block_shape, index_map, pipeline_mode=pl.Buffered(1))


def _silu(x):
    return x * jax.nn.sigmoid(x)


def _softplus(x):
    return jnp.maximum(x, 0.0) + jnp.log(1.0 + jnp.exp(-jnp.abs(x)))


def _layernorm(y, g, b):
    mu = jnp.mean(y, axis=-1, keepdims=True)
    d = y - mu
    var = jnp.mean(d * d, axis=-1, keepdims=True)
    return d * lax.rsqrt(var + LN_EPS) * g + b


def _rms(y):
    return y * lax.rsqrt(jnp.mean(y * y, axis=-1, keepdims=True) + RMS_EPS)


def _split_bf16(x):
    hi = x.astype(BF16)
    lo = (x - hi.astype(F32)).astype(BF16)
    return hi, lo


def _col_chunks(width, step):
    return [(c, min(step, width - c)) for c in range(0, width, step)]


def _ffn_ln_body(x_ref, wg_ref, wu_ref, wd_ref, g_ref, b_ref, o_ref, *, ff_chunk):
    x = x_ref[...]
    xb = x.astype(BF16)
    acc = None
    for c0, cw in _col_chunks(wg_ref.shape[1], ff_chunk):
        gate = jnp.dot(xb, wg_ref[:, c0:c0 + cw], preferred_element_type=F32)
        up = jnp.dot(xb, wu_ref[:, c0:c0 + cw], preferred_element_type=F32)
        hid = (_silu(gate) * up).astype(BF16)
        part = jnp.dot(hid, wd_ref[c0:c0 + cw, :], preferred_element_type=F32)
        acc = part if acc is None else acc + part
    o_ref[...] = _layernorm(ALPHA * x + 0.5 * acc, g_ref[...], b_ref[...])


def _ffn_ln(x, wg, wu, wd, g, b, layer, tm):
    rows, d = x.shape
    f = wg.shape[2]
    wspec = lambda shape: _resident((None,) + shape, lambda i: (layer, 0, 0))
    vmem = 3 * d * f * 2 + 4 * tm * d * 4 + 4 * tm * MXU_WIDTH * 4
    return pl.pallas_call(
        functools.partial(_ffn_ln_body, ff_chunk=MXU_WIDTH),
        out_shape=jax.ShapeDtypeStruct((rows, d), F32),
        grid=(rows // tm,),
        in_specs=[pl.BlockSpec((tm, d), lambda i: (i, 0)),
                  wspec((d, f)), wspec((d, f)), wspec((f, d)),
                  wspec((1, d)), wspec((1, d))],
        out_specs=pl.BlockSpec((tm, d), lambda i: (i, 0)),
        compiler_params=_params(("parallel",), vmem),
        name="ffn_ln",
    )(x, wg, wu, wd, g, b)


def _hgrn_decay(hf, log_lb, log1m_lb, one_m_lb):
    e_neg = jnp.exp(-jnp.abs(hf))
    one_p = 1.0 + e_neg
    log_sig = jnp.minimum(hf, 0.0) - jnp.log(one_p)
    inv = 1.0 / one_p
    kk = one_m_lb * jnp.where(hf >= 0.0, e_neg * inv, inv)
    b = log1m_lb + log_sig
    logf = jnp.maximum(log_lb, b) + jnp.log(1.0 + jnp.exp(-jnp.abs(log_lb - b)))
    return logf, kk


def _project_rows(xb, wt_ref, b_ref, o_refs):
    base = 0
    for o_ref in o_refs:
        for c0, cw in _col_chunks(o_ref.shape[1], 2 * MXU_WIDTH):
            w = wt_ref[base + c0:base + c0 + cw, :]
            o_ref[:, c0:c0 + cw] = (lax.dot_general(xb, w, NT_DIMS, preferred_element_type=F32)
                                    + b_ref[:, base + c0:base + c0 + cw])
        base += o_ref.shape[1]


def _inproj_body(x_ref, wt_ref, b_ref, *o_refs):
    _project_rows(x_ref[...].astype(BF16), wt_ref, b_ref, o_refs)


def _inproj(x, wt, b, layer, tm, widths):
    rows, d = x.shape
    n = wt.shape[1]
    assert n == sum(widths)
    vmem = d * n * 2 + 2 * tm * d * 4 + 2 * tm * n * 4
    return pl.pallas_call(
        _inproj_body,
        out_shape=[jax.ShapeDtypeStruct((rows, wd), F32) for wd in widths],
        grid=(rows // tm,),
        in_specs=[pl.BlockSpec((tm, d), lambda i: (i, 0)),
                  _resident((None, n, d), lambda i: (layer, 0, 0)),
                  _resident((None, 1, n), lambda i: (layer, 0, 0))],
        out_specs=[pl.BlockSpec((tm, wd), lambda i: (i, 0)) for wd in widths],
        compiler_params=_params(("parallel",), vmem),
        name="inproj",
    )(x, wt, b)


def _inproj_sample_body(x_ref, wt_ref, b_ref, bt_ref, *o_refs, col_rows):
    xb = x_ref[...].astype(BF16)
    _project_rows(xb, wt_ref, b_ref, o_refs[:-1])
    r0, r1 = col_rows
    o_refs[-1][...] = lax.dot_general(wt_ref[r0:r1, :], xb, NT_DIMS, preferred_element_type=F32) + bt_ref[...]


def _inproj_sample(x, wt, b, bt, layer, widths, col_rows):
    rows, d = x.shape
    n, nt = wt.shape[1], col_rows[1] - col_rows[0]
    vmem = d * n * 2 + 2 * rows * d * 4 + 2 * rows * (n + nt) * 4
    return pl.pallas_call(
        functools.partial(_inproj_sample_body, col_rows=col_rows),
        out_shape=[jax.ShapeDtypeStruct((rows, wd), F32) for wd in widths]
        + [jax.ShapeDtypeStruct((nt, rows), F32)],
        grid=(1,),
        in_specs=[pl.BlockSpec((rows, d), lambda i: (0, 0)),
                  pl.BlockSpec((None, n, d), lambda i: (layer, 0, 0)),
                  pl.BlockSpec((None, 1, n), lambda i: (layer, 0, 0)),
                  pl.BlockSpec((None, nt, 1), lambda i: (layer, 0, 0))],
        out_specs=[pl.BlockSpec((rows, wd), lambda i: (0, 0)) for wd in widths]
        + [pl.BlockSpec((nt, rows), lambda i: (0, 0))],
        compiler_params=_params(("arbitrary",), vmem),
        name="inproj_sample",
    )(x, wt, b, bt)


N_BRANCH = 3


def _merge_body(x_ref, y_ref, w_ref, bgate_ref, g_ref, b_ref, o_ref):
    x = x_ref[...]
    xb = x.astype(BF16)
    d = x.shape[1]
    wb = y_ref.shape[1] // N_BRANCH
    br0, out0 = N_BRANCH * d, N_BRANCH * d + N_BRANCH * wb
    ybs = [y_ref[:, k * wb:(k + 1) * wb].astype(BF16) for k in range(N_BRANCH)]
    merged = []
    for c0, cw in _col_chunks(d, MXU_WIDTH):
        m = None
        for k in range(N_BRANCH):
            col = k * d + c0
            gate = jax.nn.sigmoid(lax.dot_general(xb, w_ref[col:col + cw, :], NT_DIMS,
                                                  preferred_element_type=F32) + bgate_ref[:, col:col + cw])
            term = gate * jnp.dot(ybs[k], w_ref[br0 + k * wb:br0 + (k + 1) * wb, c0:c0 + cw],
                                  preferred_element_type=F32)
            m = term if m is None else m + term
        merged.append(m.astype(BF16))
    y = jnp.dot(jnp.concatenate(merged, axis=1), w_ref[out0:out0 + d, :], preferred_element_type=F32)
    o_ref[...] = _layernorm(ALPHA * x + y, g_ref[...], b_ref[...])


def _merge(x, y, w, bgate, g, b, layer, tm):
    rows, d = x.shape
    wspec = lambda a: _resident((None,) + a.shape[1:], lambda i: (layer, 0, 0))
    row = lambda width: pl.BlockSpec((tm, width), lambda i: (i, 0))
    vmem = w.shape[1] * d * 2 + 4 * tm * d * 4 + 2 * tm * y.shape[1] * y.dtype.itemsize
    return pl.pallas_call(
        _merge_body,
        out_shape=jax.ShapeDtypeStruct((rows, d), F32),
        grid=(rows // tm,),
        in_specs=[row(d), row(y.shape[1]), wspec(w), wspec(bgate), wspec(g), wspec(b)],
        out_specs=row(d),
        compiler_params=_params(("parallel",), vmem),
        name="merge_out_ln",
    )(x, y, w, bgate, g, b)


def _alibi_slope_col(head_rows):
    slope = jnp.zeros(head_rows.shape, F32)
    for h in range(ATT_HEADS):
        slope = jnp.where(head_rows == h, 2.0 ** (-8.0 * (h + 1) / ATT_HEADS), slope)
    return slope


def _attn_bias(blk):
    tq = np.arange(blk)[:, None]
    j = np.arange(2 * blk)[None, :]
    dist = WINDOW + tq - j
    ok = (dist >= 0) & (dist <= WINDOW)
    slopes = 2.0 ** (-8.0 * np.arange(1, ATT_HEADS + 1) / ATT_HEADS)
    out = np.empty((2, ATT_HEADS, blk, 2 * blk), np.float32)
    for has_prev in range(2):
        vis = ok & ((j >= blk) | (has_prev == 1))
        out[has_prev] = np.where(vis[None], -slopes[:, None, None] * dist[None], -np.inf)
    return out


def _attn_prompt_body(sink_ref, bias_ref, q_ref, kvc_ref, kvp_ref, o_ref, *, layer):
    blk = kvp_ref.shape[0]
    nsub = q_ref.shape[0] // blk
    grp = ATT_HEADS // ATT_KV_HEADS
    has_prev = jnp.minimum(pl.program_id(1), 1)
    lane = lax.broadcasted_iota(jnp.int32, (blk, LANES), 1)
    low = lane < HEAD_DIM

    def kv_block(i):
        return kvp_ref[...] if i < 0 else kvc_ref[i * blk:(i + 1) * blk, :]

    kcats, vcats = [], []
    for i in range(nsub):
        prev, cur = kv_block(i - 1), kv_block(i)
        kcats.append(jnp.concatenate([prev[:, :LANES], cur[:, :LANES]], axis=0).astype(BF16))
        vcats.append(jnp.concatenate([prev[:, LANES:], cur[:, LANES:]], axis=0).astype(BF16))

    def scores(i, h):
        kv = h // grp
        qcol = q_ref[i * blk:(i + 1) * blk, (h // 2) * LANES:(h // 2 + 1) * LANES] * (HEAD_DIM ** -0.5)
        if (h % 2) != kv:
            qcol = pltpu.roll(qcol, HEAD_DIM, 1)
        qh = jnp.where(low if kv == 0 else jnp.logical_not(low), qcol, 0.0).astype(BF16)
        bias = bias_ref[has_prev, h] if i == 0 else bias_ref[1, h]
        return lax.dot_general(qh, kcats[i], NT_DIMS, preferred_element_type=F32) + bias

    ih = [(i, h) for i in range(nsub) for h in range(ATT_HEADS)]
    ss = {k: scores(*k) for k in ih}
    ms = {k: jnp.maximum(jnp.max(ss[k], axis=1, keepdims=True), sink_ref[layer, k[1]]) for k in ih}
    ps = {k: jnp.exp(ss[k] - ms[k]) for k in ih}
    dens = {k: jnp.sum(ps[k], axis=1, keepdims=True) + jnp.exp(sink_ref[layer, k[1]] - ms[k]) for k in ih}
    ps = {k: (ps[k] * (1.0 / dens[k])).astype(BF16) for k in ih}
    outs = {k: jnp.dot(ps[k], vcats[k[0]], preferred_element_type=F32) for k in ih}
    for i in range(nsub):
        for c in range(ATT_HEADS // 2):
            even, odd = outs[i, 2 * c], outs[i, 2 * c + 1]
            if (2 * c) // grp == 0:
                col = jnp.where(low, even, pltpu.roll(odd, HEAD_DIM, 1))
            else:
                col = jnp.where(low, pltpu.roll(even, HEAD_DIM, 1), odd)
            o_ref[i * blk:(i + 1) * blk, c * LANES:(c + 1) * LANES] = col.astype(o_ref.dtype)
    qw = ATT_HEADS * HEAD_DIM
    o_ref[:, qw:] = jnp.zeros((o_ref.shape[0], o_ref.shape[1] - qw), o_ref.dtype)


def _attn_prompt(qkv, sinks, bias, layer, batch, seq, nsub):
    rows = nsub * CHUNK
    ns = seq // rows
    qw = ATT_HEADS * HEAD_DIM
    kvw = 2 * ATT_KV_HEADS * HEAD_DIM
    kv_col = qw // kvw
    return pl.pallas_call(
        functools.partial(_attn_prompt_body, layer=layer),
        out_shape=jax.ShapeDtypeStruct((batch * seq, N_BRANCH * qw), BF16),
        grid=(batch, ns),
        in_specs=[pl.BlockSpec(memory_space=pltpu.SMEM),
                  pl.BlockSpec(bias.shape, lambda b, n: (0,) * bias.ndim),
                  pl.BlockSpec((rows, qw), lambda b, n: (b * ns + n, 0)),
                  pl.BlockSpec((rows, kvw), lambda b, n: (b * ns + n, kv_col)),
                  pl.BlockSpec((CHUNK, kvw), lambda b, n: ((b * ns + n) * nsub - jnp.minimum(n, 1), kv_col))],
        out_specs=pl.BlockSpec((rows, N_BRANCH * qw), lambda b, n: (b * ns + n, 0)),
        compiler_params=_params(("parallel", "arbitrary"), 16 * 1024 * 1024),
        name="attn_prompt",
    )(sinks, bias, qkv, qkv, qkv)


def _expand_heads(col_of, lane_low):
    cols = []
    for j in range(M_HEADS // 2):
        cols.append(jnp.where(lane_low, col_of(2 * j), col_of(2 * j + 1)))
    return jnp.concatenate(cols, axis=1)


def _ssd_prompt_body(zxd_ref, cw_ref, cb_ref, dtb_ref, a_ref, dskip_ref, nw_ref, tril_ref, y_shared_ref,
                     y_ref, st_ref, xpad_ref):
    del y_shared_ref
    c = pl.program_id(1)
    L = CHUNK
    inner = M_HEADS * M_HEADDIM
    conv_dim = inner + 2 * M_GROUPS * M_STATE

    @pl.when(c == 0)
    def _():
        st_ref[...] = jnp.zeros_like(st_ref)
        xpad_ref[0:SUBLANES, :] = jnp.zeros((SUBLANES, conv_dim), F32)

    lane = lax.broadcasted_iota(jnp.int32, (L, LANES), 1)
    low = lane < M_STATE
    low1 = lane[0:1, :] < M_STATE
    causal = lax.broadcasted_iota(jnp.int32, (L, L), 0) >= lax.broadcasted_iota(jnp.int32, (L, L), 1)
    srow = lax.broadcasted_iota(jnp.int32, (M_GROUPS * M_STATE, inner), 0) // M_STATE
    scol = lax.broadcasted_iota(jnp.int32, (M_GROUPS * M_STATE, inner), 1) // (inner // M_GROUPS)
    hpg = M_HEADS // M_GROUPS

    subs = range(zxd_ref.shape[0] // L)
    heads = range(M_HEADS)
    xs, bm, cm, dt, a = {}, {}, {}, {}, {}
    for r in subs:
        rs = slice(r * L, (r + 1) * L)
        xbc = zxd_ref[rs, inner:inner + conv_dim]
        xpad_ref[SUBLANES:SUBLANES + L, :] = xbc
        conv = xbc * cw_ref[CONV_W - 1:CONV_W, :] + cb_ref[...]
        for w in range(CONV_W - 1):
            off = SUBLANES - (CONV_W - 1) + w
            conv = conv + xpad_ref[off:off + L, :] * cw_ref[w:w + 1, :]
        xpad_ref[0:SUBLANES, :] = xbc[L - SUBLANES:L, :]
        act = _silu(conv)
        xs[r] = act[:, :inner]
        bm[r] = act[:, inner:inner + LANES]
        cm[r] = act[:, inner + LANES:inner + 2 * LANES]
        dt[r] = _softplus(zxd_ref[rs, inner + conv_dim:] + dtb_ref[...])
        dta_hi, dta_lo = _split_bf16(dt[r] * a_ref[...])
        cum = jnp.dot(tril_ref[...], jnp.concatenate([dta_hi, dta_lo], axis=1), preferred_element_type=F32)
        a[r] = cum[:, :LANES] + cum[:, LANES:]
    a_t = {r: a[r].T for r in subs}
    dt_t = {r: dt[r].T for r in subs}
    bmb = {r: bm[r].astype(BF16) for r in subs}
    cb2 = {r: lax.dot_general(jnp.concatenate([jnp.where(low, cm[r], 0.0), jnp.where(low, 0.0, cm[r])],
                                              axis=0).astype(BF16), bmb[r], NT_DIMS, preferred_element_type=F32)
           for r in subs}
    cb_g = {(r, g): cb2[r][g * L:(g + 1) * L] for r in subs for g in range(M_GROUPS)}

    def weights(r, h):
        diff = jnp.broadcast_to(a[r][:, h:h + 1], (L, L)) - jnp.broadcast_to(a_t[r][h:h + 1, :], (L, L))
        decay = jnp.exp(jnp.where(causal, diff, -jnp.inf))
        return (cb_g[r, h // hpg] * decay * jnp.broadcast_to(dt_t[r][h:h + 1, :], (L, L))).astype(BF16)

    ws = {(r, h): weights(r, h) for r in subs for h in heads}
    xhalf = {(r, h): jnp.where(low if h % 2 == 0 else jnp.logical_not(low),
                               xs[r][:, (h // 2) * LANES:(h // 2 + 1) * LANES], 0.0).astype(BF16)
             for r in subs for h in heads}
    ycol = {(r, j): jnp.dot(jnp.concatenate([ws[r, 2 * j], ws[r, 2 * j + 1]], axis=1),
                            jnp.concatenate([xhalf[r, 2 * j], xhalf[r, 2 * j + 1]], axis=0),
                            preferred_element_type=F32) for r in subs for j in range(M_HEADS // 2)}
    y_intra = {r: jnp.concatenate([ycol[r, j] for j in range(M_HEADS // 2)], axis=1) for r in subs}
    ea = {r: jnp.exp(a[r]) for r in subs}
    ea_x = {r: _expand_heads(lambda h: jnp.broadcast_to(ea[r][:, h:h + 1], (L, LANES)), low) for r in subs}
    coef = {r: jnp.exp(a[r][L - 1:L, :] - a[r]) * dt[r] for r in subs}
    coef_x = {r: _expand_heads(lambda h: jnp.broadcast_to(coef[r][:, h:h + 1], (L, LANES)), low) for r in subs}
    cs = {r: jnp.dot(bm[r].T.astype(BF16), (xs[r] * coef_x[r]).astype(BF16), preferred_element_type=F32)
          for r in subs}
    dec_x = {}
    for r in subs:
        ea_last = jnp.exp(a[r][L - 1:L, :])
        dec_x[r] = _expand_heads(lambda h: jnp.broadcast_to(ea_last[:, h:h + 1], (1, LANES)), low1)
    gate = {r: _silu(zxd_ref[r * L:(r + 1) * L, 0:inner]) for r in subs}
    for r in subs:
        st = st_ref[...]
        y = y_intra[r] + jnp.dot(cm[r].astype(BF16), st.astype(BF16), preferred_element_type=F32) * ea_x[r]
        st_ref[...] = st * dec_x[r] + jnp.where(srow == scol, cs[r], 0.0)
        y = (y + dskip_ref[...] * xs[r]) * gate[r]
        y_ref[r * L:(r + 1) * L, :] = (_rms(y) * nw_ref[...]).astype(y_ref.dtype)


def _ssd_prompt(zxd, conv_w, conv_b, dtb, a_neg, dskip, norm_w, tril, y_shared, layer, batch, seq, nsub):
    nc = seq // (nsub * CHUNK)
    width = zxd.shape[1]
    inner = M_HEADS * M_HEADDIM
    conv_dim = conv_w.shape[2]
    consts = (conv_w, conv_b, dtb, a_neg, dskip, norm_w)
    return pl.pallas_call(
        _ssd_prompt_body,
        out_shape=[jax.ShapeDtypeStruct(y_shared.shape, y_shared.dtype),
                   jax.ShapeDtypeStruct((batch, M_GROUPS * M_STATE, inner), F32)],
        grid=(batch, nc),
        in_specs=[pl.BlockSpec((nsub * CHUNK, width), lambda b, c: (b * nc + c, 0))]
        + [_layer_block(a, layer) for a in consts] + [pl.BlockSpec(tril.shape, lambda b, c: (0, 0)),
                                                      pl.BlockSpec(memory_space=pl.ANY)],
        out_specs=[pl.BlockSpec((nsub * CHUNK, inner), lambda b, c: (b * nc + c, 1)),
                   pl.BlockSpec((None, M_GROUPS * M_STATE, inner), lambda b, c: (b, 0, 0))],
        scratch_shapes=[pltpu.VMEM((SUBLANES + CHUNK, conv_dim), F32)],
        input_output_aliases={len(consts) + 2: 0},
        compiler_params=_params(("arbitrary", "arbitrary"), 16 * 1024 * 1024),
        name="ssd_prompt",
    )(zxd, *consts, tril, y_shared)


HG_LEVELS = tuple(2 ** i for i in range(7))
HG_FINE = tuple(b for b in HG_LEVELS if b < SUBLANES)


def _hgrn_sum_matrix(L):
    r = np.arange(L)
    tt, rr = np.meshgrid(r, r, indexing="ij")
    mats = []
    for b in HG_FINE:
        mid = (r // (2 * b)) * 2 * b + b - 1
        upper = (r // b) % 2 == 1
        up = (mid[:, None] < rr) & (rr <= tt)
        lo = (tt < rr) & (rr <= mid[:, None])
        mats.append(np.where(upper[:, None], up, lo))
    mats.append(rr <= tt)
    m = np.concatenate(mats, axis=0).astype(np.float32)
    return np.concatenate([m, m], axis=1)


def _hgrn_level_tables(L):
    r = np.arange(L)
    later = np.concatenate([np.repeat((((r // b) % 2) == 1)[:, None], LANES, axis=1) for b in HG_LEVELS], axis=0)
    level = np.full((L, L), -1, np.int32)
    for lvl, b in enumerate(HG_LEVELS):
        t_later = ((r // b) % 2 == 1)[:, None]
        s_earlier = ((r // b) % 2 == 0)[None, :]
        same_parent = (r // (2 * b))[:, None] == (r // (2 * b))[None, :]
        level[t_later & s_earlier & same_parent] = lvl
    level[r, r] = len(HG_LEVELS)
    return np.where(later, 1.0, -1.0).astype(np.float32), level


def _hgrn_prompt_body(x_ref, sums_ref, sign_ref, level_ref, lb_ref, nw_ref, y_shared_ref, y_ref, st_ref):
    del y_shared_ref
    c = pl.program_id(1)
    L = CHUNK
    width = HG_HEADS * LANES
    nlev, nfine = len(HG_LEVELS), len(HG_FINE)

    @pl.when(c == 0)
    def _():
        st_ref[...] = jnp.zeros_like(st_ref)

    level = level_ref[...]
    nsub = x_ref.shape[0] // L
    heads = range(HG_HEADS)
    ch = [(r, h) for r in range(nsub) for h in heads]
    part = lambda j, r, h: x_ref[r * L:(r + 1) * L, j * width + h * LANES:j * width + (h + 1) * LANES]
    g2, kks = {}, {}
    for r, h in ch:
        sl = slice(h * LANES, (h + 1) * LANES)
        logf, kks[r, h] = _hgrn_decay(part(1, r, h), lb_ref[0:1, sl], lb_ref[1:2, sl], lb_ref[2:3, sl])
        g2[r, h] = logf * LOG2_E
    es = {}
    for r in range(nsub):
        for pair in range(HG_HEADS // 2):
            parts = [_split_bf16(g2[r, 2 * pair + k]) for k in range(2)]
            w = jnp.concatenate([jnp.concatenate([parts[0][0], parts[1][0]], axis=1),
                                 jnp.concatenate([parts[0][1], parts[1][1]], axis=1)], axis=0)
            e2 = jnp.dot(sums_ref[...], w, preferred_element_type=F32)
            es[r, 2 * pair], es[r, 2 * pair + 1] = e2[:, :LANES], e2[:, LANES:]
    qs = {k: _silu(part(0, *k)) for k in ch}
    bcs = {k: es[k][nfine * L:(nfine + 1) * L] for k in ch}
    atts = {k: jnp.where(level == nlev, jnp.sum(qs[k] * kks[k], axis=1, keepdims=True), 0.0) for k in ch}
    for lvl, b in enumerate(HG_LEVELS):
        for k in ch:
            q, kk, bc = qs[k], kks[k], bcs[k]
            if lvl < nfine:
                later = sign_ref[lvl * L:(lvl + 1) * L, :] > 0.0
                xk = jnp.exp2(es[k][lvl * L:(lvl + 1) * L]) * jnp.where(later, q, kk)
            else:
                blocks = []
                for p in range(0, L, 2 * b):
                    mid = bc[p + b - 1:p + b, :]
                    blocks.append(kk[p:p + b] * jnp.exp2(mid - bc[p:p + b]))
                    blocks.append(q[p + b:p + 2 * b] * jnp.exp2(bc[p + b:p + 2 * b] - mid))
                xk = jnp.concatenate(blocks, axis=0)
            xk = xk.astype(BF16)
            atts[k] = jnp.where(level == lvl, lax.dot_general(xk, xk, NT_DIMS, preferred_element_type=F32), atts[k])
    qds = {k: (qs[k] * jnp.exp2(bcs[k])).astype(BF16) for k in ch}
    kdts = {k: (kks[k] * jnp.exp2(bcs[k][L - 1:L, :] - bcs[k])).T.astype(BF16) for k in ch}
    dcols = {k: jnp.broadcast_to(jnp.exp2(bcs[k][L - 1:L, :]), (L, LANES)).T for k in ch}
    ivbs = {k: part(2, *k).astype(BF16) for k in ch}
    sogs = {k: _silu(part(3, *k)) for k in ch}
    for r in range(nsub):
        for h in heads:
            sl = slice(h * LANES, (h + 1) * LANES)
            st = st_ref[sl, :]
            o = jnp.dot(jnp.concatenate([atts[r, h].astype(BF16), qds[r, h]], axis=1),
                        jnp.concatenate([ivbs[r, h], st.astype(BF16)], axis=0), preferred_element_type=F32)
            st_ref[sl, :] = st * dcols[r, h] + jnp.dot(kdts[r, h], ivbs[r, h], preferred_element_type=F32)
            y_ref[r * L:(r + 1) * L, sl] = (_rms(o) * nw_ref[:, sl] * sogs[r, h]).astype(y_ref.dtype)


def _hgrn_prompt(xh, sums, sign, level, lb_rows, nw, y_shared, layer, batch, seq, nsub):
    rows = nsub * CHUNK
    nc = seq // rows
    width = HG_HEADS * LANES
    full = lambda a: pl.BlockSpec(a.shape, lambda b, c: (0,) * a.ndim)
    return pl.pallas_call(
        _hgrn_prompt_body,
        out_shape=[jax.ShapeDtypeStruct(y_shared.shape, y_shared.dtype),
                   jax.ShapeDtypeStruct((batch, width, LANES), F32)],
        grid=(batch, nc),
        in_specs=[pl.BlockSpec((rows, 4 * width), lambda b, c: (b * nc + c, 0)),
                  full(sums), full(sign), full(level), _layer_block(lb_rows, layer), _layer_block(nw, layer),
                  pl.BlockSpec(memory_space=pl.ANY)],
        out_specs=[pl.BlockSpec((rows, width), lambda b, c: (b * nc + c, 2)),
                   pl.BlockSpec((None, width, LANES), lambda b, c: (b, 0, 0))],
        input_output_aliases={6: 0},
        compiler_params=_params(("arbitrary", "arbitrary"), 16 * 1024 * 1024),
        name="hgrn_prompt",
    )(xh, sums, sign, level, lb_rows, nw, y_shared)


def _sample_attn_hgrn_body(q8_ref, kr_ref, vr_ref, kvc_ref, kt_ref, vt_ref, sink_ref,
                           xh_ref, s0_ref, lb_ref, hnw_ref, kt_stack_ref, vt_stack_ref, s_stack_ref,
                           oa_ref, ktn_ref, vtn_ref, yh_ref, sn_ref, *, n_prev_valid):
    del kt_stack_ref, vt_stack_ref, s_stack_ref
    nb = q8_ref.shape[0]
    width = HG_HEADS * LANES
    grp = ATT_HEADS // ATT_KV_HEADS
    row8 = lax.broadcasted_iota(jnp.int32, (SUBLANES, LANES), 0)
    lane8 = lax.broadcasted_iota(jnp.int32, (SUBLANES, LANES), 1)
    hrow = lax.broadcasted_iota(jnp.int32, (ATT_HEADS, 1), 0)
    slope = _alibi_slope_col(hrow)
    jj = lax.broadcasted_iota(jnp.int32, (ATT_HEADS, WINDOW), 1)
    cache_dist = (WINDOW - jj).astype(F32)
    cache_ok = jj >= WINDOW - n_prev_valid
    kv_half = (lane8 // HEAD_DIM) == (row8 // grp)
    last_lane = lax.broadcasted_iota(jnp.int32, (LANES, WINDOW), 1) == WINDOW - 1
    rowx = lax.broadcasted_iota(jnp.int32, (SUBLANES, width), 0)
    lanex = lax.broadcasted_iota(jnp.int32, (SUBLANES, width), 1)
    kvw = ATT_KV_HEADS * HEAD_DIM

    smp = range(nb)
    sink = sink_ref[:, 0:1]
    head_of_lane = lanex // LANES

    s_c = [jnp.dot(q8_ref[i].astype(BF16), kt_ref[i].astype(BF16), preferred_element_type=F32) * (HEAD_DIM ** -0.5)
           for i in smp]
    s_c = [jnp.where(cache_ok, s_c[i] - slope * cache_dist, -jnp.inf) for i in smp]
    s_n = [jnp.sum(q8_ref[i] * kr_ref[i], axis=1, keepdims=True) * (HEAD_DIM ** -0.5) for i in smp]
    m = [jnp.maximum(jnp.maximum(jnp.max(s_c[i], axis=1, keepdims=True), s_n[i]), sink) for i in smp]
    p_c = [jnp.exp(s_c[i] - m[i]) for i in smp]
    p_n = [jnp.exp(s_n[i] - m[i]) for i in smp]
    den = [jnp.sum(p_c[i], axis=1, keepdims=True) + p_n[i] + jnp.exp(sink - m[i]) for i in smp]
    o = [lax.dot_general((p_c[i] / den[i]).astype(BF16), vt_ref[i].astype(BF16), NT_DIMS,
                         preferred_element_type=F32) + (p_n[i] / den[i]) * vr_ref[i] for i in smp]
    for i in smp:
        oa_ref[i] = jnp.where(kv_half, o[i], 0.0)
        ktn_ref[i] = jnp.where(last_lane, kvc_ref[0:kvw, i:i + 1], pltpu.roll(kt_ref[i], WINDOW - 1, 1))
        vtn_ref[i] = jnp.where(last_lane, kvc_ref[kvw:2 * kvw, i:i + 1], pltpu.roll(vt_ref[i], WINDOW - 1, 1))

    q = [_silu(xh_ref[i][:, :width]) for i in smp]
    dk = [_hgrn_decay(xh_ref[i][:, width:2 * width], lb_ref[0:1, :], lb_ref[1:2, :], lb_ref[2:3, :]) for i in smp]
    f = [jnp.exp(dk[i][0]) for i in smp]
    kk = [dk[i][1] for i in smp]
    iv = [xh_ref[i][:, 2 * width:3 * width] for i in smp]
    lhs_q = [jnp.where(rowx == head_of_lane, jnp.broadcast_to(q[i] * f[i], (SUBLANES, width)), 0.0).astype(BF16)
             for i in smp]
    qs = [jnp.dot(lhs_q[i], s0_ref[i].astype(BF16), preferred_element_type=F32) for i in smp]
    for i in smp:
        qkk = q[i] * kk[i]
        o_parts = []
        for h in range(HG_HEADS):
            sl = slice(h * LANES, (h + 1) * LANES)
            att = jnp.sum(qkk[:, sl], axis=1, keepdims=True)
            o_parts.append(_rms(att * iv[i][:, sl] + qs[i][h:h + 1, :]))
        yh_ref[i] = jnp.concatenate(o_parts, axis=1) * hnw_ref[...] * _silu(xh_ref[i][:, 3 * width:])
    i8 = [jnp.concatenate([iv[i][:, h * LANES:(h + 1) * LANES] for h in range(HG_HEADS)]
                          + [jnp.zeros((SUBLANES - HG_HEADS, LANES), F32)], axis=0).astype(BF16) for i in smp]
    ones_rows45 = jnp.where((row8 == HG_HEADS) | (row8 == HG_HEADS + 1), 1.0, 0.0).astype(BF16)
    for i in smp:
        f_b = jnp.broadcast_to(f[i], (SUBLANES, width))
        f_hi = f_b.astype(BF16).astype(F32)
        lhs = jnp.where(rowx == head_of_lane, jnp.broadcast_to(kk[i], (SUBLANES, width)),
                        jnp.where(rowx == HG_HEADS, f_hi, jnp.where(rowx == HG_HEADS + 1, f_b - f_hi, 0.0)))
        both = lax.dot_general(lhs.astype(BF16), jnp.concatenate([i8[i], ones_rows45], axis=1), TN_DIMS,
                               preferred_element_type=F32)
        sn_ref[i] = s0_ref[i] * both[:, LANES:] + both[:, :LANES]


def _sample_attn_hgrn(q8, kr, vr, kvc, kt, vt, sink8, xh, s0, lb_rows, hnw, kt_stack, vt_stack, s_stack, layer, nb):
    n = q8.shape[0]
    per = lambda a: pl.BlockSpec((nb,) + a.shape[1:], lambda i: (i,) + (0,) * (a.ndim - 1))
    lay = lambda a: pl.BlockSpec((None, nb) + a.shape[2:], lambda i: (layer, i) + (0,) * (a.ndim - 2))
    hbm = pl.BlockSpec(memory_space=pl.ANY)
    width = HG_HEADS * LANES
    out_shape = [jax.ShapeDtypeStruct((n, ATT_HEADS, LANES), F32),
                 jax.ShapeDtypeStruct(kt_stack.shape, F32),
                 jax.ShapeDtypeStruct(vt_stack.shape, F32),
                 jax.ShapeDtypeStruct((n, 1, width), F32),
                 jax.ShapeDtypeStruct(s_stack.shape, F32)]
    return pl.pallas_call(
        functools.partial(_sample_attn_hgrn_body, n_prev_valid=min(WINDOW, PAST_LEN)),
        out_shape=out_shape,
        grid=(n // nb,),
        in_specs=[per(q8), per(kr), per(vr), pl.BlockSpec((None,) + kvc.shape[1:], lambda i: (i, 0, 0)),
                  lay(kt), lay(vt), _layer_block(sink8, layer), per(xh), lay(s0),
                  _layer_block(lb_rows, layer), _layer_block(hnw, layer), hbm, hbm, hbm],
        out_specs=[per(q8), lay(kt_stack), lay(vt_stack),
                   pl.BlockSpec((nb, 1, width), lambda i: (i, 0, 0)), lay(s_stack)],
        input_output_aliases={11: 1, 12: 2, 13: 4},
        compiler_params=_params(("parallel",), 4 * nb * (2 * kt.shape[2] + s0.shape[2]) * LANES * 4),
        name="sample_attn_hgrn",
    )(q8, kr, vr, kvc, kt, vt, sink8, xh, s0, lb_rows, hnw, kt_stack, vt_stack, s_stack)


def _ssd_sample_body(col_ref, cst_ref, h0_ref, cw_ref, cb_ref, dtb_ref, a_ref, dskip_ref, nw_ref, stack_ref,
                     ym_ref, hn_ref, xs_s, b_s, c_s, dt_s, da_s, yrow_s, y_s):
    del stack_ref
    h = pl.program_id(0)
    inner = M_HEADS * M_HEADDIM
    conv_dim = inner + 2 * M_GROUPS * M_STATE
    kvw2 = 2 * ATT_KV_HEADS * HEAD_DIM
    z0, x0, d0 = kvw2, kvw2 + inner, kvw2 + inner + conv_dim
    ns = col_ref.shape[1]

    @pl.when(h == 0)
    def _():
        conv = col_ref[x0:d0, :] * cw_ref[CONV_W - 1] + cb_ref[...]
        for w in range(CONV_W - 1):
            conv = conv + cst_ref[w] * cw_ref[w]
        act = _silu(conv)
        xs_s[...] = act[:inner]
        b_s[...] = act[inner:inner + M_GROUPS * M_STATE]
        c_s[...] = act[inner + M_GROUPS * M_STATE:]
        dt = _softplus(col_ref[d0:d0 + M_HEADS, :] + dtb_ref[...])
        da = jnp.exp(dt * a_ref[...])
        for hh in range(M_HEADS):
            dt_s[hh] = jnp.broadcast_to(dt[hh:hh + 1, :], (SUBLANES, ns))
            da_s[hh] = jnp.broadcast_to(da[hh:hh + 1, :], (SUBLANES, ns))

    hpg = M_HEADS // M_GROUPS
    x_h = xs_s[pl.ds(pl.multiple_of(h * M_HEADDIM, M_HEADDIM), M_HEADDIM), :]
    g0 = pl.multiple_of((h // hpg) * M_STATE, M_STATE)
    b_g = b_s[pl.ds(g0, M_STATE), :]
    c_g = c_s[pl.ds(g0, M_STATE), :]
    dt = dt_s[h][0:1, :]
    da = da_s[h][0:1, :]
    dtx = x_h * dt
    for p in range(M_HEADDIM):
        s_p = h0_ref[p * M_STATE:(p + 1) * M_STATE, :]
        yrow_s[p:p + 1, :] = jnp.sum(c_g * s_p, axis=0, keepdims=True)
        hn_ref[p * M_STATE:(p + 1) * M_STATE, :] = da * s_p + b_g * dtx[p:p + 1, :]
    cb = jnp.sum(c_g * b_g, axis=0, keepdims=True)
    y_s[pl.ds(pl.multiple_of(h * M_HEADDIM, M_HEADDIM), M_HEADDIM), :] = (
        da * yrow_s[...] + (dt * cb + dskip_ref[h][0:1, :]) * x_h)

    @pl.when(h == pl.num_programs(0) - 1)
    def _():
        y = y_s[...] * _silu(col_ref[z0:x0, :])
        y = y * lax.rsqrt(jnp.mean(y * y, axis=0, keepdims=True) + RMS_EPS) * nw_ref[...]
        for j in range(inner // LANES):
            ym_ref[:, j * LANES:(j + 1) * LANES] = y[j * LANES:(j + 1) * LANES, :].T


def _ssd_sample(col, cst, h0, cw, cb, dtb, a_neg, dskip, nw, stack, layer):
    ns = col.shape[1]
    inner = M_HEADS * M_HEADDIM
    state_block = pl.BlockSpec((None, None) + h0.shape[2:], lambda h: (layer, h, 0, 0))
    return pl.pallas_call(
        _ssd_sample_body,
        out_shape=[jax.ShapeDtypeStruct((ns, inner), F32), jax.ShapeDtypeStruct(stack.shape, F32)],
        grid=(M_HEADS,),
        in_specs=[pl.BlockSpec(col.shape, lambda h: (0, 0)), _layer_block(cst, layer), state_block]
        + [_layer_block(a, layer) for a in (cw, cb, dtb, a_neg, dskip, nw)] + [pl.BlockSpec(memory_space=pl.ANY)],
        out_specs=[pl.BlockSpec((ns, inner), lambda h: (0, 0)), state_block],
        scratch_shapes=[pltpu.VMEM((inner, ns), F32), pltpu.VMEM((M_GROUPS * M_STATE, ns), F32),
                        pltpu.VMEM((M_GROUPS * M_STATE, ns), F32), pltpu.VMEM((M_HEADS, SUBLANES, ns), F32),
                        pltpu.VMEM((M_HEADS, SUBLANES, ns), F32), pltpu.VMEM((M_HEADDIM, ns), F32),
                        pltpu.VMEM((inner, ns), F32)],
        input_output_aliases={9: 1},
        compiler_params=_params(("arbitrary",), 24 * 1024 * 1024),
        name="ssd_sample",
    )(col, cst, h0, cw, cb, dtb, a_neg, dskip, nw, stack)


def _row_tile(rows):
    for tm in (1024, 512, 256, 128):
        if rows % tm == 0:
            return tm
    return rows


def kernel(x_prompt, x_sample, cache_swa_k, cache_swa_v, state_conv, state_ssm, state_hgrn, ln1_g, ln1_b, ffn1_wg, ffn1_wu, ffn1_wd, w_in, b_in, att_sinks, conv_w, conv_b, dt_bias, a_log, d_skip, ssm_norm_w, hg_lb_logits, hg_norm_w, w_br_att, w_br_ssm, w_br_hg, w_out, ln2_g, ln2_b, ffn2_wg, ffn2_wu, ffn2_wd, ln3_g, ln3_b):
    bp, seq, d = x_prompt.shape
    ns = x_sample.shape[0]
    depth = w_in.shape[0]
    assert seq % CHUNK == 0 and x_sample.shape[1] == 1
    qw = ATT_HEADS * HEAD_DIM
    kvw = ATT_KV_HEADS * HEAD_DIM
    inner = M_HEADS * M_HEADDIM
    conv_dim = inner + 2 * M_GROUPS * M_STATE
    width = HG_HEADS * LANES
    hpg = M_HEADS // M_GROUPS

    bf = lambda a: a.astype(BF16)
    row3 = lambda a: a.reshape(depth, 1, a.shape[-1])
    o_z = qw + 2 * kvw
    o_dt = o_z + inner + conv_dim
    o_h = o_dt + M_HEADS
    o_g = o_h + 4 * width
    w_in_t = jnp.swapaxes(w_in, 1, 2)
    dt_pad = jnp.pad(w_in_t[:, o_dt:o_h, :], ((0, 0), (0, LANES - M_HEADS), (0, 0)))
    w_rows = bf(jnp.concatenate([w_in_t[:, :o_dt, :], dt_pad, w_in_t[:, o_h:o_g, :]], axis=1))
    b_flat = jnp.concatenate([b_in[:, :o_dt], jnp.pad(b_in[:, o_dt:o_h], ((0, 0), (0, LANES - M_HEADS))),
                              b_in[:, o_h:o_g]], axis=1)
    b_rows = row3(b_flat)
    col_rows = (qw, o_dt + LANES)
    b_cols = b_flat[:, col_rows[0]:col_rows[1], None]
    in_widths = (qw + 2 * kvw, inner + conv_dim + LANES, 4 * width)
    ffn1 = (bf(ffn1_wg), bf(ffn1_wu), bf(ffn1_wd), row3(ln1_g), row3(ln1_b))
    ffn2 = (bf(ffn2_wg), bf(ffn2_wu), bf(ffn2_wd), row3(ln3_g), row3(ln3_b))
    w_merge = bf(jnp.concatenate([w_in_t[:, o_g:, :], w_br_att, w_br_ssm, w_br_hg, w_out], axis=1))
    merge_w = (w_merge, row3(b_in[:, o_g:]), row3(ln2_g), row3(ln2_b))

    a_neg = -jnp.exp(a_log.astype(F32))
    pad_h = lambda a: row3(jnp.pad(a, ((0, 0), (0, LANES - M_HEADS))))
    ssd_rows = (conv_w, row3(conv_b), pad_h(dt_bias), pad_h(a_neg), row3(jnp.repeat(d_skip, M_HEADDIM, axis=1)),
                row3(ssm_norm_w))
    lb_all = jnp.cumsum(jax.nn.softmax(hg_lb_logits.astype(F32), axis=0), axis=0)
    lb_all = lb_all - lb_all[0]
    lb_rows = jnp.stack([jnp.log(lb_all), jnp.log1p(-lb_all), 1.0 - lb_all], axis=1)
    lb_rows = jnp.pad(lb_rows, ((0, 0), (0, SUBLANES - 3), (0, 0)))
    hnw = row3(jnp.tile(hg_norm_w, (1, HG_HEADS)))
    sums = jnp.asarray(_hgrn_sum_matrix(CHUNK), BF16)
    sign_np, level_np = _hgrn_level_tables(CHUNK)
    hg_sign, hg_level = jnp.asarray(sign_np), jnp.asarray(level_np)
    att_bias = jnp.asarray(_attn_bias(CHUNK))
    tril = jnp.asarray(np.tril(np.ones((CHUNK, CHUNK), np.float32)), BF16)
    sink8 = jnp.broadcast_to(att_sinks[:, :, None], (depth, ATT_HEADS, LANES))

    tm_p = _row_tile(bp * seq)
    tm_s = _row_tile(ns)
    nb_s = max(k for k in (2 * SUBLANES, SUBLANES, 1) if ns % k == 0)
    lanes_b = lambda a: jnp.broadcast_to(a[..., None], a.shape + (ns,))
    kt_in = cache_swa_k.transpose(0, 1, 3, 4, 2).reshape(depth, ns, kvw, WINDOW)
    vt_in = cache_swa_v.transpose(0, 1, 3, 4, 2).reshape(depth, ns, kvw, WINDOW)
    ssm_in = state_ssm.transpose(0, 2, 3, 4, 1).reshape(depth, M_HEADS, M_HEADDIM * M_STATE, ns)
    conv_in = state_conv.transpose(0, 2, 3, 1)
    hg_in = state_hgrn.reshape(depth, ns, width, LANES)
    kt_out, vt_out = jnp.zeros(kt_in.shape, F32), jnp.zeros(vt_in.shape, F32)
    ssm_out, hg_out = jnp.zeros(ssm_in.shape, F32), jnp.zeros(hg_in.shape, F32)
    cw_b, cb_b = lanes_b(conv_w), lanes_b(conv_b)
    dtb_b, a_b, nw_b = lanes_b(dt_bias), lanes_b(a_neg), lanes_b(ssm_norm_w)
    dskip_b = jnp.broadcast_to(d_skip[:, :, None, None], (depth, M_HEADS, SUBLANES, ns))
    att_sub = max(k for k in (4, 2, 1) if (seq // CHUNK) % k == 0)
    ssd_sub = max(k for k in (4, 2, 1) if (seq // CHUNK) % k == 0)
    scan_sub = max(k for k in (2, 1) if (seq // CHUNK) % k == 0)

    xp = x_prompt.reshape(bp * seq, d)
    xs = x_sample.reshape(ns, d)
    p_states, s_conv = [], []
    for l in range(depth):
        xp = _ffn_ln(xp, *ffn1, l, tm_p)
        qkv, zxd, xh = _inproj(xp, w_rows, b_rows, l, tm_p, in_widths)
        y3 = _attn_prompt(qkv, att_sinks, att_bias, l, bp, seq, att_sub)
        y3, st_ssm = _ssd_prompt(zxd, *ssd_rows, tril, y3, l, bp, seq, ssd_sub)
        y3, st_hg = _hgrn_prompt(xh, sums, hg_sign, hg_level, lb_rows, hnw, y3, l, bp, seq, scan_sub)
        xp = _merge(xp, y3, *merge_w, l, tm_p)
        xp = _ffn_ln(xp, *ffn2, l, tm_p)
        qkv3 = qkv.reshape(bp, seq, qw + 2 * kvw)
        p_k = qkv3[:, seq - WINDOW:, qw:qw + kvw].reshape(bp, WINDOW, ATT_KV_HEADS, HEAD_DIM)
        p_v = qkv3[:, seq - WINDOW:, qw + kvw:].reshape(bp, WINDOW, ATT_KV_HEADS, HEAD_DIM)
        p_conv = zxd.reshape(bp, seq, -1)[:, seq - (CONV_W - 1):, inner:inner + conv_dim]
        st6 = st_ssm.reshape(bp, M_GROUPS, M_STATE, M_GROUPS, hpg, M_HEADDIM)
        p_ssm = jnp.stack([st6[:, g, :, g] for g in range(M_GROUPS)], axis=1)
        p_ssm = p_ssm.transpose(0, 1, 3, 4, 2).reshape(bp, M_HEADS, M_HEADDIM, M_STATE)
        p_hg = st_hg.reshape(bp, HG_HEADS, LANES, LANES)
        p_states.append((p_k, p_v, p_conv, p_ssm, p_hg))

        xs = _ffn_ln(xs, *ffn1, l, tm_s)
        qkv_s, zx_s, xh_s, col_s = _inproj_sample(xs, w_rows, b_rows, b_cols, l, in_widths, col_rows)
        q4 = qkv_s[:, :qw].reshape(ns, ATT_KV_HEADS, ATT_HEADS // ATT_KV_HEADS, HEAD_DIM)
        zq = jnp.zeros_like(q4[:, 0])
        q8 = jnp.concatenate([jnp.concatenate([q4[:, 0], zq], axis=-1),
                              jnp.concatenate([zq, q4[:, 1]], axis=-1)], axis=1)
        kv_cols = col_s[:2 * kvw].reshape(2 * kvw, ns // nb_s, nb_s).transpose(1, 0, 2)
        oa, kt_out, vt_out, yh_s, hg_out = _sample_attn_hgrn(
            q8, qkv_s[:, None, qw:qw + kvw], qkv_s[:, None, qw + kvw:], kv_cols, kt_in, vt_in, sink8,
            xh_s[:, None, :], hg_in, lb_rows, hnw, kt_out, vt_out, hg_out, l, nb_s)
        ym_s, ssm_out = _ssd_sample(col_s, conv_in, ssm_in, cw_b, cb_b, dtb_b, a_b, dskip_b, nw_b, ssm_out, l)
        grp = ATT_HEADS // ATT_KV_HEADS
        y3_s = jnp.concatenate([oa[:, :grp, :HEAD_DIM].reshape(ns, grp * HEAD_DIM),
                                oa[:, grp:, HEAD_DIM:].reshape(ns, grp * HEAD_DIM),
                                ym_s, yh_s.reshape(ns, width)], axis=1)
        xs = _merge(xs, y3_s, *merge_w, l, tm_s)
        xs = _ffn_ln(xs, *ffn2, l, tm_s)
        s_conv.append(jnp.concatenate([state_conv[l][:, 1:], zx_s[:, None, inner:inner + conv_dim]], axis=1))

    outs_p = [jnp.stack(t) for t in zip(*p_states)]
    unwind = lambda t: t.reshape(depth, ns, ATT_KV_HEADS, HEAD_DIM, WINDOW).transpose(0, 1, 4, 2, 3)
    s_ssm = ssm_out.reshape(depth, M_HEADS, M_HEADDIM, M_STATE, ns).transpose(0, 4, 1, 2, 3)
    outs_s = [unwind(kt_out), unwind(vt_out), jnp.stack(s_conv), s_ssm,
              hg_out.reshape(depth, ns, HG_HEADS, LANES, LANES)]
    return (xp.reshape(bp, seq, d), xs.reshape(ns, 1, d), *outs_p, *outs_s)
```

```python
import functools

import numpy as np
import jax
import jax.numpy as jnp
from jax import lax
from jax.experimental import pallas as pl
from jax.experimental.pallas import tpu as pltpu

F32 = jnp.float32
BF16 = jnp.bfloat16

ATT_HEADS = 8
ATT_KV_HEADS = 2
HEAD_DIM = 64
WINDOW = 128
PAST_LEN = 8192
M_HEADS = 8
M_HEADDIM = 64
M_GROUPS = 2
M_STATE = 64
CONV_W = 4
HG_HEADS = 4
DEPTH = 4
ALPHA = (2.0 * DEPTH) ** 0.25
LN_EPS = 1e-5
RMS_EPS = 1e-6
LOG2_E = 1.4426950408889634
CHUNK = 128

V7X_VMEM_BYTES = 64 * 1024 * 1024
LANES = 128
SUBLANES = 8
MXU_WIDTH = 256

NN_DIMS = (((1,), (0,)), ((), ()))
NT_DIMS = (((1,), (1,)), ((), ()))
TN_DIMS = (((0,), (0,)), ((), ()))


def _vmem_limit(nbytes):
    return int(min(V7X_VMEM_BYTES - 8 * 1024 * 1024, nbytes + 16 * 1024 * 1024))


def _params(semantics, vmem_bytes):
    return pltpu.CompilerParams(dimension_semantics=semantics, vmem_limit_bytes=_vmem_limit(vmem_bytes))


def _layer_block(a, layer):
    zeros = (0,) * (a.ndim - 1)
    return pl.BlockSpec((None,) + a.shape[1:], lambda *_: (layer,) + zeros)


def _resident(block_shape, index_map):
    return pl.BlockSpec(block_shape, index_map, pipeline_mode=pl.Buffered(1))


def _silu(x):
    return x * jax.nn.sigmoid(x)


def _softplus(x):
    return jnp.maximum(x, 0.0) + jnp.log(1.0 + jnp.exp(-jnp.abs(x)))


def _layernorm(y, g, b):
    mu = jnp.mean(y, axis=-1, keepdims=True)
    d = y - mu
    var = jnp.mean(d * d, axis=-1, keepdims=True)
    return d * lax.rsqrt(var + LN_EPS) * g + b


def _rms(y):
    return y * lax.rsqrt(jnp.mean(y * y, axis=-1, keepdims=True) + RMS_EPS)


def _split_bf16(x):
    hi = x.astype(BF16)
    lo = (x - hi.astype(F32)).astype(BF16)
    return hi, lo


def _col_chunks(width, step):
    return [(c, min(step, width - c)) for c in range(0, width, step)]


def _ffn_ln_body(x_ref, wg_ref, wu_ref, wd_ref, g_ref, b_ref, o_ref, *, ff_chunk):
    x = x_ref[...]
    xb = x.astype(BF16)
    mm = lambda a, w: lax.dot_general(a, w, NN_DIMS, preferred_element_type=F32)
    acc = None
    for c0, cw in _col_chunks(wg_ref.shape[1], ff_chunk):
        gate = mm(xb, wg_ref[:, c0:c0 + cw])
        up = mm(xb, wu_ref[:, c0:c0 + cw])
        hid = (_silu(gate) * up).astype(BF16)
        part = mm(hid, wd_ref[c0:c0 + cw, :])
        acc = part if acc is None else acc + part
    o_ref[...] = _layernorm(ALPHA * x + 0.5 * acc, g_ref[...], b_ref[...])


def _ffn_ln(x, wg, wu, wd, g, b, layer, tm):
    rows, d = x.shape
    f = wg.shape[2]
    wspec = lambda shape: _resident((None,) + shape, lambda i: (layer, 0, 0))
    vmem = 3 * d * f * wg.dtype.itemsize + 4 * tm * d * 4 + 4 * tm * MXU_WIDTH * 4
    return pl.pallas_call(
        functools.partial(_ffn_ln_body, ff_chunk=MXU_WIDTH),
        out_shape=jax.ShapeDtypeStruct((rows, d), F32),
        grid=(rows // tm,),
        in_specs=[pl.BlockSpec((tm, d), lambda i: (i, 0)),
                  wspec((d, f)), wspec((d, f)), wspec((f, d)),
                  wspec((1, d)), wspec((1, d))],
        out_specs=pl.BlockSpec((tm, d), lambda i: (i, 0)),
        compiler_params=_params(("parallel",), vmem),
        name="ffn_ln",
    )(x, wg, wu, wd, g, b)


def _hgrn_decay(hf, log_lb, log1m_lb, one_m_lb):
    e_neg = jnp.exp(-jnp.abs(hf))
    one_p = 1.0 + e_neg
    log_sig = jnp.minimum(hf, 0.0) - jnp.log(one_p)
    inv = 1.0 / one_p
    kk = one_m_lb * jnp.where(hf >= 0.0, e_neg * inv, inv)
    b = log1m_lb + log_sig
    logf = jnp.maximum(log_lb, b) + jnp.log(1.0 + jnp.exp(-jnp.abs(log_lb - b)))
    return logf, kk


def _project_rows(xb, wt_ref, b_ref, o_refs):
    base = 0
    for o_ref in o_refs:
        for c0, cw in _col_chunks(o_ref.shape[1], 2 * MXU_WIDTH):
            w = wt_ref[base + c0:base + c0 + cw, :]
            o_ref[:, c0:c0 + cw] = (lax.dot_general(xb, w, NT_DIMS, preferred_element_type=F32)
                                    + b_ref[:, base + c0:base + c0 + cw])
        base += o_ref.shape[1]


def _inproj_body(x_ref, wt_ref, b_ref, *o_refs):
    _project_rows(x_ref[...].astype(BF16), wt_ref, b_ref, o_refs)


def _inproj(x, wt, b, layer, tm, widths):
    rows, d = x.shape
    n = wt.shape[1]
    assert n == sum(widths)
    vmem = d * n * 2 + 2 * tm * d * 4 + 2 * tm * n * 4
    return pl.pallas_call(
        _inproj_body,
        out_shape=[jax.ShapeDtypeStruct((rows, wd), F32) for wd in widths],
        grid=(rows // tm,),
        in_specs=[pl.BlockSpec((tm, d), lambda i: (i, 0)),
                  _resident((None, n, d), lambda i: (layer, 0, 0)),
                  _resident((None, 1, n), lambda i: (layer, 0, 0))],
        out_specs=[pl.BlockSpec((tm, wd), lambda i: (i, 0)) for wd in widths],
        compiler_params=_params(("parallel",), vmem),
        name="inproj",
    )(x, wt, b)


def _inproj_sample_body(x_ref, wt_ref, b_ref, bt_ref, *o_refs, col_rows):
    xb = x_ref[...].astype(BF16)
    _project_rows(xb, wt_ref, b_ref, o_refs[:-1])
    r0, r1 = col_rows
    o_refs[-1][...] = lax.dot_general(wt_ref[r0:r1, :], xb, NT_DIMS, preferred_element_type=F32) + bt_ref[...]


def _inproj_sample(x, wt, b, bt, layer, widths, col_rows):
    rows, d = x.shape
    n, nt = wt.shape[1], col_rows[1] - col_rows[0]
    vmem = d * n * 2 + 2 * rows * d * 4 + 2 * rows * (n + nt) * 4
    return pl.pallas_call(
        functools.partial(_inproj_sample_body, col_rows=col_rows),
        out_shape=[jax.ShapeDtypeStruct((rows, wd), F32) for wd in widths]
        + [jax.ShapeDtypeStruct((nt, rows), F32)],
        grid=(1,),
        in_specs=[pl.BlockSpec((rows, d), lambda i: (0, 0)),
                  pl.BlockSpec((None, n, d), lambda i: (layer, 0, 0)),
                  pl.BlockSpec((None, 1, n), lambda i: (layer, 0, 0)),
                  pl.BlockSpec((None, nt, 1), lambda i: (layer, 0, 0))],
        out_specs=[pl.BlockSpec((rows, wd), lambda i: (0, 0)) for wd in widths]
        + [pl.BlockSpec((nt, rows), lambda i: (0, 0))],
        compiler_params=_params(("arbitrary",), vmem),
        name="inproj_sample",
    )(x, wt, b, bt)


N_BRANCH = 3


def _merge_body(x_ref, y_ref, w_ref, bgate_ref, g_ref, b_ref, o_ref):
    x = x_ref[...]
    xb = x.astype(BF16)
    d = x.shape[1]
    wb = y_ref.shape[1] // N_BRANCH
    br0, out0 = N_BRANCH * d, N_BRANCH * d + N_BRANCH * wb
    ybs = [y_ref[:, k * wb:(k + 1) * wb].astype(BF16) for k in range(N_BRANCH)]
    merged = []
    for c0, cw in _col_chunks(d, MXU_WIDTH):
        m = None
        for k in range(N_BRANCH):
            col = k * d + c0
            gate = jax.nn.sigmoid(lax.dot_general(xb, w_ref[col:col + cw, :], NT_DIMS,
                                                  preferred_element_type=F32) + bgate_ref[:, col:col + cw])
            term = gate * jnp.dot(ybs[k], w_ref[br0 + k * wb:br0 + (k + 1) * wb, c0:c0 + cw],
                                  preferred_element_type=F32)
            m = term if m is None else m + term
        merged.append(m.astype(BF16))
    y = jnp.dot(jnp.concatenate(merged, axis=1), w_ref[out0:out0 + d, :], preferred_element_type=F32)
    o_ref[...] = _layernorm(ALPHA * x + y, g_ref[...], b_ref[...])


def _merge(x, y, w, bgate, g, b, layer, tm):
    rows, d = x.shape
    wspec = lambda a: _resident((None,) + a.shape[1:], lambda i: (layer, 0, 0))
    row = lambda width: pl.BlockSpec((tm, width), lambda i: (i, 0))
    vmem = w.shape[1] * d * 2 + 4 * tm * d * 4 + 2 * tm * y.shape[1] * y.dtype.itemsize
    return pl.pallas_call(
        _merge_body,
        out_shape=jax.ShapeDtypeStruct((rows, d), F32),
        grid=(rows // tm,),
        in_specs=[row(d), row(y.shape[1]), wspec(w), wspec(bgate), wspec(g), wspec(b)],
        out_specs=row(d),
        compiler_params=_params(("parallel",), vmem),
        name="merge_out_ln",
    )(x, y, w, bgate, g, b)


def _alibi_slope_col(head_rows):
    slope = jnp.zeros(head_rows.shape, F32)
    for h in range(ATT_HEADS):
        slope = jnp.where(head_rows == h, 2.0 ** (-8.0 * (h + 1) / ATT_HEADS), slope)
    return slope


def _attn_bias(blk):
    tq = np.arange(blk)[:, None]
    j = np.arange(2 * blk)[None, :]
    dist = WINDOW + tq - j
    ok = (dist >= 0) & (dist <= WINDOW)
    slopes = 2.0 ** (-8.0 * np.arange(1, ATT_HEADS + 1) / ATT_HEADS)
    out = np.empty((2, ATT_HEADS, blk, 2 * blk), np.float32)
    for has_prev in range(2):
        vis = ok & ((j >= blk) | (has_prev == 1))
        out[has_prev] = np.where(vis[None], -slopes[:, None, None] * dist[None], -np.inf)
    return out


def _attn_prompt_body(sink_ref, bias_ref, q_ref, kvc_ref, kvp_ref, o_ref, *, layer):
    blk = kvp_ref.shape[0]
    nsub = q_ref.shape[0] // blk
    grp = ATT_HEADS // ATT_KV_HEADS
    has_prev = jnp.minimum(pl.program_id(1), 1)
    lane = lax.broadcasted_iota(jnp.int32, (blk, LANES), 1)
    low = lane < HEAD_DIM

    def kv_block(i):
        return kvp_ref[...] if i < 0 else kvc_ref[i * blk:(i + 1) * blk, :]

    kcats, vcats = [], []
    for i in range(nsub):
        prev, cur = kv_block(i - 1), kv_block(i)
        kcats.append(jnp.concatenate([prev[:, :LANES], cur[:, :LANES]], axis=0).astype(BF16))
        vcats.append(jnp.concatenate([prev[:, LANES:], cur[:, LANES:]], axis=0).astype(BF16))

    def scores(i, h):
        kv = h // grp
        qcol = q_ref[i * blk:(i + 1) * blk, (h // 2) * LANES:(h // 2 + 1) * LANES] * (HEAD_DIM ** -0.5)
        if (h % 2) != kv:
            qcol = pltpu.roll(qcol, HEAD_DIM, 1)
        qh = jnp.where(low if kv == 0 else jnp.logical_not(low), qcol, 0.0).astype(BF16)
        bias = bias_ref[has_prev, h] if i == 0 else bias_ref[1, h]
        return lax.dot_general(qh, kcats[i], NT_DIMS, preferred_element_type=F32) + bias

    ih = [(i, h) for i in range(nsub) for h in range(ATT_HEADS)]
    ss = {k: scores(*k) for k in ih}
    ms = {k: jnp.maximum(jnp.max(ss[k], axis=1, keepdims=True), sink_ref[layer, k[1]]) for k in ih}
    ps = {k: jnp.exp(ss[k] - ms[k]) for k in ih}
    dens = {k: jnp.sum(ps[k], axis=1, keepdims=True) + jnp.exp(sink_ref[layer, k[1]] - ms[k]) for k in ih}
    ps = {k: (ps[k] * (1.0 / dens[k])).astype(BF16) for k in ih}
    outs = {k: jnp.dot(ps[k], vcats[k[0]], preferred_element_type=F32) for k in ih}
    for i in range(nsub):
        for c in range(ATT_HEADS // 2):
            even, odd = outs[i, 2 * c], outs[i, 2 * c + 1]
            if (2 * c) // grp == 0:
                col = jnp.where(low, even, pltpu.roll(odd, HEAD_DIM, 1))
            else:
                col = jnp.where(low, pltpu.roll(even, HEAD_DIM, 1), odd)
            o_ref[i * blk:(i + 1) * blk, c * LANES:(c + 1) * LANES] = col.astype(o_ref.dtype)
    qw = ATT_HEADS * HEAD_DIM
    o_ref[:, qw:] = jnp.zeros((o_ref.shape[0], o_ref.shape[1] - qw), o_ref.dtype)


def _attn_prompt(qkv, sinks, bias, layer, batch, seq, nsub):
    rows = nsub * CHUNK
    ns = seq // rows
    qw = ATT_HEADS * HEAD_DIM
    kvw = 2 * ATT_KV_HEADS * HEAD_DIM
    kv_col = qw // kvw
    return pl.pallas_call(
        functools.partial(_attn_prompt_body, layer=layer),
        out_shape=jax.ShapeDtypeStruct((batch * seq, N_BRANCH * qw), BF16),
        grid=(batch, ns),
        in_specs=[pl.BlockSpec(memory_space=pltpu.SMEM),
                  pl.BlockSpec(bias.shape, lambda b, n: (0,) * bias.ndim),
                  pl.BlockSpec((rows, qw), lambda b, n: (b * ns + n, 0)),
                  pl.BlockSpec((rows, kvw), lambda b, n: (b * ns + n, kv_col)),
                  pl.BlockSpec((CHUNK, kvw), lambda b, n: ((b * ns + n) * nsub - jnp.minimum(n, 1), kv_col))],
        out_specs=pl.BlockSpec((rows, N_BRANCH * qw), lambda b, n: (b * ns + n, 0)),
        compiler_params=_params(("parallel", "arbitrary"), 16 * 1024 * 1024),
        name="attn_prompt",
    )(sinks, bias, qkv, qkv, qkv)


def _expand_heads(col_of, lane_low):
    cols = []
    for j in range(M_HEADS // 2):
        cols.append(jnp.where(lane_low, col_of(2 * j), col_of(2 * j + 1)))
    return jnp.concatenate(cols, axis=1)


def _ssd_prompt_body(zxd_ref, cw_ref, cb_ref, dtb_ref, a_ref, dskip_ref, nw_ref, tril_ref, y_shared_ref,
                     y_ref, st_ref, xpad_ref):
    del y_shared_ref
    c = pl.program_id(1)
    L = CHUNK
    inner = M_HEADS * M_HEADDIM
    conv_dim = inner + 2 * M_GROUPS * M_STATE

    @pl.when(c == 0)
    def _():
        st_ref[...] = jnp.zeros_like(st_ref)
        xpad_ref[0:SUBLANES, :] = jnp.zeros((SUBLANES, conv_dim), F32)

    lane = lax.broadcasted_iota(jnp.int32, (L, LANES), 1)
    low = lane < M_STATE
    low1 = lane[0:1, :] < M_STATE
    causal = lax.broadcasted_iota(jnp.int32, (L, L), 0) >= lax.broadcasted_iota(jnp.int32, (L, L), 1)
    srow = lax.broadcasted_iota(jnp.int32, (M_GROUPS * M_STATE, inner), 0) // M_STATE
    scol = lax.broadcasted_iota(jnp.int32, (M_GROUPS * M_STATE, inner), 1) // (inner // M_GROUPS)
    hpg = M_HEADS // M_GROUPS

    subs = range(zxd_ref.shape[0] // L)
    heads = range(M_HEADS)
    xs, bm, cm, dt, a = {}, {}, {}, {}, {}
    for r in subs:
        rs = slice(r * L, (r + 1) * L)
        xbc = zxd_ref[rs, inner:inner + conv_dim]
        xpad_ref[SUBLANES:SUBLANES + L, :] = xbc
        conv = xbc * cw_ref[CONV_W - 1:CONV_W, :] + cb_ref[...]
        for w in range(CONV_W - 1):
            off = SUBLANES - (CONV_W - 1) + w
            conv = conv + xpad_ref[off:off + L, :] * cw_ref[w:w + 1, :]
        xpad_ref[0:SUBLANES, :] = xbc[L - SUBLANES:L, :]
        act = _silu(conv)
        xs[r] = act[:, :inner]
        bm[r] = act[:, inner:inner + LANES]
        cm[r] = act[:, inner + LANES:inner + 2 * LANES]
        dt[r] = _softplus(zxd_ref[rs, inner + conv_dim:] + dtb_ref[...])
        dta_hi, dta_lo = _split_bf16(dt[r] * a_ref[...])
        cum = jnp.dot(tril_ref[...], jnp.concatenate([dta_hi, dta_lo], axis=1), preferred_element_type=F32)
        a[r] = cum[:, :LANES] + cum[:, LANES:]
    a_t = {r: a[r].T for r in subs}
    dt_t = {r: dt[r].T for r in subs}
    bmb = {r: bm[r].astype(BF16) for r in subs}
    cb2 = {r: lax.dot_general(jnp.concatenate([jnp.where(low, cm[r], 0.0), jnp.where(low, 0.0, cm[r])],
                                              axis=0).astype(BF16), bmb[r], NT_DIMS, preferred_element_type=F32)
           for r in subs}
    cb_g = {(r, g): cb2[r][g * L:(g + 1) * L] for r in subs for g in range(M_GROUPS)}

    def weights(r, h):
        diff = jnp.broadcast_to(a[r][:, h:h + 1], (L, L)) - jnp.broadcast_to(a_t[r][h:h + 1, :], (L, L))
        decay = jnp.exp(jnp.where(causal, diff, -jnp.inf))
        return (cb_g[r, h // hpg] * decay * jnp.broadcast_to(dt_t[r][h:h + 1, :], (L, L))).astype(BF16)

    ws = {(r, h): weights(r, h) for r in subs for h in heads}
    xhalf = {(r, h): jnp.where(low if h % 2 == 0 else jnp.logical_not(low),
                               xs[r][:, (h // 2) * LANES:(h // 2 + 1) * LANES], 0.0).astype(BF16)
             for r in subs for h in heads}
    ycol = {(r, j): jnp.dot(jnp.concatenate([ws[r, 2 * j], ws[r, 2 * j + 1]], axis=1),
                            jnp.concatenate([xhalf[r, 2 * j], xhalf[r, 2 * j + 1]], axis=0),
                            preferred_element_type=F32) for r in subs for j in range(M_HEADS // 2)}
    y_intra = {r: jnp.concatenate([ycol[r, j] for j in range(M_HEADS // 2)], axis=1) for r in subs}
    ea = {r: jnp.exp(a[r]) for r in subs}
    ea_x = {r: _expand_heads(lambda h: jnp.broadcast_to(ea[r][:, h:h + 1], (L, LANES)), low) for r in subs}
    coef = {r: jnp.exp(a[r][L - 1:L, :] - a[r]) * dt[r] for r in subs}
    coef_x = {r: _expand_heads(lambda h: jnp.broadcast_to(coef[r][:, h:h + 1], (L, LANES)), low) for r in subs}
    cs = {r: jnp.dot(bm[r].T.astype(BF16), (xs[r] * coef_x[r]).astype(BF16), preferred_element_type=F32)
          for r in subs}
    dec_x = {}
    for r in subs:
        ea_last = jnp.exp(a[r][L - 1:L, :])
        dec_x[r] = _expand_heads(lambda h: jnp.broadcast_to(ea_last[:, h:h + 1], (1, LANES)), low1)
    gate = {r: _silu(zxd_ref[r * L:(r + 1) * L, 0:inner]) for r in subs}
    for r in subs:
        st = st_ref[...]
        y = y_intra[r] + jnp.dot(cm[r].astype(BF16), st.astype(BF16), preferred_element_type=F32) * ea_x[r]
        st_ref[...] = st * dec_x[r] + jnp.where(srow == scol, cs[r], 0.0)
        y = (y + dskip_ref[...] * xs[r]) * gate[r]
        y_ref[r * L:(r + 1) * L, :] = (_rms(y) * nw_ref[...]).astype(y_ref.dtype)


def _ssd_prompt(zxd, conv_w, conv_b, dtb, a_neg, dskip, norm_w, tril, y_shared, layer, batch, seq, nsub):
    nc = seq // (nsub * CHUNK)
    width = zxd.shape[1]
    inner = M_HEADS * M_HEADDIM
    conv_dim = conv_w.shape[2]
    consts = (conv_w, conv_b, dtb, a_neg, dskip, norm_w)
    return pl.pallas_call(
        _ssd_prompt_body,
        out_shape=[jax.ShapeDtypeStruct(y_shared.shape, y_shared.dtype),
                   jax.ShapeDtypeStruct((batch, M_GROUPS * M_STATE, inner), F32)],
        grid=(batch, nc),
        in_specs=[pl.BlockSpec((nsub * CHUNK, width), lambda b, c: (b * nc + c, 0))]
        + [_layer_block(a, layer) for a in consts] + [pl.BlockSpec(tril.shape, lambda b, c: (0, 0)),
                                                      pl.BlockSpec(memory_space=pl.ANY)],
        out_specs=[pl.BlockSpec((nsub * CHUNK, inner), lambda b, c: (b * nc + c, 1)),
                   pl.BlockSpec((None, M_GROUPS * M_STATE, inner), lambda b, c: (b, 0, 0))],
        scratch_shapes=[pltpu.VMEM((SUBLANES + CHUNK, conv_dim), F32)],
        input_output_aliases={len(consts) + 2: 0},
        compiler_params=_params(("arbitrary", "arbitrary"), 16 * 1024 * 1024),
        name="ssd_prompt",
    )(zxd, *consts, tril, y_shared)


HG_LEVELS = tuple(2 ** i for i in range(7))
HG_FINE = tuple(b for b in HG_LEVELS if b < SUBLANES)


def _hgrn_sum_matrix(L):
    r = np.arange(L)
    tt, rr = np.meshgrid(r, r, indexing="ij")
    mats = []
    for b in HG_FINE:
        mid = (r // (2 * b)) * 2 * b + b - 1
        upper = (r // b) % 2 == 1
        up = (mid[:, None] < rr) & (rr <= tt)
        lo = (tt < rr) & (rr <= mid[:, None])
        mats.append(np.where(upper[:, None], up, lo))
    mats.append(rr <= tt)
    m = np.concatenate(mats, axis=0).astype(np.float32)
    return np.concatenate([m, m], axis=1)


def _hgrn_level_tables(L):
    r = np.arange(L)
    later = np.concatenate([np.repeat((((r // b) % 2) == 1)[:, None], LANES, axis=1) for b in HG_LEVELS], axis=0)
    level = np.full((L, L), -1, np.int32)
    for lvl, b in enumerate(HG_LEVELS):
        t_later = ((r // b) % 2 == 1)[:, None]
        s_earlier = ((r // b) % 2 == 0)[None, :]
        same_parent = (r // (2 * b))[:, None] == (r // (2 * b))[None, :]
        level[t_later & s_earlier & same_parent] = lvl
    level[r, r] = len(HG_LEVELS)
    return np.where(later, 1.0, -1.0).astype(np.float32), level


def _hgrn_prompt_body(x_ref, sums_ref, sign_ref, level_ref, lb_ref, nw_ref, y_shared_ref, y_ref, st_ref):
    del y_shared_ref
    c = pl.program_id(1)
    L = CHUNK
    width = HG_HEADS * LANES
    nlev, nfine = len(HG_LEVELS), len(HG_FINE)

    @pl.when(c == 0)
    def _():
        st_ref[...] = jnp.zeros_like(st_ref)

    level = level_ref[...]
    nsub = x_ref.shape[0] // L
    heads = range(HG_HEADS)
    ch = [(r, h) for r in range(nsub) for h in heads]
    part = lambda j, r, h: x_ref[r * L:(r + 1) * L, j * width + h * LANES:j * width + (h + 1) * LANES]
    g2, kks = {}, {}
    for r, h in ch:
        sl = slice(h * LANES, (h + 1) * LANES)
        logf, kks[r, h] = _hgrn_decay(part(1, r, h), lb_ref[0:1, sl], lb_ref[1:2, sl], lb_ref[2:3, sl])
        g2[r, h] = logf * LOG2_E
    es = {}
    for r in range(nsub):
        for pair in range(HG_HEADS // 2):
            parts = [_split_bf16(g2[r, 2 * pair + k]) for k in range(2)]
            w = jnp.concatenate([jnp.concatenate([parts[0][0], parts[1][0]], axis=1),
                                 jnp.concatenate([parts[0][1], parts[1][1]], axis=1)], axis=0)
            e2 = jnp.dot(sums_ref[...], w, preferred_element_type=F32)
            es[r, 2 * pair], es[r, 2 * pair + 1] = e2[:, :LANES], e2[:, LANES:]
    qs = {k: _silu(part(0, *k)) for k in ch}
    bcs = {k: es[k][nfine * L:(nfine + 1) * L] for k in ch}
    atts = {k: jnp.where(level == nlev, jnp.sum(qs[k] * kks[k], axis=1, keepdims=True), 0.0) for k in ch}
    for lvl, b in enumerate(HG_LEVELS):
        for k in ch:
            q, kk, bc = qs[k], kks[k], bcs[k]
            if lvl < nfine:
                later = sign_ref[lvl * L:(lvl + 1) * L, :] > 0.0
                xk = jnp.exp2(es[k][lvl * L:(lvl + 1) * L]) * jnp.where(later, q, kk)
            else:
                blocks = []
                for p in range(0, L, 2 * b):
                    mid = bc[p + b - 1:p + b, :]
                    blocks.append(kk[p:p + b] * jnp.exp2(mid - bc[p:p + b]))
                    blocks.append(q[p + b:p + 2 * b] * jnp.exp2(bc[p + b:p + 2 * b] - mid))
                xk = jnp.concatenate(blocks, axis=0)
            xk = xk.astype(BF16)
            atts[k] = jnp.where(level == lvl, lax.dot_general(xk, xk, NT_DIMS, preferred_element_type=F32), atts[k])
    qds = {k: (qs[k] * jnp.exp2(bcs[k])).astype(BF16) for k in ch}
    kdts = {k: (kks[k] * jnp.exp2(bcs[k][L - 1:L, :] - bcs[k])).T.astype(BF16) for k in ch}
    dcols = {k: jnp.broadcast_to(jnp.exp2(bcs[k][L - 1:L, :]), (L, LANES)).T for k in ch}
    ivbs = {k: part(2, *k).astype(BF16) for k in ch}
    sogs = {k: _silu(part(3, *k)) for k in ch}
    for r in range(nsub):
        for h in heads:
            sl = slice(h * LANES, (h + 1) * LANES)
            st = st_ref[sl, :]
            o = jnp.dot(jnp.concatenate([atts[r, h].astype(BF16), qds[r, h]], axis=1),
                        jnp.concatenate([ivbs[r, h], st.astype(BF16)], axis=0), preferred_element_type=F32)
            st_ref[sl, :] = st * dcols[r, h] + jnp.dot(kdts[r, h], ivbs[r, h], preferred_element_type=F32)
            y_ref[r * L:(r + 1) * L, sl] = (_rms(o) * nw_ref[:, sl] * sogs[r, h]).astype(y_ref.dtype)


def _hgrn_prompt(xh, sums, sign, level, lb_rows, nw, y_shared, layer, batch, seq, nsub):
    rows = nsub * CHUNK
    nc = seq // rows
    width = HG_HEADS * LANES
    full = lambda a: pl.BlockSpec(a.shape, lambda b, c: (0,) * a.ndim)
    return pl.pallas_call(
        _hgrn_prompt_body,
        out_shape=[jax.ShapeDtypeStruct(y_shared.shape, y_shared.dtype),
                   jax.ShapeDtypeStruct((batch, width, LANES), F32)],
        grid=(batch, nc),
        in_specs=[pl.BlockSpec((rows, 4 * width), lambda b, c: (b * nc + c, 0)),
                  full(sums), full(sign), full(level), _layer_block(lb_rows, layer), _layer_block(nw, layer),
                  pl.BlockSpec(memory_space=pl.ANY)],
        out_specs=[pl.BlockSpec((rows, width), lambda b, c: (b * nc + c, 2)),
                   pl.BlockSpec((None, width, LANES), lambda b, c: (b, 0, 0))],
        input_output_aliases={6: 0},
        compiler_params=_params(("arbitrary", "arbitrary"), 16 * 1024 * 1024),
        name="hgrn_prompt",
    )(xh, sums, sign, level, lb_rows, nw, y_shared)


def _sample_attn_hgrn_body(q8_ref, kr_ref, vr_ref, kvc_ref, kt_ref, vt_ref, sink_ref,
                           xh_ref, s0_ref, lb_ref, hnw_ref, kt_stack_ref, vt_stack_ref, s_stack_ref,
                           oa_ref, ktn_ref, vtn_ref, yh_ref, sn_ref, *, n_prev_valid):
    del kt_stack_ref, vt_stack_ref, s_stack_ref
    nb = q8_ref.shape[0]
    width = HG_HEADS * LANES
    grp = ATT_HEADS // ATT_KV_HEADS
    row8 = lax.broadcasted_iota(jnp.int32, (SUBLANES, LANES), 0)
    lane8 = lax.broadcasted_iota(jnp.int32, (SUBLANES, LANES), 1)
    hrow = lax.broadcasted_iota(jnp.int32, (ATT_HEADS, 1), 0)
    slope = _alibi_slope_col(hrow)
    jj = lax.broadcasted_iota(jnp.int32, (ATT_HEADS, WINDOW), 1)
    cache_dist = (WINDOW - jj).astype(F32)
    cache_ok = jj >= WINDOW - n_prev_valid
    kv_half = (lane8 // HEAD_DIM) == (row8 // grp)
    last_lane = lax.broadcasted_iota(jnp.int32, (LANES, WINDOW), 1) == WINDOW - 1
    rowx = lax.broadcasted_iota(jnp.int32, (SUBLANES, width), 0)
    lanex = lax.broadcasted_iota(jnp.int32, (SUBLANES, width), 1)
    kvw = ATT_KV_HEADS * HEAD_DIM

    smp = range(nb)
    sink = sink_ref[:, 0:1]
    head_of_lane = lanex // LANES

    s_c = [jnp.dot(q8_ref[i].astype(BF16), kt_ref[i].astype(BF16), preferred_element_type=F32) * (HEAD_DIM ** -0.5)
           for i in smp]
    s_c = [jnp.where(cache_ok, s_c[i] - slope * cache_dist, -jnp.inf) for i in smp]
    s_n = [jnp.sum(q8_ref[i] * kr_ref[i], axis=1, keepdims=True) * (HEAD_DIM ** -0.5) for i in smp]
    m = [jnp.maximum(jnp.maximum(jnp.max(s_c[i], axis=1, keepdims=True), s_n[i]), sink) for i in smp]
    p_c = [jnp.exp(s_c[i] - m[i]) for i in smp]
    p_n = [jnp.exp(s_n[i] - m[i]) for i in smp]
    den = [jnp.sum(p_c[i], axis=1, keepdims=True) + p_n[i] + jnp.exp(sink - m[i]) for i in smp]
    o = [lax.dot_general((p_c[i] / den[i]).astype(BF16), vt_ref[i].astype(BF16), NT_DIMS,
                         preferred_element_type=F32) + (p_n[i] / den[i]) * vr_ref[i] for i in smp]
    for i in smp:
        oa_ref[i] = jnp.where(kv_half, o[i], 0.0)
        ktn_ref[i] = jnp.where(last_lane, kvc_ref[0:kvw, i:i + 1], pltpu.roll(kt_ref[i], WINDOW - 1, 1))
        vtn_ref[i] = jnp.where(last_lane, kvc_ref[kvw:2 * kvw, i:i + 1], pltpu.roll(vt_ref[i], WINDOW - 1, 1))

    q = [_silu(xh_ref[i][:, :width]) for i in smp]
    dk = [_hgrn_decay(xh_ref[i][:, width:2 * width], lb_ref[0:1, :], lb_ref[1:2, :], lb_ref[2:3, :]) for i in smp]
    f = [jnp.exp(dk[i][0]) for i in smp]
    kk = [dk[i][1] for i in smp]
    iv = [xh_ref[i][:, 2 * width:3 * width] for i in smp]
    lhs_q = [jnp.where(rowx == head_of_lane, jnp.broadcast_to(q[i] * f[i], (SUBLANES, width)), 0.0).astype(BF16)
             for i in smp]
    qs = [jnp.dot(lhs_q[i], s0_ref[i].astype(BF16), preferred_element_type=F32) for i in smp]
    for i in smp:
        qkk = q[i] * kk[i]
        o_parts = []
        for h in range(HG_HEADS):
            sl = slice(h * LANES, (h + 1) * LANES)
            att = jnp.sum(qkk[:, sl], axis=1, keepdims=True)
            o_parts.append(_rms(att * iv[i][:, sl] + qs[i][h:h + 1, :]))
        yh_ref[i] = jnp.concatenate(o_parts, axis=1) * hnw_ref[...] * _silu(xh_ref[i][:, 3 * width:])
    i8 = [jnp.concatenate([iv[i][:, h * LANES:(h + 1) * LANES] for h in range(HG_HEADS)]
                          + [jnp.zeros((SUBLANES - HG_HEADS, LANES), F32)], axis=0).astype(BF16) for i in smp]
    ones_rows45 = jnp.where((row8 == HG_HEADS) | (row8 == HG_HEADS + 1), 1.0, 0.0).astype(BF16)
    for i in smp:
        f_b = jnp.broadcast_to(f[i], (SUBLANES, width))
        f_hi = f_b.astype(BF16).astype(F32)
        lhs = jnp.where(rowx == head_of_lane, jnp.broadcast_to(kk[i], (SUBLANES, width)),
                        jnp.where(rowx == HG_HEADS, f_hi, jnp.where(rowx == HG_HEADS + 1, f_b - f_hi, 0.0)))
        both = lax.dot_general(lhs.astype(BF16), jnp.concatenate([i8[i], ones_rows45], axis=1), TN_DIMS,
                               preferred_element_type=F32)
        sn_ref[i] = s0_ref[i] * both[:, LANES:] + both[:, :LANES]


def _sample_attn_hgrn(q8, kr, vr, kvc, kt, vt, sink8, xh, s0, lb_rows, hnw, kt_stack, vt_stack, s_stack, layer, nb):
    n = q8.shape[0]
    per = lambda a: pl.BlockSpec((nb,) + a.shape[1:], lambda i: (i,) + (0,) * (a.ndim - 1))
    lay = lambda a: pl.BlockSpec((None, nb) + a.shape[2:], lambda i: (layer, i) + (0,) * (a.ndim - 2))
    hbm = pl.BlockSpec(memory_space=pl.ANY)
    width = HG_HEADS * LANES
    out_shape = [jax.ShapeDtypeStruct((n, ATT_HEADS, LANES), F32),
                 jax.ShapeDtypeStruct(kt_stack.shape, F32),
                 jax.ShapeDtypeStruct(vt_stack.shape, F32),
                 jax.ShapeDtypeStruct((n, 1, width), F32),
                 jax.ShapeDtypeStruct(s_stack.shape, F32)]
    return pl.pallas_call(
        functools.partial(_sample_attn_hgrn_body, n_prev_valid=min(WINDOW, PAST_LEN)),
        out_shape=out_shape,
        grid=(n // nb,),
        in_specs=[per(q8), per(kr), per(vr), pl.BlockSpec((None,) + kvc.shape[1:], lambda i: (i, 0, 0)),
                  lay(kt), lay(vt), _layer_block(sink8, layer), per(xh), lay(s0),
                  _layer_block(lb_rows, layer), _layer_block(hnw, layer), hbm, hbm, hbm],
        out_specs=[per(q8), lay(kt_stack), lay(vt_stack),
                   pl.BlockSpec((nb, 1, width), lambda i: (i, 0, 0)), lay(s_stack)],
        input_output_aliases={11: 1, 12: 2, 13: 4},
        compiler_params=_params(("parallel",), 4 * nb * (2 * kt.shape[2] + s0.shape[2]) * LANES * 4),
        name="sample_attn_hgrn",
    )(q8, kr, vr, kvc, kt, vt, sink8, xh, s0, lb_rows, hnw, kt_stack, vt_stack, s_stack)


def _ssd_sample_body(col_ref, cst_ref, h0_ref, cw_ref, cb_ref, dtb_ref, a_ref, dskip_ref, nw_ref, stack_ref,
                     ym_ref, hn_ref, xs_s, b_s, c_s, dt_s, da_s, yrow_s, y_s):
    del stack_ref
    h = pl.program_id(0)
    inner = M_HEADS * M_HEADDIM
    conv_dim = inner + 2 * M_GROUPS * M_STATE
    kvw2 = 2 * ATT_KV_HEADS * HEAD_DIM
    z0, x0, d0 = kvw2, kvw2 + inner, kvw2 + inner + conv_dim
    ns = col_ref.shape[1]

    @pl.when(h == 0)
    def _():
        conv = col_ref[x0:d0, :] * cw_ref[CONV_W - 1] + cb_ref[...]
        for w in range(CONV_W - 1):
            conv = conv + cst_ref[w] * cw_ref[w]
        act = _silu(conv)
        xs_s[...] = act[:inner]
        b_s[...] = act[inner:inner + M_GROUPS * M_STATE]
        c_s[...] = act[inner + M_GROUPS * M_STATE:]
        dt = _softplus(col_ref[d0:d0 + M_HEADS, :] + dtb_ref[...])
        da = jnp.exp(dt * a_ref[...])
        for hh in range(M_HEADS):
            dt_s[hh] = jnp.broadcast_to(dt[hh:hh + 1, :], (SUBLANES, ns))
            da_s[hh] = jnp.broadcast_to(da[hh:hh + 1, :], (SUBLANES, ns))

    hpg = M_HEADS // M_GROUPS
    x_h = xs_s[pl.ds(pl.multiple_of(h * M_HEADDIM, M_HEADDIM), M_HEADDIM), :]
    g0 = pl.multiple_of((h // hpg) * M_STATE, M_STATE)
    b_g = b_s[pl.ds(g0, M_STATE), :]
    c_g = c_s[pl.ds(g0, M_STATE), :]
    dt = dt_s[h][0:1, :]
    da = da_s[h][0:1, :]
    dtx = x_h * dt
    for p in range(M_HEADDIM):
        s_p = h0_ref[p * M_STATE:(p + 1) * M_STATE, :]
        yrow_s[p:p + 1, :] = jnp.sum(c_g * s_p, axis=0, keepdims=True)
        hn_ref[p * M_STATE:(p + 1) * M_STATE, :] = da * s_p + b_g * dtx[p:p + 1, :]
    cb = jnp.sum(c_g * b_g, axis=0, keepdims=True)
    y_s[pl.ds(pl.multiple_of(h * M_HEADDIM, M_HEADDIM), M_HEADDIM), :] = (
        da * yrow_s[...] + (dt * cb + dskip_ref[h][0:1, :]) * x_h)

    @pl.when(h == pl.num_programs(0) - 1)
    def _():
        y = y_s[...] * _silu(col_ref[z0:x0, :])
        y = y * lax.rsqrt(jnp.mean(y * y, axis=0, keepdims=True) + RMS_EPS) * nw_ref[...]
        for j in range(inner // LANES):
            ym_ref[:, j * LANES:(j + 1) * LANES] = y[j * LANES:(j + 1) * LANES, :].T


def _ssd_sample(col, cst, h0, cw, cb, dtb, a_neg, dskip, nw, stack, layer):
    ns = col.shape[1]
    inner = M_HEADS * M_HEADDIM
    state_block = pl.BlockSpec((None, None) + h0.shape[2:], lambda h: (layer, h, 0, 0))
    return pl.pallas_call(
        _ssd_sample_body,
        out_shape=[jax.ShapeDtypeStruct((ns, inner), F32), jax.ShapeDtypeStruct(stack.shape, F32)],
        grid=(M_HEADS,),
        in_specs=[pl.BlockSpec(col.shape, lambda h: (0, 0)), _layer_block(cst, layer), state_block]
        + [_layer_block(a, layer) for a in (cw, cb, dtb, a_neg, dskip, nw)] + [pl.BlockSpec(memory_space=pl.ANY)],
        out_specs=[pl.BlockSpec((ns, inner), lambda h: (0, 0)), state_block],
        scratch_shapes=[pltpu.VMEM((inner, ns), F32), pltpu.VMEM((M_GROUPS * M_STATE, ns), F32),
                        pltpu.VMEM((M_GROUPS * M_STATE, ns), F32), pltpu.VMEM((M_HEADS, SUBLANES, ns), F32),
                        pltpu.VMEM((M_HEADS, SUBLANES, ns), F32), pltpu.VMEM((M_HEADDIM, ns), F32),
                        pltpu.VMEM((inner, ns), F32)],
        input_output_aliases={9: 1},
        compiler_params=_params(("arbitrary",), 24 * 1024 * 1024),
        name="ssd_sample",
    )(col, cst, h0, cw, cb, dtb, a_neg, dskip, nw, stack)


def _row_tile(rows, largest=1024):
    for tm in (1024, 512, 256, 128):
        if tm <= largest and rows % tm == 0:
            return tm
    return rows


def kernel(x_prompt, x_sample, cache_swa_k, cache_swa_v, state_conv, state_ssm, state_hgrn, ln1_g, ln1_b, ffn1_wg, ffn1_wu, ffn1_wd, w_in, b_in, att_sinks, conv_w, conv_b, dt_bias, a_log, d_skip, ssm_norm_w, hg_lb_logits, hg_norm_w, w_br_att, w_br_ssm, w_br_hg, w_out, ln2_g, ln2_b, ffn2_wg, ffn2_wu, ffn2_wd, ln3_g, ln3_b):
    bp, seq, d = x_prompt.shape
    ns = x_sample.shape[0]
    depth = w_in.shape[0]
    assert seq % CHUNK == 0 and x_sample.shape[1] == 1
    qw = ATT_HEADS * HEAD_DIM
    kvw = ATT_KV_HEADS * HEAD_DIM
    inner = M_HEADS * M_HEADDIM
    conv_dim = inner + 2 * M_GROUPS * M_STATE
    width = HG_HEADS * LANES
    hpg = M_HEADS // M_GROUPS

    bf = lambda a: a.astype(BF16)
    row3 = lambda a: a.reshape(depth, 1, a.shape[-1])
    o_z = qw + 2 * kvw
    o_dt = o_z + inner + conv_dim
    o_h = o_dt + M_HEADS
    o_g = o_h + 4 * width
    w_in_t = jnp.swapaxes(w_in, 1, 2)
    dt_pad = jnp.pad(w_in_t[:, o_dt:o_h, :], ((0, 0), (0, LANES - M_HEADS), (0, 0)))
    w_rows = bf(jnp.concatenate([w_in_t[:, :o_dt, :], dt_pad, w_in_t[:, o_h:o_g, :]], axis=1))
    b_flat = jnp.concatenate([b_in[:, :o_dt], jnp.pad(b_in[:, o_dt:o_h], ((0, 0), (0, LANES - M_HEADS))),
                              b_in[:, o_h:o_g]], axis=1)
    b_rows = row3(b_flat)
    col_rows = (qw, o_dt + LANES)
    b_cols = b_flat[:, col_rows[0]:col_rows[1], None]
    in_widths = (qw + 2 * kvw, inner + conv_dim + LANES, 4 * width)
    ffn1 = (ffn1_wg, ffn1_wu, ffn1_wd, row3(ln1_g), row3(ln1_b))
    ffn2 = (ffn2_wg, ffn2_wu, ffn2_wd, row3(ln3_g), row3(ln3_b))
    w_merge = bf(jnp.concatenate([w_in_t[:, o_g:, :], w_br_att, w_br_ssm, w_br_hg, w_out], axis=1))
    merge_w = (w_merge, row3(b_in[:, o_g:]), row3(ln2_g), row3(ln2_b))

    a_neg = -jnp.exp(a_log.astype(F32))
    pad_h = lambda a: row3(jnp.pad(a, ((0, 0), (0, LANES - M_HEADS))))
    ssd_rows = (conv_w, row3(conv_b), pad_h(dt_bias), pad_h(a_neg), row3(jnp.repeat(d_skip, M_HEADDIM, axis=1)),
                row3(ssm_norm_w))
    lb_all = jnp.cumsum(jax.nn.softmax(hg_lb_logits.astype(F32), axis=0), axis=0)
    lb_all = lb_all - lb_all[0]
    lb_rows = jnp.stack([jnp.log(lb_all), jnp.log1p(-lb_all), 1.0 - lb_all], axis=1)
    lb_rows = jnp.pad(lb_rows, ((0, 0), (0, SUBLANES - 3), (0, 0)))
    hnw = row3(jnp.tile(hg_norm_w, (1, HG_HEADS)))
    sums = jnp.asarray(_hgrn_sum_matrix(CHUNK), BF16)
    sign_np, level_np = _hgrn_level_tables(CHUNK)
    hg_sign, hg_level = jnp.asarray(sign_np), jnp.asarray(level_np)
    att_bias = jnp.asarray(_attn_bias(CHUNK))
    tril = jnp.asarray(np.tril(np.ones((CHUNK, CHUNK), np.float32)), BF16)
    sink8 = jnp.broadcast_to(att_sinks[:, :, None], (depth, ATT_HEADS, LANES))

    tm_p = _row_tile(bp * seq)
    tm_f = _row_tile(bp * seq, 512)
    tm_s = _row_tile(ns)
    nb_s = max(k for k in (2 * SUBLANES, SUBLANES, 1) if ns % k == 0)
    lanes_b = lambda a: jnp.broadcast_to(a[..., None], a.shape + (ns,))
    kt_in = cache_swa_k.transpose(0, 1, 3, 4, 2).reshape(depth, ns, kvw, WINDOW)
    vt_in = cache_swa_v.transpose(0, 1, 3, 4, 2).reshape(depth, ns, kvw, WINDOW)
    ssm_in = state_ssm.transpose(0, 2, 3, 4, 1).reshape(depth, M_HEADS, M_HEADDIM * M_STATE, ns)
    conv_in = state_conv.transpose(0, 2, 3, 1)
    hg_in = state_hgrn.reshape(depth, ns, width, LANES)
    kt_out, vt_out = jnp.zeros(kt_in.shape, F32), jnp.zeros(vt_in.shape, F32)
    ssm_out, hg_out = jnp.zeros(ssm_in.shape, F32), jnp.zeros(hg_in.shape, F32)
    cw_b, cb_b = lanes_b(conv_w), lanes_b(conv_b)
    dtb_b, a_b, nw_b = lanes_b(dt_bias), lanes_b(a_neg), lanes_b(ssm_norm_w)
    dskip_b = jnp.broadcast_to(d_skip[:, :, None, None], (depth, M_HEADS, SUBLANES, ns))
    att_sub = max(k for k in (4, 2, 1) if (seq // CHUNK) % k == 0)
    ssd_sub = max(k for k in (4, 2, 1) if (seq // CHUNK) % k == 0)
    scan_sub = max(k for k in (2, 1) if (seq // CHUNK) % k == 0)

    xp = x_prompt.reshape(bp * seq, d)
    xs = x_sample.reshape(ns, d)
    p_states, s_conv = [], []
    for l in range(depth):
        xp = _ffn_ln(xp, *ffn1, l, tm_f)
        qkv, zxd, xh = _inproj(xp, w_rows, b_rows, l, tm_p, in_widths)
        y3 = _attn_prompt(qkv, att_sinks, att_bias, l, bp, seq, att_sub)
        y3, st_ssm = _ssd_prompt(zxd, *ssd_rows, tril, y3, l, bp, seq, ssd_sub)
        y3, st_hg = _hgrn_prompt(xh, sums, hg_sign, hg_level, lb_rows, hnw, y3, l, bp, seq, scan_sub)
        xp = _merge(xp, y3, *merge_w, l, tm_p)
        xp = _ffn_ln(xp, *ffn2, l, tm_f)
        qkv3 = qkv.reshape(bp, seq, qw + 2 * kvw)
        p_k = qkv3[:, seq - WINDOW:, qw:qw + kvw].reshape(bp, WINDOW, ATT_KV_HEADS, HEAD_DIM)
        p_v = qkv3[:, seq - WINDOW:, qw + kvw:].reshape(bp, WINDOW, ATT_KV_HEADS, HEAD_DIM)
        p_conv = zxd.reshape(bp, seq, -1)[:, seq - (CONV_W - 1):, inner:inner + conv_dim]
        st6 = st_ssm.reshape(bp, M_GROUPS, M_STATE, M_GROUPS, hpg, M_HEADDIM)
        p_ssm = jnp.stack([st6[:, g, :, g] for g in range(M_GROUPS)], axis=1)
        p_ssm = p_ssm.transpose(0, 1, 3, 4, 2).reshape(bp, M_HEADS, M_HEADDIM, M_STATE)
        p_hg = st_hg.reshape(bp, HG_HEADS, LANES, LANES)
        p_states.append((p_k, p_v, p_conv, p_ssm, p_hg))

        xs = _ffn_ln(xs, *ffn1, l, tm_s)
        qkv_s, zx_s, xh_s, col_s = _inproj_sample(xs, w_rows, b_rows, b_cols, l, in_widths, col_rows)
        q4 = qkv_s[:, :qw].reshape(ns, ATT_KV_HEADS, ATT_HEADS // ATT_KV_HEADS, HEAD_DIM)
        zq = jnp.zeros_like(q4[:, 0])
        q8 = jnp.concatenate([jnp.concatenate([q4[:, 0], zq], axis=-1),
                              jnp.concatenate([zq, q4[:, 1]], axis=-1)], axis=1)
        kv_cols = col_s[:2 * kvw].reshape(2 * kvw, ns // nb_s, nb_s).transpose(1, 0, 2)
        oa, kt_out, vt_out, yh_s, hg_out = _sample_attn_hgrn(
            q8, qkv_s[:, None, qw:qw + kvw], qkv_s[:, None, qw + kvw:], kv_cols, kt_in, vt_in, sink8,
            xh_s[:, None, :], hg_in, lb_rows, hnw, kt_out, vt_out, hg_out, l, nb_s)
        ym_s, ssm_out = _ssd_sample(col_s, conv_in, ssm_in, cw_b, cb_b, dtb_b, a_b, dskip_b, nw_b, ssm_out, l)
        grp = ATT_HEADS // ATT_KV_HEADS
        y3_s = jnp.concatenate([oa[:, :grp, :HEAD_DIM].reshape(ns, grp * HEAD_DIM),
                                oa[:, grp:, HEAD_DIM:].reshape(ns, grp * HEAD_DIM),
                                ym_s, yh_s.reshape(ns, width)], axis=1)
        xs = _merge(xs, y3_s, *merge_w, l, tm_s)
        xs = _ffn_ln(xs, *ffn2, l, tm_s)
        s_conv.append(jnp.concatenate([state_conv[l][:, 1:], zx_s[:, None, inner:inner + conv_dim]], axis=1))

    outs_p = [jnp.stack(t) for t in zip(*p_states)]
    unwind = lambda t: t.reshape(depth, ns, ATT_KV_HEADS, HEAD_DIM, WINDOW).transpose(0, 1, 4, 2, 3)
    s_ssm = ssm_out.reshape(depth, M_HEADS, M_HEADDIM, M_STATE, ns).transpose(0, 4, 1, 2, 3)
    outs_s = [unwind(kt_out), unwind(vt_out), jnp.stack(s_conv), s_ssm,
              hg_out.reshape(depth, ns, HG_HEADS, LANES, LANES)]
    return (xp.reshape(bp, seq, d), xs.reshape(ns, 1, d), *outs_p, *outs_s)
```
